```python
import math
import jax, jax.numpy as jnp
from jax import lax
import numpy as np

D_MODEL = 4096
BATCH = 4
SEQ = 2048
DEPTH = 1
DEC_BATCH = 32
DEC_SEQ = 8
PAST_LEN = 8192
PAGE_SIZE = 128

MIX_WIDTH = D_MODEL
ATTN_WIDTH = MIX_WIDTH // 2
POOL_WIDTH = MIX_WIDTH - ATTN_WIDTH
HEAD_DIM = 128
N_HEADS = ATTN_WIDTH // HEAD_DIM
N_KV_HEADS = 4
KV_WIDTH = N_KV_HEADS * HEAD_DIM
IDX_HEADS = 32
IDX_DIM = 128
TOPK_MAX = 256
POOL_WINDOWS = (2, 4, 8, 16)
N_POOL_GROUPS = 4
POOL_GROUP = POOL_WIDTH // N_POOL_GROUPS
POOL_STATE = 15
N_BUCKETS = 32
MAX_DISTANCE = 128
RMS_EPS = 1e-6
Q_BLOCK = 128
SPLITS = (ATTN_WIDTH, KV_WIDTH, KV_WIDTH, ATTN_WIDTH, IDX_HEADS * IDX_DIM, IDX_HEADS, IDX_DIM, POOL_WIDTH, POOL_WIDTH)
IN_WIDTH = ATTN_WIDTH * 2 + KV_WIDTH * 2 + IDX_HEADS * IDX_DIM + IDX_HEADS + IDX_DIM + POOL_WIDTH * 2

kernel_name = "hybrid_dsa_pool_decoder_step"

F32 = jnp.float32


def rmsnorm(x, g):
    xf = x.astype(F32)
    var = jnp.mean(xf * xf, axis=-1, keepdims=True)
    return (xf * lax.rsqrt(var + RMS_EPS) * g.astype(F32)).astype(x.dtype)


def project(x, ln_g, w_in):
    B, T, _ = x.shape
    h = jnp.einsum('btd,dn->btn', rmsnorm(x, ln_g), w_in)
    parts = []
    off = 0
    for width in SPLITS:
        parts.append(h[..., off:off + width])
        off += width
    q, k, v, g_a, qi, wi, ki, u, g_b = parts
    q = q.reshape(B, T, N_HEADS, HEAD_DIM)
    k = k.reshape(B, T, N_KV_HEADS, HEAD_DIM)
    v = v.reshape(B, T, N_KV_HEADS, HEAD_DIM)
    qi = qi.reshape(B, T, IDX_HEADS, IDX_DIM) * (IDX_DIM ** -0.5)
    wi = wi * (IDX_HEADS ** -0.5)
    return q, k, v, g_a, qi, wi, ki, u, g_b


def t5_bucket(dist):
    max_exact = N_BUCKETS // 2
    d = jnp.maximum(dist, 0)
    df = jnp.maximum(d.astype(F32), 1.0)
    large = max_exact + (jnp.log(df / max_exact) / math.log(MAX_DISTANCE / max_exact)
                         * (N_BUCKETS - max_exact)).astype(jnp.int32)
    large = jnp.minimum(large, N_BUCKETS - 1)
    return jnp.where(d < max_exact, d, large)


def indexer_scores(qi, wi, ki):
    s = jnp.einsum('bthd,bsd->bths', qi, ki, preferred_element_type=F32)
    return jnp.einsum('bth,bths->bts', wi.astype(F32), jax.nn.relu(s))


def sparse_attend(q, k_sel, v_sel, sel_pos, q_pos, valid, rel_bias):
    B, T, H, D = q.shape
    K = sel_pos.shape[-1]
    R = H // N_KV_HEADS
    qg = q.reshape(B, T, N_KV_HEADS, R, D)
    logits = jnp.einsum('btgrd,btkgd->btgrk', qg, k_sel, preferred_element_type=F32) * (D ** -0.5)
    bucket = t5_bucket(q_pos[None, :, None] - sel_pos)
    bias = rel_bias[bucket].astype(F32).reshape(B, T, K, N_KV_HEADS, R)
    logits = logits + jnp.transpose(bias, (0, 1, 3, 4, 2))
    logits = jnp.where(valid[:, :, None, None, :], logits, -1e30)
    p = jax.nn.softmax(logits, axis=-1)
    o = jnp.einsum('btgrk,btkgd->btgrd', p.astype(v_sel.dtype), v_sel)
    return o.reshape(B, T, H * D)


def gather_rows(rows, idx):
    return jax.vmap(lambda r, i: r[i])(rows, idx)


def prompt_attention(q, k, v, qi, wi, ki, rel_bias):
    B, S = q.shape[:2]
    topk = min(TOPK_MAX, S // 4)
    n_blk = S // Q_BLOCK
    key_pos = jnp.arange(S)

    def block(i):
        t0 = i * Q_BLOCK
        qb = lax.dynamic_slice_in_dim(q, t0, Q_BLOCK, axis=1)
        qib = lax.dynamic_slice_in_dim(qi, t0, Q_BLOCK, axis=1)
        wib = lax.dynamic_slice_in_dim(wi, t0, Q_BLOCK, axis=1)
        q_pos = t0 + jnp.arange(Q_BLOCK)
        sc = indexer_scores(qib, wib, ki)
        sc = jnp.where((key_pos[None, :] <= q_pos[:, None])[None], sc, -jnp.inf)
        _, sel = lax.top_k(sc, topk)
        valid = sel <= q_pos[None, :, None]
        return sparse_attend(qb, gather_rows(k, sel), gather_rows(v, sel), sel, q_pos, valid, rel_bias)

    out = lax.map(block, jnp.arange(n_blk))
    return jnp.transpose(out, (1, 0, 2, 3)).reshape(B, S, ATTN_WIDTH)


def sample_attention(q, k_new, v_new, qi, wi, ki_new, cache_k, cache_v, cache_idx_k, page_table, rel_bias):
    Bd, T = q.shape[:2]
    n_pages = page_table.shape[1]
    past = n_pages * PAGE_SIZE
    L = past + T
    topk = min(TOPK_MAX, L // 4)
    past_ki = cache_idx_k[page_table].reshape(Bd, past, IDX_DIM)
    ki_all = jnp.concatenate([past_ki, ki_new.astype(past_ki.dtype)], axis=1)
    q_pos = past + jnp.arange(T)
    key_pos = jnp.arange(L)
    sc = indexer_scores(qi, wi, ki_all)
    sc = jnp.where((key_pos[None, :] <= q_pos[:, None])[None], sc, -jnp.inf)
    _, sel = lax.top_k(sc, topk)
    valid = sel <= q_pos[None, :, None]
    in_past = (sel < past)[..., None, None]
    ps = jnp.minimum(sel, past - 1)
    phys = gather_rows(page_table, ps // PAGE_SIZE)
    row = ps % PAGE_SIZE
    ns = jnp.clip(sel - past, 0, T - 1)
    k_sel = jnp.where(in_past, cache_k[phys, row], gather_rows(k_new, ns).astype(cache_k.dtype))
    v_sel = jnp.where(in_past, cache_v[phys, row], gather_rows(v_new, ns).astype(cache_v.dtype))
    return sparse_attend(q, k_sel, v_sel, sel, q_pos, valid, rel_bias)


def multiscale_pool(u_full, first_pos, n_ctx, w_pool, pool_scale):
    B, N, W = u_full.shape
    T = N - n_ctx
    uf = u_full.astype(F32)
    cs = jnp.concatenate([jnp.zeros((B, 1, W), F32), jnp.cumsum(uf, axis=1)], axis=1)
    j = n_ctx + jnp.arange(T)
    pos = first_pos + j
    outs = []
    for g, win in enumerate(POOL_WINDOWS):
        sl = slice(g * POOL_GROUP, (g + 1) * POOL_GROUP)
        lo = jnp.maximum(j + 1 - win, 0)
        cnt = jnp.minimum(pos + 1, win).astype(F32)
        mean = (cs[:, j + 1, sl] - cs[:, lo, sl]) / cnt[None, :, None]
        d = mean - uf[:, n_ctx:, sl]
        outs.append(jnp.einsum('btc,cd->btd', d, w_pool[g].astype(F32)))
    return (jnp.concatenate(outs, axis=-1) * pool_scale.astype(F32)).astype(u_full.dtype)


def merge(x, o_a, g_a, o_b, g_b, w_out):
    mix = jnp.concatenate([o_a * jax.nn.silu(g_a), o_b * jax.nn.silu(g_b)], axis=-1)
    return x + jnp.einsum('btm,md->btd', mix.astype(x.dtype), w_out)


def setup_inputs(seed: int = 0) -> dict:
    key = jax.random.key(seed)
    ks = jax.random.split(key, 16)
    n_pages = PAST_LEN // PAGE_SIZE
    n_used = DEC_BATCH * n_pages
    n_phys = n_used + (n_used + 3) // 4
    page_table = jax.random.permutation(ks[0], n_phys)[:n_used].reshape(DEC_BATCH, n_pages).astype(jnp.int32)
    nrm = jax.random.normal
    return {
        "x_prompt": nrm(ks[1], (BATCH, SEQ, D_MODEL), F32),
        "x_sample": nrm(ks[2], (DEC_BATCH, DEC_SEQ, D_MODEL), F32),
        "cache_k": nrm(ks[3], (DEPTH, n_phys, PAGE_SIZE, N_KV_HEADS, HEAD_DIM), F32),
        "cache_v": nrm(ks[4], (DEPTH, n_phys, PAGE_SIZE, N_KV_HEADS, HEAD_DIM), F32),
        "cache_idx_k": nrm(ks[5], (DEPTH, n_phys, PAGE_SIZE, IDX_DIM), F32),
        "state_pool": nrm(ks[6], (DEPTH, DEC_BATCH, POOL_STATE, POOL_WIDTH), F32),
        "page_table": page_table,
        "rel_bias": 0.5 * nrm(ks[7], (N_BUCKETS, N_HEADS), F32),
        "ln_g": 1.0 + 0.1 * nrm(ks[8], (DEPTH, D_MODEL), F32),
        "w_in": nrm(ks[9], (DEPTH, D_MODEL, IN_WIDTH), F32) * (D_MODEL ** -0.5),
        "w_pool": nrm(ks[10], (DEPTH, N_POOL_GROUPS, POOL_GROUP, POOL_GROUP), F32) * (POOL_GROUP ** -0.5),
        "pool_scale": 1.0 + 0.1 * nrm(ks[11], (DEPTH, POOL_WIDTH), F32),
        "w_out": nrm(ks[12], (DEPTH, MIX_WIDTH, D_MODEL), F32) * (MIX_WIDTH ** -0.5),
        "final_g": 1.0 + 0.1 * nrm(ks[13], (D_MODEL,), F32),
    }


def reference(x_prompt, x_sample, cache_k, cache_v, cache_idx_k, state_pool, page_table,
              rel_bias, ln_g, w_in, w_pool, pool_scale, w_out, final_g):
    past = page_table.shape[1] * PAGE_SIZE
    hp, hs = x_prompt, x_sample
    kp, vp, ikp, pp = [], [], [], []
    ksm, vsm, iks, psm = [], [], [], []
    for layer in range(DEPTH):
        q, k, v, g_a, qi, wi, ki, u, g_b = project(hp, ln_g[layer], w_in[layer])
        o_a = prompt_attention(q, k, v, qi, wi, ki, rel_bias)
        o_b = multiscale_pool(u, 0, 0, w_pool[layer], pool_scale[layer])
        hp = merge(hp, o_a, g_a, o_b, g_b, w_out[layer])
        kp.append(k)
        vp.append(v)
        ikp.append(ki)
        pp.append(u[:, -POOL_STATE:])
        q, k, v, g_a, qi, wi, ki, u, g_b = project(hs, ln_g[layer], w_in[layer])
        o_a = sample_attention(q, k, v, qi, wi, ki, cache_k[layer], cache_v[layer],
                               cache_idx_k[layer], page_table, rel_bias)
        u_full = jnp.concatenate([state_pool[layer].astype(u.dtype), u], axis=1)
        o_b = multiscale_pool(u_full, past - POOL_STATE, POOL_STATE, w_pool[layer], pool_scale[layer])
        hs = merge(hs, o_a, g_a, o_b, g_b, w_out[layer])
        ksm.append(k)
        vsm.append(v)
        iks.append(ki)
        psm.append(u_full[:, -POOL_STATE:])
    y_prompt = rmsnorm(hp, final_g)
    y_sample = rmsnorm(hs, final_g)
    return (y_prompt, y_sample, jnp.stack(kp), jnp.stack(vp), jnp.stack(ikp), jnp.stack(pp),
            jnp.stack(ksm), jnp.stack(vsm), jnp.stack(iks), jnp.stack(psm))
```

```python
import functools
import math

import jax
import jax.numpy as jnp
from jax import lax
from jax.experimental import pallas as pl
from jax.experimental.pallas import tpu as pltpu

F32 = jnp.float32
BF16 = jnp.bfloat16
I32 = jnp.int32

_TOPK_MAX = 256
_POOL_WINDOWS = (2, 4, 8, 16)
_POOL_HALO = 16
_MAX_DISTANCE = 128
_RMS_EPS = 1e-6
_NEG = -1e30
_INT_MIN = -(2 ** 31)
_LANES = 128
_VMEM_LIMIT = 56 * 1024 * 1024
_MXU_DTYPE = BF16


def _cparams(n_axes):
    return pltpu.CompilerParams(dimension_semantics=("arbitrary",) * n_axes,
                                vmem_limit_bytes=_VMEM_LIMIT)


def _nt_dot(a, b):
    return lax.dot_general(a, b, (((1,), (1,)), ((), ())), preferred_element_type=F32)


def _sort_key(x):
    bits = lax.bitcast_convert_type(x + 0.0, I32)
    return bits ^ ((bits >> 31) & 0x7FFFFFFF)


def _rmsnorm_body(x_ref, g_ref, o_ref):
    x = x_ref[...].astype(F32)
    var = jnp.mean(x * x, axis=-1, keepdims=True)
    o_ref[...] = (x * lax.rsqrt(var + _RMS_EPS) * g_ref[...]).astype(o_ref.dtype)


def _rmsnorm(x2d, g, out_dtype):
    m, d = x2d.shape
    tm = min(256, m)
    return pl.pallas_call(
        _rmsnorm_body,
        grid=(m // tm,),
        in_specs=[pl.BlockSpec((tm, d), lambda i: (i, 0)), pl.BlockSpec((1, d), lambda i: (0, 0))],
        out_specs=pl.BlockSpec((tm, d), lambda i: (i, 0)),
        out_shape=jax.ShapeDtypeStruct((m, d), out_dtype),
        compiler_params=_cparams(1),
        name="rmsnorm",
    )(x2d, g.reshape(1, d).astype(F32))


def _proj_body(xp_ref, xs_ref, w_ref, *out_refs, n_p, epi_p, epi_s):
    w = w_ref[...]
    epi_p(jnp.dot(xp_ref[...], w, preferred_element_type=F32), out_refs[:n_p])

    @pl.when(pl.program_id(1) == 0)
    def _():
        epi_s(jnp.dot(xs_ref[...], w, preferred_element_type=F32), out_refs[n_p:])


def _proj_call(xn_p, xn_s, w, tn, tm, outs_p, outs_s, epi_p, epi_s, name):
    mp, d = xn_p.shape
    ms = xn_s.shape[0]
    ncols = w.shape[1]
    outs = list(outs_p) + list(outs_s)
    return pl.pallas_call(
        functools.partial(_proj_body, n_p=len(outs_p), epi_p=epi_p, epi_s=epi_s),
        grid=(ncols // tn, mp // tm),
        in_specs=[pl.BlockSpec((tm, d), lambda j, i: (i, 0)),
                  pl.BlockSpec((ms, d), lambda j, i: (0, 0)),
                  pl.BlockSpec((d, tn), lambda j, i: (0, j))],
        out_specs=[pl.BlockSpec(o[2], o[3]) for o in outs],
        out_shape=[jax.ShapeDtypeStruct(o[0], o[1]) for o in outs],
        compiler_params=_cparams(2),
        name=name,
    )(xn_p, xn_s, w)


def _row_major(m, ncols, tm, tn, dtype, sample):
    if sample:
        return ((m, ncols), dtype, (m, tn), lambda j, i: (0, j))
    return ((m, ncols), dtype, (tm, tn), lambda j, i: (i, j))


def _head_major(m, ncols, tm, tn, dtype, sample):
    nh, hb = ncols // _LANES, tn // _LANES
    if sample:
        return ((nh, m, _LANES), dtype, (hb, m, _LANES), lambda j, i: (j, 0, 0))
    return ((nh, m, _LANES), dtype, (hb, tm, _LANES), lambda j, i: (j, i, 0))


def _store_heads(ref, val):
    for hh in range(ref.shape[0]):
        ref[hh] = val[:, hh * _LANES:(hh + 1) * _LANES].astype(ref.dtype)


def _silu(x):
    return x * jax.nn.sigmoid(x)


def _bias_tables_body(rb_ref, tp_ref, ts_ref, *, n_buckets, tq, t_dec, page):
    h = pl.program_id(0)
    max_exact = n_buckets // 2

    def table(dist):
        d = jnp.maximum(dist, 0)
        df = jnp.maximum(d.astype(F32), 1.0)
        large = max_exact + (jnp.log(df / max_exact) / math.log(_MAX_DISTANCE / max_exact)
                             * (n_buckets - max_exact)).astype(I32)
        bucket = jnp.where(d < max_exact, d, jnp.minimum(large, n_buckets - 1))
        out = jnp.zeros(dist.shape, F32)
        for b in range(n_buckets):
            out = jnp.where(bucket == b, rb_ref[b, h], out)
        return out

    rp = lax.broadcasted_iota(I32, (tq, _LANES), 0)
    cp = lax.broadcasted_iota(I32, (tq, _LANES), 1)
    tp_ref[0, 0] = table(rp - cp)
    tp_ref[0, 1] = table(rp - cp + tq)
    rs = lax.broadcasted_iota(I32, (t_dec, _LANES), 0)
    cs = lax.broadcasted_iota(I32, (t_dec, _LANES), 1)
    ts_ref[0, 0] = table(rs - cs + page)
    ts_ref[0, 1] = table(rs - cs)


def _bias_tables(rel_bias, tq, t_dec, page):
    n_buckets, n_heads = rel_bias.shape
    return pl.pallas_call(
        functools.partial(_bias_tables_body, n_buckets=n_buckets, tq=tq, t_dec=t_dec, page=page),
        grid=(n_heads,),
        in_specs=[pl.BlockSpec(memory_space=pltpu.SMEM)],
        out_specs=[pl.BlockSpec((1, 2, tq, _LANES), lambda h: (h, 0, 0, 0)),
                   pl.BlockSpec((1, 2, t_dec, _LANES), lambda h: (h, 0, 0, 0))],
        out_shape=[jax.ShapeDtypeStruct((n_heads, 2, tq, _LANES), F32),
                   jax.ShapeDtypeStruct((n_heads, 2, t_dec, _LANES), F32)],
        compiler_params=_cparams(1),
        name="bias_tables",
    )(rel_bias.astype(F32))


def _kth_largest_rows(keys_ref, n_rows, n_cols, k):
    def step(it, thr):
        cand = thr + jnp.left_shift(jnp.int32(1), 31 - it)
        cnt = jnp.sum((keys_ref[0:n_rows, :] >= cand).astype(F32), axis=0, keepdims=True)
        return jnp.where(cnt >= k, cand, thr)

    return lax.fori_loop(0, 32, step, jnp.full((1, n_cols), _INT_MIN, I32))


def _prompt_attn_body(rb_ref, qi_ref, wk_ref, ki_ref, q_ref, k_ref, v_ref, sg_ref, tb_ref, o_ref,
                      wt_ref, sc_ref, keys_ref, maskt_ref, mask_ref, lg_ref, p_ref, linv_ref,
                      *, tq, n_q, topk, idx_heads, n_kv, rep, n_buckets, kchunk):
    i = pl.program_id(1)
    hd = q_ref.shape[-1]
    hgrp = 4

    def scores(L):
        wt_ref[...] = wk_ref[...].T
        sc_ref[0:L, :] = jnp.zeros((L, tq), F32)

        def hg_body(hg, carry):
            qi4 = qi_ref[pl.ds(hg * hgrp, hgrp)].reshape(hgrp * tq, -1)
            for c0 in range(0, L, kchunk):
                cw = min(kchunk, L - c0)
                s = _nt_dot(ki_ref[c0:c0 + cw, :], qi4)
                a = sc_ref[c0:c0 + cw, :]
                for r in range(hgrp):
                    w_row = wt_ref[pl.ds(hg * hgrp + r, 1), :]
                    a = a + jnp.maximum(s[:, r * tq:(r + 1) * tq], 0.0) * w_row
                sc_ref[c0:c0 + cw, :] = a
            return carry

        lax.fori_loop(0, idx_heads // hgrp, hg_body, 0)

    def select(L):
        row = lax.broadcasted_iota(I32, (L, tq), 0)
        col = lax.broadcasted_iota(I32, (L, tq), 1)
        causal = row <= col + (L - tq)
        if L <= topk:
            maskt_ref[0:L, :] = causal.astype(F32)
        else:
            keys_ref[0:L, :] = jnp.where(causal, _sort_key(sc_ref[0:L, :]), _INT_MIN)
            thr = _kth_largest_rows(keys_ref, L, tq, topk)
            keys = keys_ref[0:L, :]
            cnt_gt = jnp.sum((keys > thr).astype(F32), axis=0, keepdims=True)
            cnt_eq = jnp.sum((keys == thr).astype(F32), axis=0, keepdims=True)
            need = topk - cnt_gt
            maskt_ref[0:L, :] = ((keys >= thr) & causal).astype(F32)
            tied = jnp.max(jnp.where((cnt_eq > need) & (thr != _INT_MIN), 1.0, 0.0)) > 0.0

            @pl.when(tied)
            def _():
                tri = (lax.broadcasted_iota(I32, (_LANES, _LANES), 1)
                       <= lax.broadcasted_iota(I32, (_LANES, _LANES), 0)).astype(BF16)
                run = jnp.zeros((1, tq), F32)
                for c0 in range(0, L, _LANES):
                    kc = keys_ref[c0:c0 + _LANES, :]
                    eq = kc == thr
                    pre = jnp.dot(tri, eq.astype(BF16), preferred_element_type=F32) + run
                    run = run + jnp.sum(eq.astype(F32), axis=0, keepdims=True)
                    keep = (kc > thr) | (eq & (pre <= need))
                    rc = lax.broadcasted_iota(I32, (_LANES, tq), 0) + c0
                    cc = lax.broadcasted_iota(I32, (_LANES, tq), 1) + (L - tq)
                    maskt_ref[c0:c0 + _LANES, :] = (keep & (rc <= cc)).astype(F32)

        for c0 in range(0, L, _LANES):
            mask_ref[:, c0:c0 + _LANES] = maskt_ref[c0:c0 + _LANES, :].T

    def attend(L):
        rt = 32

        def g_body(g, carry):
            q4 = q_ref[pl.ds(g * rep, rep)].reshape(rep * tq, hd)
            lg_ref[:, 0:L] = _nt_dot(q4, k_ref[g, 0:L, :])
            for r in range(rep):
                h = g * rep + r
                far = rb_ref[n_buckets - 1, h]
                for t0 in range(0, tq, rt):
                    rows = slice(r * tq + t0, r * tq + t0 + rt)
                    parts = []
                    if L > 2 * tq:
                        parts.append(lg_ref[rows, 0:L - 2 * tq] + far)
                    if L > tq:
                        parts.append(lg_ref[rows, L - 2 * tq:L - tq] + tb_ref[h, 1, t0:t0 + rt, :])
                    parts.append(lg_ref[rows, L - tq:L] + tb_ref[h, 0, t0:t0 + rt, :])
                    x = parts[0] if len(parts) == 1 else jnp.concatenate(parts, axis=1)
                    x = jnp.where(mask_ref[t0:t0 + rt, 0:L] > 0.5, x, _NEG)
                    m = jnp.max(x, axis=-1, keepdims=True)
                    e = jnp.exp(x - m)
                    linv_ref[rows, :] = jnp.broadcast_to(1.0 / jnp.sum(e, axis=-1, keepdims=True), (rt, hd))
                    p_ref[rows, 0:L] = e.astype(p_ref.dtype)
            o = jnp.dot(p_ref[:, 0:L], v_ref[g, 0:L, :], preferred_element_type=F32)
            o = o * linv_ref[...] * sg_ref[pl.ds(g * rep, rep)].reshape(rep * tq, hd).astype(F32)
            o_ref[pl.ds(g * rep, rep)] = o.reshape(rep, tq, hd).astype(o_ref.dtype)
            return carry

        lax.fori_loop(0, n_kv, g_body, 0)

    for qb in range(n_q):
        @pl.when(i == qb)
        def _(qb=qb):
            L = (qb + 1) * tq
            scores(L)
            select(L)
            attend(L)


def _prompt_attention(rel_bias, qi_hm, wk, ki_bf, q_hm, k_hm, v_hm, sg_hm, tb_p, *, batch, seq, tq):
    idx_heads, m, idx_dim = qi_hm.shape
    n_heads, _, hd = q_hm.shape
    n_kv = k_hm.shape[0]
    rep = n_heads // n_kv
    n_q = seq // tq
    n_buckets = rel_bias.shape[0]
    topk = min(_TOPK_MAX, seq // 4)
    qrow = lambda b, i: (0, b * n_q + i, 0)
    body = functools.partial(_prompt_attn_body, tq=tq, n_q=n_q, topk=topk, idx_heads=idx_heads,
                             n_kv=n_kv, rep=rep, n_buckets=n_buckets, kchunk=min(512, seq))
    return pl.pallas_call(
        body,
        grid=(batch, n_q),
        in_specs=[pl.BlockSpec(memory_space=pltpu.SMEM),
                  pl.BlockSpec((idx_heads, tq, idx_dim), qrow),
                  pl.BlockSpec((tq, _LANES), lambda b, i: (b * n_q + i, 0)),
                  pl.BlockSpec((seq, idx_dim), lambda b, i: (b, 0)),
                  pl.BlockSpec((n_heads, tq, hd), qrow),
                  pl.BlockSpec((n_kv, seq, hd), lambda b, i: (0, b, 0)),
                  pl.BlockSpec((n_kv, seq, hd), lambda b, i: (0, b, 0)),
                  pl.BlockSpec((n_heads, tq, hd), qrow),
                  pl.BlockSpec((n_heads, 2, tq, _LANES), lambda b, i: (0, 0, 0, 0))],
        out_specs=pl.BlockSpec((n_heads, tq, hd), qrow),
        out_shape=jax.ShapeDtypeStruct((n_heads, m, hd), _MXU_DTYPE),
        scratch_shapes=[pltpu.VMEM((_LANES, tq), F32),
                        pltpu.VMEM((seq, tq), F32),
                        pltpu.VMEM((seq, tq), I32),
                        pltpu.VMEM((seq, tq), F32),
                        pltpu.VMEM((tq, seq), F32),
                        pltpu.VMEM((rep * tq, seq), F32),
                        pltpu.VMEM((rep * tq, seq), _MXU_DTYPE),
                        pltpu.VMEM((rep * tq, hd), F32)],
        compiler_params=_cparams(2),
        name="prompt_attention",
    )(rel_bias.astype(F32), qi_hm, wk, ki_bf, q_hm, k_hm, v_hm, sg_hm, tb_p)


def _page_copies(pt_ref, cache_ref, buf_ref, sem_ref, b, first_page, n_copy, slot, page):
    return [pltpu.make_async_copy(cache_ref.at[pt_ref[b, first_page + p]],
                                  buf_ref.at[slot, pl.ds(p * page, page)],
                                  sem_ref.at[slot]) for p in range(n_copy)]


def _sample_scores_body(pt_ref, qi_ref, wk_ref, cache_ref, o_ref, buf_ref, wb_ref, sem_ref,
                        *, n_pages, page, idx_heads, t_dec):
    b = pl.program_id(0)
    nb = pl.num_programs(0)
    slot = b % 2
    copies = functools.partial(_page_copies, pt_ref, cache_ref, buf_ref, sem_ref,
                               first_page=0, n_copy=n_pages, page=page)

    @pl.when(b == 0)
    def _():
        for c in copies(b=0, slot=0):
            c.start()

    @pl.when(b + 1 < nb)
    def _():
        for c in copies(b=b + 1, slot=1 - slot):
            c.start()

    qi = qi_ref[...].reshape(idx_heads * t_dec, -1).astype(_MXU_DTYPE)
    w_blk = wk_ref[:, 0:_LANES]
    for h in range(idx_heads):
        wb_ref[h * t_dec:(h + 1) * t_dec, :] = jnp.broadcast_to(w_blk[:, h:h + 1], (t_dec, _LANES))

    def page_scores(keys):
        s = _nt_dot(qi, keys.astype(_MXU_DTYPE))
        return jnp.sum((jnp.maximum(s, 0.0) * wb_ref[...]).reshape(idx_heads, t_dec, page), axis=0)

    past = n_pages * page
    ki_new = jnp.concatenate([wk_ref[:, _LANES:], jnp.zeros((page - t_dec, wk_ref.shape[1] - _LANES), F32)], axis=0)
    row = lax.broadcasted_iota(I32, (t_dec, page), 0)
    col = lax.broadcasted_iota(I32, (t_dec, page), 1)
    o_ref[0, :, past:past + page] = jnp.where(col <= row, page_scores(ki_new), -jnp.inf)

    for c in copies(b=b, slot=slot):
        c.wait()
    for p in range(n_pages):
        o_ref[0, :, p * page:(p + 1) * page] = page_scores(buf_ref[slot, p * page:(p + 1) * page, :])


def _sample_scores(page_table, qi_s, wk_s, cache_idx, *, t_dec):
    bd, n_pages = page_table.shape
    _, page, idx_dim = cache_idx.shape
    idx_heads = qi_s.shape[0]
    past = n_pages * page
    grid_spec = pltpu.PrefetchScalarGridSpec(
        num_scalar_prefetch=1,
        grid=(bd,),
        in_specs=[pl.BlockSpec((idx_heads, t_dec, idx_dim), lambda b, pt: (0, b, 0)),
                  pl.BlockSpec((t_dec, wk_s.shape[1]), lambda b, pt: (b, 0)),
                  pl.BlockSpec(memory_space=pl.ANY)],
        out_specs=pl.BlockSpec((1, t_dec, past + page), lambda b, pt: (b, 0, 0)),
        scratch_shapes=[pltpu.VMEM((2, past, idx_dim), F32),
                        pltpu.VMEM((idx_heads * t_dec, _LANES), F32),
                        pltpu.SemaphoreType.DMA((2,))],
    )
    return pl.pallas_call(
        functools.partial(_sample_scores_body, n_pages=n_pages, page=page, idx_heads=idx_heads, t_dec=t_dec),
        grid_spec=grid_spec,
        out_shape=jax.ShapeDtypeStruct((bd, t_dec, past + page), F32),
        compiler_params=_cparams(1),
        name="sample_scores",
    )(page_table, qi_s, wk_s, cache_idx)


def _sample_select_body(sc_ref, o_ref, keys_ref, *, topk, past):
    bd, t_dec, n_cols = sc_ref.shape

    def valid(c0, width):
        col = lax.broadcasted_iota(I32, (bd, t_dec, width), 2) + c0
        return col <= past + lax.broadcasted_iota(I32, (bd, t_dec, width), 1)

    keys_ref[...] = jnp.where(valid(0, n_cols), _sort_key(sc_ref[...]), _INT_MIN)

    def step(it, thr):
        cand = thr + jnp.left_shift(jnp.int32(1), 31 - it)
        cnt = jnp.sum((keys_ref[...] >= cand).astype(F32), axis=-1, keepdims=True)
        return jnp.where(cnt >= topk, cand, thr)

    thr = lax.fori_loop(0, 32, step, jnp.full((bd, t_dec, 1), _INT_MIN, I32))
    keys = keys_ref[...]
    cnt_gt = jnp.sum((keys > thr).astype(F32), axis=-1, keepdims=True)
    cnt_eq = jnp.sum((keys == thr).astype(F32), axis=-1, keepdims=True)
    need = topk - cnt_gt
    o_ref[...] = ((keys >= thr) & valid(0, n_cols)).astype(F32)
    tied = jnp.max(jnp.where((cnt_eq > need) & (thr != _INT_MIN), 1.0, 0.0)) > 0.0

    @pl.when(tied)
    def _():
        tri = (lax.broadcasted_iota(I32, (_LANES, _LANES), 0)
               <= lax.broadcasted_iota(I32, (_LANES, _LANES), 1)).astype(BF16)
        run = jnp.zeros((bd, t_dec, 1), F32)
        for c0 in range(0, n_cols, _LANES):
            kc = keys_ref[:, :, c0:c0 + _LANES]
            eq = kc == thr
            eq2 = eq.astype(F32).reshape(bd * t_dec, _LANES).astype(BF16)
            pre = jnp.dot(eq2, tri, preferred_element_type=F32).reshape(bd, t_dec, _LANES) + run
            run = run + jnp.sum(eq.astype(F32), axis=-1, keepdims=True)
            keep = (kc > thr) | (eq & (pre <= need))
            o_ref[:, :, c0:c0 + _LANES] = (keep & valid(c0, _LANES)).astype(F32)


def _sample_select(scores, *, topk, past):
    return pl.pallas_call(
        functools.partial(_sample_select_body, topk=topk, past=past),
        out_shape=jax.ShapeDtypeStruct(scores.shape, F32),
        scratch_shapes=[pltpu.VMEM(scores.shape, I32)],
        compiler_params=pltpu.CompilerParams(vmem_limit_bytes=_VMEM_LIMIT),
        name="sample_select",
    )(scores)


def _sample_attn_body(pt_ref, rb_ref, q_ref, mask_ref, mtail_ref, kn_ref, vn_ref, sg_ref, ts_ref,
                      ck_ref, cv_ref, o_ref, kbuf_ref, vbuf_ref, m_ref, l_ref, acc_ref, sem_ref,
                      *, n_chunks, cpages, page, n_kv, rep, t_dec, n_buckets):
    b = pl.program_id(0)
    c = pl.program_id(1)
    step = b * n_chunks + c
    n_steps = pl.num_programs(0) * n_chunks
    slot = step % 2
    hd = q_ref.shape[-1]
    rows = rep * t_dec
    ck = cpages * page

    def copies(bb, cc, sl):
        return (_page_copies(pt_ref, ck_ref, kbuf_ref, sem_ref.at[0], bb, cc * cpages, cpages, sl, page)
                + _page_copies(pt_ref, cv_ref, vbuf_ref, sem_ref.at[1], bb, cc * cpages, cpages, sl, page))

    @pl.when(step == 0)
    def _():
        for cp in copies(0, 0, 0):
            cp.start()

    @pl.when(step + 1 < n_steps)
    def _():
        nxt = step + 1
        for cp in copies(nxt // n_chunks, nxt % n_chunks, 1 - slot):
            cp.start()

    @pl.when(c == 0)
    def _():
        m_ref[...] = jnp.full(m_ref.shape, -jnp.inf, F32)
        l_ref[...] = jnp.zeros(l_ref.shape, F32)
        acc_ref[...] = jnp.zeros(acc_ref.shape, F32)

    def update(g, logits, mask, values):
        n = logits.shape[-1]
        x = jnp.where(jnp.broadcast_to(mask[None], (rep, t_dec, n)).reshape(rows, n) > 0.5, logits, -jnp.inf)
        rs = slice(g * rows, (g + 1) * rows)
        m_old = m_ref[rs, :]
        m_new = jnp.maximum(m_old, jnp.max(x, axis=-1, keepdims=True))
        m_safe = jnp.where(m_new == -jnp.inf, 0.0, m_new)
        alpha = jnp.exp(m_old - m_safe)
        p = jnp.exp(x - m_safe[:, 0:1])
        l_ref[rs, :] = alpha * l_ref[rs, :] + jnp.sum(p, axis=-1, keepdims=True)
        acc_ref[rs, :] = alpha * acc_ref[rs, :] + jnp.dot(p.astype(_MXU_DTYPE), values.astype(_MXU_DTYPE),
                                                          preferred_element_type=F32)
        m_ref[rs, :] = m_new

    def head_bias(g, kind):
        return ts_ref[g * rep:(g + 1) * rep, kind].reshape(rows, _LANES)

    def far_bias(g):
        r_idx = lax.broadcasted_iota(I32, (rows, 1), 0) // t_dec
        out = jnp.zeros((rows, 1), F32)
        for r in range(rep):
            out = jnp.where(r_idx == r, rb_ref[n_buckets - 1, g * rep + r], out)
        return out

    for cp in copies(b, c, slot):
        cp.wait()

    is_last = (c == n_chunks - 1).astype(F32)
    for g in range(n_kv):
        qg = q_ref[g * rep:(g + 1) * rep].reshape(rows, hd).astype(_MXU_DTYPE)
        kc = kbuf_ref[slot, :, g * hd:(g + 1) * hd]
        far = far_bias(g)
        lg = _nt_dot(qg, kc.astype(_MXU_DTYPE)) + far
        near = lg[:, ck - page:] + (head_bias(g, 0) - far) * is_last
        lg = jnp.concatenate([lg[:, :ck - page], near], axis=1)
        update(g, lg, mask_ref[...], vbuf_ref[slot, :, g * hd:(g + 1) * hd])

    @pl.when(c == n_chunks - 1)
    def _():
        zpad = jnp.zeros((page - t_dec, hd), F32)
        for g in range(n_kv):
            qg = q_ref[g * rep:(g + 1) * rep].reshape(rows, hd).astype(_MXU_DTYPE)
            kn = jnp.concatenate([kn_ref[:, g * hd:(g + 1) * hd], zpad], axis=0)
            vn = jnp.concatenate([vn_ref[:, g * hd:(g + 1) * hd], zpad], axis=0)
            lg = _nt_dot(qg, kn.astype(_MXU_DTYPE)) + head_bias(g, 1)
            update(g, lg, mtail_ref[...], vn)
            rs = slice(g * rows, (g + 1) * rows)
            o = acc_ref[rs, :] / l_ref[rs, :] * sg_ref[g * rep:(g + 1) * rep].reshape(rows, hd)
            o_ref[g * rep:(g + 1) * rep] = o.reshape(rep, t_dec, hd)


def _sample_attention(page_table, rel_bias, q_s, mask, kn, vn, sg_s, ts, cache_k, cache_v, *, t_dec, cpages):
    bd, n_pages = page_table.shape
    _, page, kvw = cache_k.shape
    n_heads, _, hd = q_s.shape
    n_kv = kvw // hd
    rep = n_heads // n_kv
    n_chunks = n_pages // cpages
    ck = cpages * page
    past = n_pages * page
    hrow = lambda b, c, pt: (0, b, 0)
    grid_spec = pltpu.PrefetchScalarGridSpec(
        num_scalar_prefetch=1,
        grid=(bd, n_chunks),
        in_specs=[pl.BlockSpec(memory_space=pltpu.SMEM),
                  pl.BlockSpec((n_heads, t_dec, hd), hrow),
                  pl.BlockSpec((t_dec, ck), lambda b, c, pt: (b, c)),
                  pl.BlockSpec((t_dec, page), lambda b, c, pt: (b, past // page)),
                  pl.BlockSpec((t_dec, kvw), lambda b, c, pt: (b, 0)),
                  pl.BlockSpec((t_dec, kvw), lambda b, c, pt: (b, 0)),
                  pl.BlockSpec((n_heads, t_dec, hd), hrow),
                  pl.BlockSpec((n_heads, 2, t_dec, _LANES), lambda b, c, pt: (0, 0, 0, 0)),
                  pl.BlockSpec(memory_space=pl.ANY),
                  pl.BlockSpec(memory_space=pl.ANY)],
        out_specs=pl.BlockSpec((n_heads, t_dec, hd), hrow),
        scratch_shapes=[pltpu.VMEM((2, ck, kvw), F32),
                        pltpu.VMEM((2, ck, kvw), F32),
                        pltpu.VMEM((n_heads * t_dec, _LANES), F32),
                        pltpu.VMEM((n_heads * t_dec, _LANES), F32),
                        pltpu.VMEM((n_heads * t_dec, hd), F32),
                        pltpu.SemaphoreType.DMA((2, 2))],
    )
    return pl.pallas_call(
        functools.partial(_sample_attn_body, n_chunks=n_chunks, cpages=cpages, page=page, n_kv=n_kv,
                          rep=rep, t_dec=t_dec, n_buckets=rel_bias.shape[0]),
        grid_spec=grid_spec,
        out_shape=jax.ShapeDtypeStruct((n_heads, bd * t_dec, hd), F32),
        compiler_params=_cparams(2),
        name="sample_attention",
    )(page_table, rel_bias.astype(F32), q_s, mask, mask, kn, vn, sg_s, ts, cache_k, cache_v)


def _pool_body(u_ref, halo_ref, gb_ref, wp_ref, ps_ref, o_ref, ext_ref, *, tm, pos_base, own_halo, pg):
    i = pl.program_id(1)
    halo = halo_ref[...].reshape(_POOL_HALO, -1)
    if own_halo:
        halo = jnp.where(i > 0, halo, 0.0)
    ext_ref[0:_POOL_HALO, :] = halo
    ext_ref[_POOL_HALO:, :] = u_ref[...]
    pos = pos_base + i * tm + lax.broadcasted_iota(I32, (tm, 1), 0)
    for g, win in enumerate(_POOL_WINDOWS):
        cols = slice(g * pg, (g + 1) * pg)
        tot = ext_ref[_POOL_HALO:, cols]
        for k in range(1, win):
            tot = tot + ext_ref[_POOL_HALO - k:_POOL_HALO - k + tm, cols]
        cnt = jnp.minimum(pos + 1, win).astype(F32)
        d = tot / cnt - ext_ref[_POOL_HALO:, cols]
        y = jnp.dot(d.astype(_MXU_DTYPE), wp_ref[g], preferred_element_type=F32)
        o_ref[:, cols] = (y * ps_ref[:, cols] * gb_ref[:, cols].astype(F32)).astype(o_ref.dtype)


def _pool(u, halo, gb, wp, ps, *, batch, rows, tm, pos_base, own_halo, out_dtype):
    m, pw = u.shape
    nt = rows // tm
    pg = pw // len(_POOL_WINDOWS)
    if own_halo:
        hb = tm // _POOL_HALO
        halo_spec = pl.BlockSpec((_POOL_HALO, pw), lambda b, i: (jnp.maximum((b * nt + i) * hb - 1, 0), 0))
    else:
        halo_spec = pl.BlockSpec((1, _POOL_HALO, pw), lambda b, i: (b, 0, 0))
    return pl.pallas_call(
        functools.partial(_pool_body, tm=tm, pos_base=pos_base, own_halo=own_halo, pg=pg),
        grid=(batch, nt),
        in_specs=[pl.BlockSpec((tm, pw), lambda b, i: (b * nt + i, 0)),
                  halo_spec,
                  pl.BlockSpec((tm, pw), lambda b, i: (b * nt + i, 0)),
                  pl.BlockSpec(wp.shape, lambda b, i: (0, 0, 0)),
                  pl.BlockSpec((1, pw), lambda b, i: (0, 0))],
        out_specs=pl.BlockSpec((tm, pw), lambda b, i: (b * nt + i, 0)),
        out_shape=jax.ShapeDtypeStruct((m, pw), out_dtype),
        scratch_shapes=[pltpu.VMEM((_POOL_HALO + tm, pw), F32)],
        compiler_params=_cparams(2),
        name="pool",
    )(u, halo, gb, wp, ps)


def _merge_body(ma_ref, mb_ref, w_ref, x_ref, g_ref, o_ref, mix_ref, h_ref, *, n_j, tn):
    j = pl.program_id(1)
    n_heads, _, hd = ma_ref.shape

    @pl.when(j == 0)
    def _():
        for h in range(n_heads):
            mix_ref[:, h * hd:(h + 1) * hd] = ma_ref[h].astype(mix_ref.dtype)
        mix_ref[:, n_heads * hd:] = mb_ref[...].astype(mix_ref.dtype)

    h_ref[j] = x_ref[...] + jnp.dot(mix_ref[...], w_ref[...], preferred_element_type=F32)

    @pl.when(j == n_j - 1)
    def _():
        ssq = jnp.zeros((h_ref.shape[1], 1), F32)
        for jj in range(n_j):
            hv = h_ref[jj]
            ssq = ssq + jnp.sum(hv * hv, axis=-1, keepdims=True)
        inv = lax.rsqrt(ssq / (n_j * tn) + _RMS_EPS)
        for jj in range(n_j):
            o_ref[:, jj * tn:(jj + 1) * tn] = h_ref[jj] * inv * g_ref[:, jj * tn:(jj + 1) * tn]


def _merge(mix_a, mix_b, w_out, x2d, final_g, *, tm, tn):
    n_heads, m, hd = mix_a.shape
    pw = mix_b.shape[1]
    mixw, d = w_out.shape
    n_j = d // tn
    return pl.pallas_call(
        functools.partial(_merge_body, n_j=n_j, tn=tn),
        grid=(m // tm, n_j),
        in_specs=[pl.BlockSpec((n_heads, tm, hd), lambda i, j: (0, i, 0)),
                  pl.BlockSpec((tm, pw), lambda i, j: (i, 0)),
                  pl.BlockSpec((mixw, tn), lambda i, j: (0, j)),
                  pl.BlockSpec((tm, tn), lambda i, j: (i, j)),
                  pl.BlockSpec((1, d), lambda i, j: (0, 0))],
        out_specs=pl.BlockSpec((tm, d), lambda i, j: (i, 0)),
        out_shape=jax.ShapeDtypeStruct((m, d), F32),
        scratch_shapes=[pltpu.VMEM((tm, mixw), _MXU_DTYPE), pltpu.VMEM((n_j, tm, tn), F32)],
        compiler_params=_cparams(2),
        name="merge",
    )(mix_a, mix_b, w_out, x2d, final_g.reshape(1, d).astype(F32))


def kernel(x_prompt, x_sample, cache_k, cache_v, cache_idx_k, state_pool, page_table,
           rel_bias, ln_g, w_in, w_pool, pool_scale, w_out, final_g):
    depth = w_in.shape[0]
    assert depth == 1, "single-layer step only"
    batch, seq, d_model = x_prompt.shape
    bd, t_dec, _ = x_sample.shape
    _, n_phys, page, n_kv, hd = cache_k.shape
    idx_dim = cache_idx_k.shape[-1]
    n_buckets, n_heads = rel_bias.shape
    pw = pool_scale.shape[-1]
    n_pg = w_pool.shape[1]
    attn_w, kv_w = n_heads * hd, n_kv * hd
    in_w = w_in.shape[-1]
    idx_heads = (in_w - 2 * attn_w - 2 * kv_w - idx_dim - 2 * pw) // (idx_dim + 1)
    n_pages = page_table.shape[1]
    past = n_pages * page
    mp, ms = batch * seq, bd * t_dec
    tq = _LANES
    assert hd == _LANES and idx_dim == _LANES and idx_heads <= _LANES and n_pg == len(_POOL_WINDOWS)
    assert page >= _MAX_DISTANCE and tq >= _MAX_DISTANCE and seq % tq == 0 and t_dec <= _POOL_HALO - 1
    assert state_pool.shape[2] == _POOL_HALO - 1

    o_q, o_k, o_v = 0, attn_w, attn_w + kv_w
    o_ga = o_v + kv_w
    o_qi = o_ga + attn_w
    o_wi = o_qi + idx_heads * idx_dim
    o_ki = o_wi + idx_heads
    o_u = o_ki + idx_dim
    o_gb = o_u + pw
    w2d = w_in[0]
    wcast = lambda lo, hi: w2d[:, lo:hi].astype(_MXU_DTYPE)

    xn_p = _rmsnorm(x_prompt.reshape(mp, d_model), ln_g[0], _MXU_DTYPE)
    xn_s = _rmsnorm(x_sample.reshape(ms, d_model), ln_g[0], _MXU_DTYPE)

    tm = min(512, mp)
    tn = lambda n: min(1024, n)
    proj = functools.partial(_proj_call, xn_p, xn_s, tm=tm)
    rm = lambda n, t, dt, s: _row_major(ms if s else mp, n, tm, t, dt, s)
    hm = lambda n, t, dt, s: _head_major(ms if s else mp, n, tm, t, dt, s)

    def scaled_heads(scale):
        def epi(acc, refs):
            _store_heads(refs[0], acc * scale)
        return epi

    def silu_heads(acc, refs):
        _store_heads(refs[0], _silu(acc))

    def silu_rows(acc, refs):
        refs[0][...] = _silu(acc).astype(refs[0].dtype)

    def plain_rows(acc, refs):
        refs[0][...] = acc.astype(refs[0].dtype)

    def kv_p(acc, refs):
        refs[0][...] = acc[:, :kv_w]
        refs[1][...] = acc[:, kv_w:]
        _store_heads(refs[2], acc[:, :kv_w])
        _store_heads(refs[3], acc[:, kv_w:])

    def kv_s(acc, refs):
        refs[0][...] = acc[:, :kv_w]
        refs[1][...] = acc[:, kv_w:]

    def wk_p(acc, refs):
        refs[0][...] = jnp.concatenate([acc[:, :_LANES] * idx_heads ** -0.5, acc[:, _LANES:]], axis=1)
        refs[1][...] = acc[:, _LANES:].astype(refs[1].dtype)

    def wk_s(acc, refs):
        refs[0][...] = jnp.concatenate([acc[:, :_LANES] * idx_heads ** -0.5, acc[:, _LANES:]], axis=1)

    t = tn(attn_w)
    q_hm, q_s = proj(wcast(o_q, o_k), t, outs_p=[hm(attn_w, t, _MXU_DTYPE, False)], outs_s=[hm(attn_w, t, F32, True)],
                     epi_p=scaled_heads(hd ** -0.5), epi_s=scaled_heads(hd ** -0.5), name="proj_q")
    t = 2 * kv_w
    k_p, v_p, k_hm, v_hm, k_s, v_s = proj(
        wcast(o_k, o_ga), t,
        outs_p=[((mp, kv_w), F32, (tm, kv_w), lambda j, i: (i, 0)), ((mp, kv_w), F32, (tm, kv_w), lambda j, i: (i, 0)),
                ((n_kv, mp, hd), _MXU_DTYPE, (n_kv, tm, hd), lambda j, i: (0, i, 0)),
                ((n_kv, mp, hd), _MXU_DTYPE, (n_kv, tm, hd), lambda j, i: (0, i, 0))],
        outs_s=[((ms, kv_w), F32, (ms, kv_w), lambda j, i: (0, 0)), ((ms, kv_w), F32, (ms, kv_w), lambda j, i: (0, 0))],
        epi_p=kv_p, epi_s=kv_s, name="proj_kv")
    t = tn(attn_w)
    sga_hm, sga_s = proj(wcast(o_ga, o_qi), t, outs_p=[hm(attn_w, t, _MXU_DTYPE, False)],
                         outs_s=[hm(attn_w, t, F32, True)], epi_p=silu_heads, epi_s=silu_heads, name="proj_ga")
    t = tn(idx_heads * idx_dim)
    qi_hm, qi_s = proj(wcast(o_qi, o_wi), t, outs_p=[hm(idx_heads * idx_dim, t, _MXU_DTYPE, False)],
                       outs_s=[hm(idx_heads * idx_dim, t, F32, True)],
                       epi_p=scaled_heads(idx_dim ** -0.5), epi_s=scaled_heads(idx_dim ** -0.5), name="proj_qi")
    w_wk = jnp.concatenate([wcast(o_wi, o_wi + _LANES), wcast(o_ki, o_u)], axis=1)
    t = 2 * _LANES
    wk_pr, ki_bf, wk_sm = proj(
        w_wk, t,
        outs_p=[((mp, t), F32, (tm, t), lambda j, i: (i, 0)), ((mp, idx_dim), _MXU_DTYPE, (tm, idx_dim), lambda j, i: (i, 0))],
        outs_s=[((ms, t), F32, (ms, t), lambda j, i: (0, 0))],
        epi_p=wk_p, epi_s=wk_s, name="proj_wk")
    t = tn(pw)
    u_p, u_s = proj(wcast(o_u, o_gb), t, outs_p=[rm(pw, t, F32, False)], outs_s=[rm(pw, t, F32, True)],
                    epi_p=plain_rows, epi_s=plain_rows, name="proj_u")
    sgb_p, sgb_s = proj(wcast(o_gb, in_w), t, outs_p=[rm(pw, t, _MXU_DTYPE, False)], outs_s=[rm(pw, t, F32, True)],
                        epi_p=silu_rows, epi_s=silu_rows, name="proj_gb")

    tb_p, tb_s = _bias_tables(rel_bias, tq, t_dec, page)

    mixa_p = _prompt_attention(rel_bias, qi_hm, wk_pr, ki_bf, q_hm, k_hm, v_hm, sga_hm, tb_p,
                               batch=batch, seq=seq, tq=tq)

    scores = _sample_scores(page_table, qi_s, wk_sm, cache_idx_k[0], t_dec=t_dec)
    topk_s = min(_TOPK_MAX, (past + t_dec) // 4)
    mask_s = _sample_select(scores, topk=topk_s, past=past).reshape(ms, past + page)
    cpages = max(1, min(16, n_pages // 4))
    assert n_pages % cpages == 0
    mixa_s = _sample_attention(page_table, rel_bias, q_s, mask_s, k_s, v_s, sga_s, tb_s,
                               cache_k[0].reshape(n_phys, page, kv_w), cache_v[0].reshape(n_phys, page, kv_w),
                               t_dec=t_dec, cpages=cpages)

    wp = w_pool[0].astype(_MXU_DTYPE)
    ps = pool_scale[0].reshape(1, pw).astype(F32)
    mixb_p = _pool(u_p, u_p, sgb_p, wp, ps, batch=batch, rows=seq, tm=min(512, seq), pos_base=0,
                   own_halo=True, out_dtype=_MXU_DTYPE)
    halo_s = jnp.concatenate([jnp.zeros((bd, 1, pw), F32), state_pool[0].astype(F32)], axis=1)
    mixb_s = _pool(u_s, halo_s, sgb_s, wp, ps, batch=bd, rows=t_dec, tm=t_dec, pos_base=past,
                   own_halo=False, out_dtype=F32)

    w_o = w_out[0].astype(_MXU_DTYPE)
    tn_o = min(512, d_model)
    y_p = _merge(mixa_p, mixb_p, w_o, x_prompt.reshape(mp, d_model), final_g, tm=min(512, mp), tn=tn_o)
    y_s = _merge(mixa_s, mixb_s, w_o, x_sample.reshape(ms, d_model), final_g, tm=ms, tn=tn_o)

    n_state = state_pool.shape[2]
    u_full_s = jnp.concatenate([state_pool[0].astype(F32), u_s.reshape(bd, t_dec, pw)], axis=1)
    return (y_p.reshape(batch, seq, d_model), y_s.reshape(bd, t_dec, d_model),
            k_p.reshape(1, batch, seq, n_kv, hd), v_p.reshape(1, batch, seq, n_kv, hd),
            wk_pr[:, _LANES:].reshape(1, batch, seq, idx_dim),
            u_p.reshape(batch, seq, pw)[:, seq - n_state:][None],
            k_s.reshape(1, bd, t_dec, n_kv, hd), v_s.reshape(1, bd, t_dec, n_kv, hd),
            wk_sm[:, _LANES:].reshape(1, bd, t_dec, idx_dim),
            u_full_s[:, t_dec:][None])
```

```python
import functools
import math

import jax
import jax.numpy as jnp
from jax import lax
from jax.experimental import pallas as pl
from jax.experimental.pallas import tpu as pltpu

F32 = jnp.float32
BF16 = jnp.bfloat16
I32 = jnp.int32

_TOPK_MAX = 256
_POOL_WINDOWS = (2, 4, 8, 16)
_POOL_HALO = 16
_MAX_DISTANCE = 128
_RMS_EPS = 1e-6
_NEG = -1e30
_INT_MIN = -(2 ** 31)
_LANES = 128
_VMEM_LIMIT = 56 * 1024 * 1024
_MXU_DTYPE = BF16


def _cparams(n_axes):
    return pltpu.CompilerParams(dimension_semantics=("arbitrary",) * n_axes,
                                vmem_limit_bytes=_VMEM_LIMIT)


def _nt_dot(a, b):
    return lax.dot_general(a, b, (((1,), (1,)), ((), ())), preferred_element_type=F32)


def _sort_key(x):
    bits = lax.bitcast_convert_type(x + 0.0, I32)
    return bits ^ ((bits >> 31) & 0x7FFFFFFF)


def _rmsnorm_body(x_ref, g_ref, o_ref):
    x = x_ref[...].astype(F32)
    var = jnp.mean(x * x, axis=-1, keepdims=True)
    o_ref[...] = (x * lax.rsqrt(var + _RMS_EPS) * g_ref[...]).astype(o_ref.dtype)


def _rmsnorm(x2d, g, out_dtype):
    m, d = x2d.shape
    tm = min(256, m)
    return pl.pallas_call(
        _rmsnorm_body,
        grid=(m // tm,),
        in_specs=[pl.BlockSpec((tm, d), lambda i: (i, 0)), pl.BlockSpec((1, d), lambda i: (0, 0))],
        out_specs=pl.BlockSpec((tm, d), lambda i: (i, 0)),
        out_shape=jax.ShapeDtypeStruct((m, d), out_dtype),
        compiler_params=_cparams(1),
        name="rmsnorm",
    )(x2d, g.reshape(1, d).astype(F32))


def _proj_body(xp_ref, xs_ref, w_ref, *out_refs, n_p, epi_p, epi_s):
    w = w_ref[...]
    epi_p(jnp.dot(xp_ref[...], w, preferred_element_type=F32), out_refs[:n_p])

    @pl.when(pl.program_id(1) == 0)
    def _():
        epi_s(jnp.dot(xs_ref[...], w, preferred_element_type=F32), out_refs[n_p:])


def _proj_call(xn_p, xn_s, w, tn, tm, outs_p, outs_s, epi_p, epi_s, name):
    mp, d = xn_p.shape
    ms = xn_s.shape[0]
    ncols = w.shape[1]
    outs = list(outs_p) + list(outs_s)
    return pl.pallas_call(
        functools.partial(_proj_body, n_p=len(outs_p), epi_p=epi_p, epi_s=epi_s),
        grid=(ncols // tn, mp // tm),
        in_specs=[pl.BlockSpec((tm, d), lambda j, i: (i, 0)),
                  pl.BlockSpec((ms, d), lambda j, i: (0, 0)),
                  pl.BlockSpec((d, tn), lambda j, i: (0, j))],
        out_specs=[pl.BlockSpec(o[2], o[3]) for o in outs],
        out_shape=[jax.ShapeDtypeStruct(o[0], o[1]) for o in outs],
        compiler_params=_cparams(2),
        name=name,
    )(xn_p, xn_s, w)


def _row_major(m, ncols, tm, tn, dtype, sample):
    if sample:
        return ((m, ncols), dtype, (m, tn), lambda j, i: (0, j))
    return ((m, ncols), dtype, (tm, tn), lambda j, i: (i, j))


def _head_major(m, ncols, tm, tn, dtype, sample):
    nh, hb = ncols // _LANES, tn // _LANES
    if sample:
        return ((nh, m, _LANES), dtype, (hb, m, _LANES), lambda j, i: (j, 0, 0))
    return ((nh, m, _LANES), dtype, (hb, tm, _LANES), lambda j, i: (j, i, 0))


def _store_heads(ref, val):
    for hh in range(ref.shape[0]):
        ref[hh] = val[:, hh * _LANES:(hh + 1) * _LANES].astype(ref.dtype)


def _silu(x):
    return x * jax.nn.sigmoid(x)


def _bias_tables_body(rb_ref, tp_ref, ts_ref, *, n_buckets, tq, t_dec, page):
    h = pl.program_id(0)
    max_exact = n_buckets // 2

    def table(dist):
        d = jnp.maximum(dist, 0)
        df = jnp.maximum(d.astype(F32), 1.0)
        large = max_exact + (jnp.log(df / max_exact) / math.log(_MAX_DISTANCE / max_exact)
                             * (n_buckets - max_exact)).astype(I32)
        bucket = jnp.where(d < max_exact, d, jnp.minimum(large, n_buckets - 1))
        out = jnp.zeros(dist.shape, F32)
        for b in range(n_buckets):
            out = jnp.where(bucket == b, rb_ref[b, h], out)
        return out

    rp = lax.broadcasted_iota(I32, (tq, _LANES), 0)
    cp = lax.broadcasted_iota(I32, (tq, _LANES), 1)
    far = rb_ref[n_buckets - 1, h]
    tp_ref[0, 0] = table(rp - cp) - far
    tp_ref[0, 1] = table(rp - cp + tq) - far
    rs = lax.broadcasted_iota(I32, (t_dec, _LANES), 0)
    cs = lax.broadcasted_iota(I32, (t_dec, _LANES), 1)
    ts_ref[0, 0] = table(rs - cs + page)
    ts_ref[0, 1] = table(rs - cs)


def _bias_tables(rel_bias, tq, t_dec, page):
    n_buckets, n_heads = rel_bias.shape
    return pl.pallas_call(
        functools.partial(_bias_tables_body, n_buckets=n_buckets, tq=tq, t_dec=t_dec, page=page),
        grid=(n_heads,),
        in_specs=[pl.BlockSpec(memory_space=pltpu.SMEM)],
        out_specs=[pl.BlockSpec((1, 2, tq, _LANES), lambda h: (h, 0, 0, 0)),
                   pl.BlockSpec((1, 2, t_dec, _LANES), lambda h: (h, 0, 0, 0))],
        out_shape=[jax.ShapeDtypeStruct((n_heads, 2, tq, _LANES), F32),
                   jax.ShapeDtypeStruct((n_heads, 2, t_dec, _LANES), F32)],
        compiler_params=_cparams(1),
        name="bias_tables",
    )(rel_bias.astype(F32))


def _prompt_attn_body(qi_ref, wk_ref, ki_ref, q_ref, k_ref, v_ref, sg_ref, tb_ref, o_ref,
                      wt_ref, sc_ref, keys_ref, madd_ref, lg_ref, mp_ref, mb_ref, lp_ref, acc_ref,
                      *, tq, topk, idx_heads, n_kv, rep):
    i = pl.program_id(1)
    n_ch = i + 1
    hd = q_ref.shape[-1]
    rows = rep * tq
    nb = tq // _LANES
    hgrp = 2
    kc2 = 2 * tq

    def chunk(c):
        return pl.ds(pl.multiple_of(c * tq, tq), tq)

    def causal(c):
        key_pos = lax.broadcasted_iota(I32, (tq, tq), 0) + c * tq
        return key_pos <= lax.broadcasted_iota(I32, (tq, tq), 1) + i * tq

    wt_ref[...] = wk_ref[...].T
    sc_ref[...] = jnp.zeros(sc_ref.shape, F32)

    def hg_body(hg, carry):
        qi2 = qi_ref[pl.ds(hg * hgrp, hgrp)].reshape(hgrp * tq, -1)
        w_rows = [wt_ref[pl.ds(hg * hgrp + r, 1), :] for r in range(hgrp)]

        def c_body(c2, carry2):
            ks = pl.ds(pl.multiple_of(c2 * kc2, kc2), kc2)
            s = _nt_dot(ki_ref[ks, :], qi2)
            a = sc_ref[ks, :]
            for r in range(hgrp):
                a = a + jnp.maximum(s[:, r * tq:(r + 1) * tq], 0.0) * w_rows[r]
            sc_ref[ks, :] = a
            return carry2

        lax.fori_loop(0, (n_ch + 1) // 2, c_body, 0)
        return carry

    lax.fori_loop(0, idx_heads // hgrp, hg_body, 0)

    def key_body(c, carry):
        keys_ref[chunk(c), :] = jnp.where(causal(c), _sort_key(sc_ref[chunk(c), :]), _INT_MIN)
        return carry

    lax.fori_loop(0, n_ch, key_body, 0)

    def count(pred):
        def c_body(c, acc):
            hit = pred(keys_ref[chunk(c), :]).astype(F32)
            return acc + jnp.sum(hit.reshape(tq // 8, 8, tq), axis=0)

        return jnp.sum(lax.fori_loop(0, n_ch, c_body, jnp.zeros((8, tq), F32)), axis=0, keepdims=True)

    def radix_step(it, thr):
        cand = thr + jnp.left_shift(jnp.int32(1), 31 - it)
        return jnp.where(count(lambda k: k >= cand) >= topk, cand, thr)

    thr = lax.fori_loop(0, 32, radix_step, jnp.full((1, tq), _INT_MIN, I32))
    need = topk - count(lambda k: k > thr)
    n_eq = count(lambda k: k == thr)
    tied = jnp.max(jnp.where((n_eq > need) & (thr != _INT_MIN), 1.0, 0.0)) > 0.0

    def madd_body(c, carry):
        keep = (keys_ref[chunk(c), :] >= thr) & causal(c)
        madd_ref[c] = jnp.where(keep, 0.0, _NEG).T
        return carry

    lax.fori_loop(0, n_ch, madd_body, 0)

    @pl.when(tied)
    def _():
        tri = (lax.broadcasted_iota(I32, (tq, tq), 1) <= lax.broadcasted_iota(I32, (tq, tq), 0)).astype(BF16)

        def tie_body(c, run):
            kc = keys_ref[chunk(c), :]
            eq = kc == thr
            pre = jnp.dot(tri, eq.astype(F32).astype(BF16), preferred_element_type=F32) + run
            keep = ((kc > thr) | (eq & (pre <= need))) & causal(c)
            madd_ref[c] = jnp.where(keep, 0.0, _NEG).T
            return run + jnp.sum(eq.astype(F32), axis=0, keepdims=True)

        lax.fori_loop(0, n_ch, tie_body, jnp.zeros((1, tq), F32))

    def g_body(g, carry):
        q4 = q_ref[pl.ds(g * rep, rep)].reshape(rows, hd)
        mp_ref[...] = jnp.full(mp_ref.shape, _NEG, F32)

        def logits(c, near):
            s = _nt_dot(q4, k_ref[g, chunk(c), :])
            madd = madd_ref[c]
            for r in range(rep):
                for a in range(nb):
                    rs = slice(r * tq + a * _LANES, r * tq + (a + 1) * _LANES)
                    x = s[rs, :] + madd[a * _LANES:(a + 1) * _LANES, :]
                    if near is not None:
                        cols = []
                        for b in range(nb):
                            xb = x[:, b * _LANES:(b + 1) * _LANES]
                            kind = near * nb + a - b
                            if kind in (0, 1):
                                xb = xb + tb_ref[g * rep + r, kind]
                            cols.append(xb)
                        x = jnp.concatenate(cols, axis=1)
                    lg_ref[c, rs, :] = x
                    xm = x[:, 0:_LANES]
                    for b in range(1, nb):
                        xm = jnp.maximum(xm, x[:, b * _LANES:(b + 1) * _LANES])
                    mp_ref[rs, :] = jnp.maximum(mp_ref[rs, :], xm)

        def far_body(c, carry2):
            logits(c, None)
            return carry2

        lax.fori_loop(0, jnp.maximum(i - 1, 0), far_body, 0)

        @pl.when(i >= 1)
        def _():
            logits(i - 1, 1)

        logits(i, 0)

        mb_ref[...] = jnp.broadcast_to(jnp.max(mp_ref[...], axis=-1, keepdims=True), mb_ref.shape)
        lp_ref[...] = jnp.zeros(lp_ref.shape, F32)
        acc_ref[...] = jnp.zeros(acc_ref.shape, F32)

        def pv_body(c, carry2):
            m = mb_ref[...]
            e = jnp.exp(lg_ref[c] - jnp.concatenate([m] * nb, axis=1))
            es = e[:, 0:_LANES]
            for b in range(1, nb):
                es = es + e[:, b * _LANES:(b + 1) * _LANES]
            lp_ref[...] += es
            acc_ref[...] += jnp.dot(e.astype(_MXU_DTYPE), v_ref[g, chunk(c), :], preferred_element_type=F32)
            return carry2

        lax.fori_loop(0, n_ch, pv_body, 0)
        o = acc_ref[...] / jnp.sum(lp_ref[...], axis=-1, keepdims=True)
        o = o * sg_ref[pl.ds(g * rep, rep)].reshape(rows, hd).astype(F32)
        o_ref[pl.ds(g * rep, rep)] = o.reshape(rep, tq, hd).astype(o_ref.dtype)
        return carry

    lax.fori_loop(0, n_kv, g_body, 0)


def _prompt_attention(qi_hm, wk, ki_bf, q_hm, k_hm, v_hm, sg_hm, tb_p, *, batch, seq, tq):
    idx_heads, m, idx_dim = qi_hm.shape
    n_heads, _, hd = q_hm.shape
    n_kv = k_hm.shape[0]
    rep = n_heads // n_kv
    n_q = seq // tq
    topk = min(_TOPK_MAX, seq // 4)
    qrow = lambda b, i: (0, b * n_q + i, 0)
    body = functools.partial(_prompt_attn_body, tq=tq, topk=topk, idx_heads=idx_heads, n_kv=n_kv, rep=rep)
    return pl.pallas_call(
        body,
        grid=(batch, n_q),
        in_specs=[pl.BlockSpec((idx_heads, tq, idx_dim), qrow),
                  pl.BlockSpec((tq, _LANES), lambda b, i: (b * n_q + i, 0)),
                  pl.BlockSpec((seq, idx_dim), lambda b, i: (b, 0)),
                  pl.BlockSpec((n_heads, tq, hd), qrow),
                  pl.BlockSpec((n_kv, seq, hd), lambda b, i: (0, b, 0)),
                  pl.BlockSpec((n_kv, seq, hd), lambda b, i: (0, b, 0)),
                  pl.BlockSpec((n_heads, tq, hd), qrow),
                  pl.BlockSpec((n_heads, 2, _LANES, _LANES), lambda b, i: (0, 0, 0, 0))],
        out_specs=pl.BlockSpec((n_heads, tq, hd), qrow),
        out_shape=jax.ShapeDtypeStruct((n_heads, m, hd), _MXU_DTYPE),
        scratch_shapes=[pltpu.VMEM((_LANES, tq), F32),
                        pltpu.VMEM((seq, tq), F32),
                        pltpu.VMEM((seq, tq), I32),
                        pltpu.VMEM((n_q, tq, tq), F32),
                        pltpu.VMEM((n_q, rep * tq, tq), F32),
                        pltpu.VMEM((rep * tq, _LANES), F32),
                        pltpu.VMEM((rep * tq, _LANES), F32),
                        pltpu.VMEM((rep * tq, _LANES), F32),
                        pltpu.VMEM((rep * tq, hd), F32)],
        compiler_params=_cparams(2),
        name="prompt_attention",
    )(qi_hm, wk, ki_bf, q_hm, k_hm, v_hm, sg_hm, tb_p)


def _page_copies(pt_ref, cache_ref, buf_ref, sem_ref, b, first_page, n_copy, slot, page):
    return [pltpu.make_async_copy(cache_ref.at[pt_ref[b, first_page + p]],
                                  buf_ref.at[slot, pl.ds(p * page, page)],
                                  sem_ref.at[slot]) for p in range(n_copy)]


def _sample_scores_body(pt_ref, qi_ref, wk_ref, cache_ref, o_ref, buf_ref, wb_ref, sem_ref,
                        *, n_pages, page, idx_heads, t_dec):
    b = pl.program_id(0)
    nb = pl.num_programs(0)
    slot = b % 2
    copies = functools.partial(_page_copies, pt_ref, cache_ref, buf_ref, sem_ref,
                               first_page=0, n_copy=n_pages, page=page)

    @pl.when(b == 0)
    def _():
        for c in copies(b=0, slot=0):
            c.start()

    @pl.when(b + 1 < nb)
    def _():
        for c in copies(b=b + 1, slot=1 - slot):
            c.start()

    qi = qi_ref[...].reshape(idx_heads * t_dec, -1).astype(_MXU_DTYPE)
    w_blk = wk_ref[:, 0:_LANES]
    for h in range(idx_heads):
        wb_ref[h * t_dec:(h + 1) * t_dec, :] = jnp.broadcast_to(w_blk[:, h:h + 1], (t_dec, _LANES))

    def page_scores(keys):
        s = _nt_dot(qi, keys.astype(_MXU_DTYPE))
        return jnp.sum((jnp.maximum(s, 0.0) * wb_ref[...]).reshape(idx_heads, t_dec, page), axis=0)

    past = n_pages * page
    ki_new = jnp.concatenate([wk_ref[:, _LANES:], jnp.zeros((page - t_dec, wk_ref.shape[1] - _LANES), F32)], axis=0)
    row = lax.broadcasted_iota(I32, (t_dec, page), 0)
    col = lax.broadcasted_iota(I32, (t_dec, page), 1)
    o_ref[0, :, past:past + page] = jnp.where(col <= row, page_scores(ki_new), -jnp.inf)

    for c in copies(b=b, slot=slot):
        c.wait()
    for p in range(n_pages):
        o_ref[0, :, p * page:(p + 1) * page] = page_scores(buf_ref[slot, p * page:(p + 1) * page, :])


def _sample_scores(page_table, qi_s, wk_s, cache_idx, *, t_dec):
    bd, n_pages = page_table.shape
    _, page, idx_dim = cache_idx.shape
    idx_heads = qi_s.shape[0]
    past = n_pages * page
    grid_spec = pltpu.PrefetchScalarGridSpec(
        num_scalar_prefetch=1,
        grid=(bd,),
        in_specs=[pl.BlockSpec((idx_heads, t_dec, idx_dim), lambda b, pt: (0, b, 0)),
                  pl.BlockSpec((t_dec, wk_s.shape[1]), lambda b, pt: (b, 0)),
                  pl.BlockSpec(memory_space=pl.ANY)],
        out_specs=pl.BlockSpec((1, t_dec, past + page), lambda b, pt: (b, 0, 0)),
        scratch_shapes=[pltpu.VMEM((2, past, idx_dim), F32),
                        pltpu.VMEM((idx_heads * t_dec, _LANES), F32),
                        pltpu.SemaphoreType.DMA((2,))],
    )
    return pl.pallas_call(
        functools.partial(_sample_scores_body, n_pages=n_pages, page=page, idx_heads=idx_heads, t_dec=t_dec),
        grid_spec=grid_spec,
        out_shape=jax.ShapeDtypeStruct((bd, t_dec, past + page), F32),
        compiler_params=_cparams(1),
        name="sample_scores",
    )(page_table, qi_s, wk_s, cache_idx)


def _sample_select_body(sc_ref, o_ref, keys_ref, *, topk, past):
    bd, t_dec, n_cols = sc_ref.shape

    def valid(c0, width):
        col = lax.broadcasted_iota(I32, (bd, t_dec, width), 2) + c0
        return col <= past + lax.broadcasted_iota(I32, (bd, t_dec, width), 1)

    keys_ref[...] = jnp.where(valid(0, n_cols), _sort_key(sc_ref[...]), _INT_MIN)

    def step(it, thr):
        cand = thr + jnp.left_shift(jnp.int32(1), 31 - it)
        cnt = jnp.sum((keys_ref[...] >= cand).astype(F32), axis=-1, keepdims=True)
        return jnp.where(cnt >= topk, cand, thr)

    thr = lax.fori_loop(0, 32, step, jnp.full((bd, t_dec, 1), _INT_MIN, I32))
    keys = keys_ref[...]
    cnt_gt = jnp.sum((keys > thr).astype(F32), axis=-1, keepdims=True)
    cnt_eq = jnp.sum((keys == thr).astype(F32), axis=-1, keepdims=True)
    need = topk - cnt_gt
    o_ref[...] = ((keys >= thr) & valid(0, n_cols)).astype(F32)
    tied = jnp.max(jnp.where((cnt_eq > need) & (thr != _INT_MIN), 1.0, 0.0)) > 0.0

    @pl.when(tied)
    def _():
        tri = (lax.broadcasted_iota(I32, (_LANES, _LANES), 0)
               <= lax.broadcasted_iota(I32, (_LANES, _LANES), 1)).astype(BF16)
        run = jnp.zeros((bd, t_dec, 1), F32)
        for c0 in range(0, n_cols, _LANES):
            kc = keys_ref[:, :, c0:c0 + _LANES]
            eq = kc == thr
            eq2 = eq.astype(F32).reshape(bd * t_dec, _LANES).astype(BF16)
            pre = jnp.dot(eq2, tri, preferred_element_type=F32).reshape(bd, t_dec, _LANES) + run
            run = run + jnp.sum(eq.astype(F32), axis=-1, keepdims=True)
            keep = (kc > thr) | (eq & (pre <= need))
            o_ref[:, :, c0:c0 + _LANES] = (keep & valid(c0, _LANES)).astype(F32)


def _sample_select(scores, *, topk, past):
    return pl.pallas_call(
        functools.partial(_sample_select_body, topk=topk, past=past),
        out_shape=jax.ShapeDtypeStruct(scores.shape, F32),
        scratch_shapes=[pltpu.VMEM(scores.shape, I32)],
        compiler_params=pltpu.CompilerParams(vmem_limit_bytes=_VMEM_LIMIT),
        name="sample_select",
    )(scores)


def _sample_attn_body(pt_ref, rb_ref, q_ref, mask_ref, mtail_ref, kn_ref, vn_ref, sg_ref, ts_ref,
                      ck_ref, cv_ref, o_ref, kbuf_ref, vbuf_ref, m_ref, l_ref, acc_ref, sem_ref,
                      *, n_chunks, cpages, page, n_kv, rep, t_dec, n_buckets):
    b = pl.program_id(0)
    c = pl.program_id(1)
    step = b * n_chunks + c
    n_steps = pl.num_programs(0) * n_chunks
    slot = step % 2
    hd = q_ref.shape[-1]
    rows = rep * t_dec
    ck = cpages * page

    prow = page * n_kv

    def copies(bb, cc, sl):
        return (_page_copies(pt_ref, ck_ref, kbuf_ref, sem_ref.at[0], bb, cc * cpages, cpages, sl, prow)
                + _page_copies(pt_ref, cv_ref, vbuf_ref, sem_ref.at[1], bb, cc * cpages, cpages, sl, prow))

    def kv_head(buf_ref, g):
        return buf_ref[slot, pl.ds(g, ck, stride=n_kv), :]

    @pl.when(step == 0)
    def _():
        for cp in copies(0, 0, 0):
            cp.start()

    @pl.when(step + 1 < n_steps)
    def _():
        nxt = step + 1
        for cp in copies(nxt // n_chunks, nxt % n_chunks, 1 - slot):
            cp.start()

    @pl.when(c == 0)
    def _():
        m_ref[...] = jnp.full(m_ref.shape, -jnp.inf, F32)
        l_ref[...] = jnp.zeros(l_ref.shape, F32)
        acc_ref[...] = jnp.zeros(acc_ref.shape, F32)

    def update(g, logits, mask, values):
        n = logits.shape[-1]
        x = jnp.where(jnp.broadcast_to(mask[None], (rep, t_dec, n)).reshape(rows, n) > 0.5, logits, -jnp.inf)
        rs = slice(g * rows, (g + 1) * rows)
        m_old = m_ref[rs, :]
        m_new = jnp.maximum(m_old, jnp.max(x, axis=-1, keepdims=True))
        m_safe = jnp.where(m_new == -jnp.inf, 0.0, m_new)
        alpha = jnp.exp(m_old - m_safe)
        p = jnp.exp(x - m_safe[:, 0:1])
        l_ref[rs, :] = alpha * l_ref[rs, :] + jnp.sum(p, axis=-1, keepdims=True)
        acc_ref[rs, :] = alpha * acc_ref[rs, :] + jnp.dot(p.astype(_MXU_DTYPE), values.astype(_MXU_DTYPE),
                                                          preferred_element_type=F32)
        m_ref[rs, :] = m_new

    def head_bias(g, kind):
        return ts_ref[g * rep:(g + 1) * rep, kind].reshape(rows, _LANES)

    def far_bias(g):
        r_idx = lax.broadcasted_iota(I32, (rows, 1), 0) // t_dec
        out = jnp.zeros((rows, 1), F32)
        for r in range(rep):
            out = jnp.where(r_idx == r, rb_ref[n_buckets - 1, g * rep + r], out)
        return out

    for cp in copies(b, c, slot):
        cp.wait()

    is_last = (c == n_chunks - 1).astype(F32)
    for g in range(n_kv):
        qg = q_ref[g * rep:(g + 1) * rep].reshape(rows, hd).astype(_MXU_DTYPE)
        kc = kv_head(kbuf_ref, g)
        far = far_bias(g)
        lg = _nt_dot(qg, kc.astype(_MXU_DTYPE)) + far
        near = lg[:, ck - page:] + (head_bias(g, 0) - far) * is_last
        lg = jnp.concatenate([lg[:, :ck - page], near], axis=1)
        update(g, lg, mask_ref[...], kv_head(vbuf_ref, g))

    @pl.when(c == n_chunks - 1)
    def _():
        zpad = jnp.zeros((page - t_dec, hd), F32)
        for g in range(n_kv):
            qg = q_ref[g * rep:(g + 1) * rep].reshape(rows, hd).astype(_MXU_DTYPE)
            kn = jnp.concatenate([kn_ref[:, g * hd:(g + 1) * hd], zpad], axis=0)
            vn = jnp.concatenate([vn_ref[:, g * hd:(g + 1) * hd], zpad], axis=0)
            lg = _nt_dot(qg, kn.astype(_MXU_DTYPE)) + head_bias(g, 1)
            update(g, lg, mtail_ref[...], vn)
            rs = slice(g * rows, (g + 1) * rows)
            o = acc_ref[rs, :] / l_ref[rs, :] * sg_ref[g * rep:(g + 1) * rep].reshape(rows, hd)
            o_ref[g * rep:(g + 1) * rep] = o.reshape(rep, t_dec, hd)


def _sample_attention(page_table, rel_bias, q_s, mask, kn, vn, sg_s, ts, cache_k, cache_v, *, t_dec, cpages, page):
    bd, n_pages = page_table.shape
    n_heads, _, hd = q_s.shape
    n_kv = cache_k.shape[1] // page
    kvw = n_kv * hd
    rep = n_heads // n_kv
    n_chunks = n_pages // cpages
    ck = cpages * page
    past = n_pages * page
    hrow = lambda b, c, pt: (0, b, 0)
    grid_spec = pltpu.PrefetchScalarGridSpec(
        num_scalar_prefetch=1,
        grid=(bd, n_chunks),
        in_specs=[pl.BlockSpec(memory_space=pltpu.SMEM),
                  pl.BlockSpec((n_heads, t_dec, hd), hrow),
                  pl.BlockSpec((t_dec, ck), lambda b, c, pt: (b, c)),
                  pl.BlockSpec((t_dec, page), lambda b, c, pt: (b, past // page)),
                  pl.BlockSpec((t_dec, kvw), lambda b, c, pt: (b, 0)),
                  pl.BlockSpec((t_dec, kvw), lambda b, c, pt: (b, 0)),
                  pl.BlockSpec((n_heads, t_dec, hd), hrow),
                  pl.BlockSpec((n_heads, 2, t_dec, _LANES), lambda b, c, pt: (0, 0, 0, 0)),
                  pl.BlockSpec(memory_space=pl.ANY),
                  pl.BlockSpec(memory_space=pl.ANY)],
        out_specs=pl.BlockSpec((n_heads, t_dec, hd), hrow),
        scratch_shapes=[pltpu.VMEM((2, ck * n_kv, hd), F32),
                        pltpu.VMEM((2, ck * n_kv, hd), F32),
                        pltpu.VMEM((n_heads * t_dec, _LANES), F32),
                        pltpu.VMEM((n_heads * t_dec, _LANES), F32),
                        pltpu.VMEM((n_heads * t_dec, hd), F32),
                        pltpu.SemaphoreType.DMA((2, 2))],
    )
    return pl.pallas_call(
        functools.partial(_sample_attn_body, n_chunks=n_chunks, cpages=cpages, page=page, n_kv=n_kv,
                          rep=rep, t_dec=t_dec, n_buckets=rel_bias.shape[0]),
        grid_spec=grid_spec,
        out_shape=jax.ShapeDtypeStruct((n_heads, bd * t_dec, hd), F32),
        compiler_params=_cparams(2),
        name="sample_attention",
    )(page_table, rel_bias.astype(F32), q_s, mask, mask, kn, vn, sg_s, ts, cache_k, cache_v)


def _pool_body(u_ref, halo_ref, gb_ref, wp_ref, ps_ref, o_ref, ext_ref, *, tm, pos_base, own_halo, pg):
    i = pl.program_id(1)
    halo = halo_ref[...].reshape(_POOL_HALO, -1)
    if own_halo:
        halo = jnp.where(i > 0, halo, 0.0)
    ext_ref[0:_POOL_HALO, :] = halo
    ext_ref[_POOL_HALO:, :] = u_ref[...]
    pos = pos_base + i * tm + lax.broadcasted_iota(I32, (tm, 1), 0)
    for g, win in enumerate(_POOL_WINDOWS):
        cols = slice(g * pg, (g + 1) * pg)
        tot = ext_ref[_POOL_HALO:, cols]
        for k in range(1, win):
            tot = tot + ext_ref[_POOL_HALO - k:_POOL_HALO - k + tm, cols]
        cnt = jnp.minimum(pos + 1, win).astype(F32)
        d = tot / cnt - ext_ref[_POOL_HALO:, cols]
        y = jnp.dot(d.astype(_MXU_DTYPE), wp_ref[g], preferred_element_type=F32)
        o_ref[:, cols] = (y * ps_ref[:, cols] * gb_ref[:, cols].astype(F32)).astype(o_ref.dtype)


def _pool(u, halo, gb, wp, ps, *, batch, rows, tm, pos_base, own_halo, out_dtype):
    m, pw = u.shape
    nt = rows // tm
    pg = pw // len(_POOL_WINDOWS)
    if own_halo:
        hb = tm // _POOL_HALO
        halo_spec = pl.BlockSpec((_POOL_HALO, pw), lambda b, i: (jnp.maximum((b * nt + i) * hb - 1, 0), 0))
    else:
        halo_spec = pl.BlockSpec((1, _POOL_HALO, pw), lambda b, i: (b, 0, 0))
    return pl.pallas_call(
        functools.partial(_pool_body, tm=tm, pos_base=pos_base, own_halo=own_halo, pg=pg),
        grid=(batch, nt),
        in_specs=[pl.BlockSpec((tm, pw), lambda b, i: (b * nt + i, 0)),
                  halo_spec,
                  pl.BlockSpec((tm, pw), lambda b, i: (b * nt + i, 0)),
                  pl.BlockSpec(wp.shape, lambda b, i: (0, 0, 0)),
                  pl.BlockSpec((1, pw), lambda b, i: (0, 0))],
        out_specs=pl.BlockSpec((tm, pw), lambda b, i: (b * nt + i, 0)),
        out_shape=jax.ShapeDtypeStruct((m, pw), out_dtype),
        scratch_shapes=[pltpu.VMEM((_POOL_HALO + tm, pw), F32)],
        compiler_params=_cparams(2),
        name="pool",
    )(u, halo, gb, wp, ps)


def _merge_body(ma_ref, mb_ref, w_ref, x_ref, g_ref, o_ref, mix_ref, h_ref, *, n_j, tn):
    j = pl.program_id(1)
    n_heads, _, hd = ma_ref.shape

    @pl.when(j == 0)
    def _():
        for h in range(n_heads):
            mix_ref[:, h * hd:(h + 1) * hd] = ma_ref[h].astype(mix_ref.dtype)
        mix_ref[:, n_heads * hd:] = mb_ref[...].astype(mix_ref.dtype)

    h_ref[j] = x_ref[...] + jnp.dot(mix_ref[...], w_ref[...], preferred_element_type=F32)

    @pl.when(j == n_j - 1)
    def _():
        ssq = jnp.zeros((h_ref.shape[1], 1), F32)
        for jj in range(n_j):
            hv = h_ref[jj]
            ssq = ssq + jnp.sum(hv * hv, axis=-1, keepdims=True)
        inv = lax.rsqrt(ssq / (n_j * tn) + _RMS_EPS)
        for jj in range(n_j):
            o_ref[:, jj * tn:(jj + 1) * tn] = h_ref[jj] * inv * g_ref[:, jj * tn:(jj + 1) * tn]


def _merge(mix_a, mix_b, w_out, x2d, final_g, *, tm, tn):
    n_heads, m, hd = mix_a.shape
    pw = mix_b.shape[1]
    mixw, d = w_out.shape
    n_j = d // tn
    return pl.pallas_call(
        functools.partial(_merge_body, n_j=n_j, tn=tn),
        grid=(m // tm, n_j),
        in_specs=[pl.BlockSpec((n_heads, tm, hd), lambda i, j: (0, i, 0)),
                  pl.BlockSpec((tm, pw), lambda i, j: (i, 0)),
                  pl.BlockSpec((mixw, tn), lambda i, j: (0, j)),
                  pl.BlockSpec((tm, tn), lambda i, j: (i, j)),
                  pl.BlockSpec((1, d), lambda i, j: (0, 0))],
        out_specs=pl.BlockSpec((tm, d), lambda i, j: (i, 0)),
        out_shape=jax.ShapeDtypeStruct((m, d), F32),
        scratch_shapes=[pltpu.VMEM((tm, mixw), _MXU_DTYPE), pltpu.VMEM((n_j, tm, tn), F32)],
        compiler_params=_cparams(2),
        name="merge",
    )(mix_a, mix_b, w_out, x2d, final_g.reshape(1, d).astype(F32))


def kernel(x_prompt, x_sample, cache_k, cache_v, cache_idx_k, state_pool, page_table,
           rel_bias, ln_g, w_in, w_pool, pool_scale, w_out, final_g):
    depth = w_in.shape[0]
    assert depth == 1, "single-layer step only"
    batch, seq, d_model = x_prompt.shape
    bd, t_dec, _ = x_sample.shape
    _, n_phys, page, n_kv, hd = cache_k.shape
    idx_dim = cache_idx_k.shape[-1]
    n_buckets, n_heads = rel_bias.shape
    pw = pool_scale.shape[-1]
    n_pg = w_pool.shape[1]
    attn_w, kv_w = n_heads * hd, n_kv * hd
    in_w = w_in.shape[-1]
    idx_heads = (in_w - 2 * attn_w - 2 * kv_w - idx_dim - 2 * pw) // (idx_dim + 1)
    n_pages = page_table.shape[1]
    past = n_pages * page
    mp, ms = batch * seq, bd * t_dec
    tq = 2 * _LANES
    assert hd == _LANES and idx_dim == _LANES and idx_heads <= _LANES and n_pg == len(_POOL_WINDOWS)
    assert page >= _MAX_DISTANCE and _LANES >= _MAX_DISTANCE and seq % (2 * tq) == 0 and t_dec <= _POOL_HALO - 1
    assert state_pool.shape[2] == _POOL_HALO - 1

    o_q, o_k, o_v = 0, attn_w, attn_w + kv_w
    o_ga = o_v + kv_w
    o_qi = o_ga + attn_w
    o_wi = o_qi + idx_heads * idx_dim
    o_ki = o_wi + idx_heads
    o_u = o_ki + idx_dim
    o_gb = o_u + pw
    w2d = w_in[0]
    wcast = lambda lo, hi: w2d[:, lo:hi].astype(_MXU_DTYPE)

    xn_p = _rmsnorm(x_prompt.reshape(mp, d_model), ln_g[0], _MXU_DTYPE)
    xn_s = _rmsnorm(x_sample.reshape(ms, d_model), ln_g[0], _MXU_DTYPE)

    tm = min(512, mp)
    tn = lambda n: min(1024, n)
    proj = functools.partial(_proj_call, xn_p, xn_s, tm=tm)
    rm = lambda n, t, dt, s: _row_major(ms if s else mp, n, tm, t, dt, s)
    hm = lambda n, t, dt, s: _head_major(ms if s else mp, n, tm, t, dt, s)

    def scaled_heads(scale):
        def epi(acc, refs):
            _store_heads(refs[0], acc * scale)
        return epi

    def silu_heads(acc, refs):
        _store_heads(refs[0], _silu(acc))

    def silu_rows(acc, refs):
        refs[0][...] = _silu(acc).astype(refs[0].dtype)

    def plain_rows(acc, refs):
        refs[0][...] = acc.astype(refs[0].dtype)

    def kv_p(acc, refs):
        refs[0][...] = acc[:, :kv_w]
        refs[1][...] = acc[:, kv_w:]
        _store_heads(refs[2], acc[:, :kv_w])
        _store_heads(refs[3], acc[:, kv_w:])

    def kv_s(acc, refs):
        refs[0][...] = acc[:, :kv_w]
        refs[1][...] = acc[:, kv_w:]

    def wk_p(acc, refs):
        refs[0][...] = jnp.concatenate([acc[:, :_LANES] * idx_heads ** -0.5, acc[:, _LANES:]], axis=1)
        refs[1][...] = acc[:, _LANES:].astype(refs[1].dtype)

    def wk_s(acc, refs):
        refs[0][...] = jnp.concatenate([acc[:, :_LANES] * idx_heads ** -0.5, acc[:, _LANES:]], axis=1)

    t = tn(attn_w)
    q_hm, q_s = proj(wcast(o_q, o_k), t, outs_p=[hm(attn_w, t, _MXU_DTYPE, False)], outs_s=[hm(attn_w, t, F32, True)],
                     epi_p=scaled_heads(hd ** -0.5), epi_s=scaled_heads(hd ** -0.5), name="proj_q")
    t = 2 * kv_w
    k_p, v_p, k_hm, v_hm, k_s, v_s = proj(
        wcast(o_k, o_ga), t,
        outs_p=[((mp, kv_w), F32, (tm, kv_w), lambda j, i: (i, 0)), ((mp, kv_w), F32, (tm, kv_w), lambda j, i: (i, 0)),
                ((n_kv, mp, hd), _MXU_DTYPE, (n_kv, tm, hd), lambda j, i: (0, i, 0)),
                ((n_kv, mp, hd), _MXU_DTYPE, (n_kv, tm, hd), lambda j, i: (0, i, 0))],
        outs_s=[((ms, kv_w), F32, (ms, kv_w), lambda j, i: (0, 0)), ((ms, kv_w), F32, (ms, kv_w), lambda j, i: (0, 0))],
        epi_p=kv_p, epi_s=kv_s, name="proj_kv")
    t = tn(attn_w)
    sga_hm, sga_s = proj(wcast(o_ga, o_qi), t, outs_p=[hm(attn_w, t, _MXU_DTYPE, False)],
                         outs_s=[hm(attn_w, t, F32, True)], epi_p=silu_heads, epi_s=silu_heads, name="proj_ga")
    t = tn(idx_heads * idx_dim)
    qi_hm, qi_s = proj(wcast(o_qi, o_wi), t, outs_p=[hm(idx_heads * idx_dim, t, _MXU_DTYPE, False)],
                       outs_s=[hm(idx_heads * idx_dim, t, F32, True)],
                       epi_p=scaled_heads(idx_dim ** -0.5), epi_s=scaled_heads(idx_dim ** -0.5), name="proj_qi")
    w_wk = jnp.concatenate([wcast(o_wi, o_wi + _LANES), wcast(o_ki, o_u)], axis=1)
    t = 2 * _LANES
    wk_pr, ki_bf, wk_sm = proj(
        w_wk, t,
        outs_p=[((mp, t), F32, (tm, t), lambda j, i: (i, 0)), ((mp, idx_dim), _MXU_DTYPE, (tm, idx_dim), lambda j, i: (i, 0))],
        outs_s=[((ms, t), F32, (ms, t), lambda j, i: (0, 0))],
        epi_p=wk_p, epi_s=wk_s, name="proj_wk")
    t = tn(pw)
    u_p, u_s = proj(wcast(o_u, o_gb), t, outs_p=[rm(pw, t, F32, False)], outs_s=[rm(pw, t, F32, True)],
                    epi_p=plain_rows, epi_s=plain_rows, name="proj_u")
    sgb_p, sgb_s = proj(wcast(o_gb, in_w), t, outs_p=[rm(pw, t, _MXU_DTYPE, False)], outs_s=[rm(pw, t, F32, True)],
                        epi_p=silu_rows, epi_s=silu_rows, name="proj_gb")

    tb_p, tb_s = _bias_tables(rel_bias, _LANES, t_dec, page)

    mixa_p = _prompt_attention(qi_hm, wk_pr, ki_bf, q_hm, k_hm, v_hm, sga_hm, tb_p, batch=batch, seq=seq, tq=tq)

    scores = _sample_scores(page_table, qi_s, wk_sm, cache_idx_k[0], t_dec=t_dec)
    topk_s = min(_TOPK_MAX, (past + t_dec) // 4)
    mask_s = _sample_select(scores, topk=topk_s, past=past).reshape(ms, past + page)
    cpages = max(1, min(16, n_pages // 4))
    assert n_pages % cpages == 0
    mixa_s = _sample_attention(page_table, rel_bias, q_s, mask_s, k_s, v_s, sga_s, tb_s,
                               cache_k[0].reshape(n_phys, page * n_kv, hd), cache_v[0].reshape(n_phys, page * n_kv, hd),
                               t_dec=t_dec, cpages=cpages, page=page)

    wp = w_pool[0].astype(_MXU_DTYPE)
    ps = pool_scale[0].reshape(1, pw).astype(F32)
    mixb_p = _pool(u_p, u_p, sgb_p, wp, ps, batch=batch, rows=seq, tm=min(512, seq), pos_base=0,
                   own_halo=True, out_dtype=_MXU_DTYPE)
    halo_s = jnp.concatenate([jnp.zeros((bd, 1, pw), F32), state_pool[0].astype(F32)], axis=1)
    mixb_s = _pool(u_s, halo_s, sgb_s, wp, ps, batch=bd, rows=t_dec, tm=t_dec, pos_base=past,
                   own_halo=False, out_dtype=F32)

    w_o = w_out[0].astype(_MXU_DTYPE)
    tn_o = min(512, d_model)
    y_p = _merge(mixa_p, mixb_p, w_o, x_prompt.reshape(mp, d_model), final_g, tm=min(512, mp), tn=tn_o)
    y_s = _merge(mixa_s, mixb_s, w_o, x_sample.reshape(ms, d_model), final_g, tm=ms, tn=tn_o)

    n_state = state_pool.shape[2]
    u_full_s = jnp.concatenate([state_pool[0].astype(F32), u_s.reshape(bd, t_dec, pw)], axis=1)
    return (y_p.reshape(batch, seq, d_model), y_s.reshape(bd, t_dec, d_model),
            k_p.reshape(1, batch, seq, n_kv, hd), v_p.reshape(1, batch, seq, n_kv, hd),
            wk_pr[:, _LANES:].reshape(1, batch, seq, idx_dim),
            u_p.reshape(batch, seq, pw)[:, seq - n_state:][None],
            k_s.reshape(1, bd, t_dec, n_kv, hd), v_s.reshape(1, bd, t_dec, n_kv, hd),
            wk_sm[:, _LANES:].reshape(1, bd, t_dec, idx_dim),
            u_full_s[:, t_dec:][None])
```

```python
import functools
import math

import jax
import jax.numpy as jnp
from jax import lax
from jax.experimental import pallas as pl
from jax.experimental.pallas import tpu as pltpu

F32 = jnp.float32
BF16 = jnp.bfloat16
I32 = jnp.int32

_TOPK_MAX = 256
_POOL_WINDOWS = (2, 4, 8, 16)
_POOL_HALO = 16
_MAX_DISTANCE = 128
_RMS_EPS = 1e-6
_NEG = -1e30
_INT_MIN = -(2 ** 31)
_LANES = 128
_VMEM_LIMIT = 56 * 1024 * 1024
_MXU_DTYPE = BF16


def _cparams(n_axes):
    return pltpu.CompilerParams(dimension_semantics=("arbitrary",) * n_axes,
                                vmem_limit_bytes=_VMEM_LIMIT)


def _nt_dot(a, b):
    return lax.dot_general(a, b, (((1,), (1,)), ((), ())), preferred_element_type=F32)


def _sort_key(x):
    bits = lax.bitcast_convert_type(x + 0.0, I32)
    return bits ^ ((bits >> 31) & 0x7FFFFFFF)


def _rmsnorm_body(x_ref, g_ref, o_ref):
    x = x_ref[...].astype(F32)
    var = jnp.mean(x * x, axis=-1, keepdims=True)
    o_ref[...] = (x * lax.rsqrt(var + _RMS_EPS) * g_ref[...]).astype(o_ref.dtype)


def _rmsnorm(x2d, g, out_dtype):
    m, d = x2d.shape
    tm = min(256, m)
    return pl.pallas_call(
        _rmsnorm_body,
        grid=(m // tm,),
        in_specs=[pl.BlockSpec((tm, d), lambda i: (i, 0)), pl.BlockSpec((1, d), lambda i: (0, 0))],
        out_specs=pl.BlockSpec((tm, d), lambda i: (i, 0)),
        out_shape=jax.ShapeDtypeStruct((m, d), out_dtype),
        compiler_params=_cparams(1),
        name="rmsnorm",
    )(x2d, g.reshape(1, d).astype(F32))


def _proj_body(xp_ref, xs_ref, w_ref, *refs, n_p, n_out, epi_p, epi_s):
    out_refs = refs[:n_out]
    first_row_tile = pl.program_id(1) == 0
    if len(refs) > n_out:
        wc_ref = refs[n_out]

        @pl.when(first_row_tile)
        def _():
            wc_ref[...] = w_ref[...].astype(wc_ref.dtype)

        w = wc_ref[...]
    else:
        w = w_ref[...]
    epi_p(jnp.dot(xp_ref[...], w, preferred_element_type=F32), out_refs[:n_p])

    @pl.when(first_row_tile)
    def _():
        epi_s(jnp.dot(xs_ref[...], w, preferred_element_type=F32), out_refs[n_p:])


def _proj_call(xn_p, xn_s, w, col0, ncols, tn, tm, outs_p, outs_s, epi_p, epi_s, name):
    mp, d = xn_p.shape
    ms = xn_s.shape[0]
    assert col0 % tn == 0 and ncols % tn == 0 and mp % tm == 0
    jb = col0 // tn
    outs = list(outs_p) + list(outs_s)
    cast_w = w.dtype != xn_p.dtype
    return pl.pallas_call(
        functools.partial(_proj_body, n_p=len(outs_p), n_out=len(outs), epi_p=epi_p, epi_s=epi_s),
        grid=(ncols // tn, mp // tm),
        in_specs=[pl.BlockSpec((tm, d), lambda j, i: (i, 0)),
                  pl.BlockSpec((ms, d), lambda j, i: (0, 0)),
                  pl.BlockSpec((d, tn), lambda j, i: (0, jb + j))],
        out_specs=[pl.BlockSpec(o[2], o[3]) for o in outs],
        out_shape=[jax.ShapeDtypeStruct(o[0], o[1]) for o in outs],
        scratch_shapes=[pltpu.VMEM((d, tn), xn_p.dtype)] if cast_w else [],
        compiler_params=_cparams(2),
        name=name,
    )(xn_p, xn_s, w)


def _row_major(m, ncols, tm, tn, dtype, sample):
    if sample:
        return ((m, ncols), dtype, (m, tn), lambda j, i: (0, j))
    return ((m, ncols), dtype, (tm, tn), lambda j, i: (i, j))


def _head_major(m, ncols, tm, tn, dtype, sample):
    nh, hb = ncols // _LANES, tn // _LANES
    if sample:
        return ((nh, m, _LANES), dtype, (hb, m, _LANES), lambda j, i: (j, 0, 0))
    return ((nh, m, _LANES), dtype, (hb, tm, _LANES), lambda j, i: (j, i, 0))


def _store_heads(ref, val):
    for hh in range(ref.shape[0]):
        ref[hh] = val[:, hh * _LANES:(hh + 1) * _LANES].astype(ref.dtype)


def _silu(x):
    return x * jax.nn.sigmoid(x)


def _bias_tables_body(rb_ref, tp_ref, ts_ref, *, n_buckets, tq, t_dec, page):
    h = pl.program_id(0)
    max_exact = n_buckets // 2

    def table(dist):
        d = jnp.maximum(dist, 0)
        df = jnp.maximum(d.astype(F32), 1.0)
        large = max_exact + (jnp.log(df / max_exact) / math.log(_MAX_DISTANCE / max_exact)
                             * (n_buckets - max_exact)).astype(I32)
        bucket = jnp.where(d < max_exact, d, jnp.minimum(large, n_buckets - 1))
        out = jnp.zeros(dist.shape, F32)
        for b in range(n_buckets):
            out = jnp.where(bucket == b, rb_ref[b, h], out)
        return out

    rp = lax.broadcasted_iota(I32, (tq, _LANES), 0)
    cp = lax.broadcasted_iota(I32, (tq, _LANES), 1)
    far = rb_ref[n_buckets - 1, h]
    tp_ref[0, 0] = table(rp - cp) - far
    tp_ref[0, 1] = table(rp - cp + tq) - far
    rs = lax.broadcasted_iota(I32, (t_dec, _LANES), 0)
    cs = lax.broadcasted_iota(I32, (t_dec, _LANES), 1)
    ts_ref[0, 0] = table(rs - cs + page)
    ts_ref[0, 1] = table(rs - cs)


def _bias_tables(rel_bias, tq, t_dec, page):
    n_buckets, n_heads = rel_bias.shape
    return pl.pallas_call(
        functools.partial(_bias_tables_body, n_buckets=n_buckets, tq=tq, t_dec=t_dec, page=page),
        grid=(n_heads,),
        in_specs=[pl.BlockSpec(memory_space=pltpu.SMEM)],
        out_specs=[pl.BlockSpec((1, 2, tq, _LANES), lambda h: (h, 0, 0, 0)),
                   pl.BlockSpec((1, 2, t_dec, _LANES), lambda h: (h, 0, 0, 0))],
        out_shape=[jax.ShapeDtypeStruct((n_heads, 2, tq, _LANES), F32),
                   jax.ShapeDtypeStruct((n_heads, 2, t_dec, _LANES), F32)],
        compiler_params=_cparams(1),
        name="bias_tables",
    )(rel_bias.astype(F32))


def _prompt_attn_body(qi_ref, wk_ref, ki_ref, q_ref, k_ref, v_ref, sg_ref, tb_ref, o_ref,
                      wt_ref, sc_ref, keys_ref, madd_ref, lg_ref, mp_ref, mb_ref, lp_ref, acc_ref,
                      *, tq, topk, idx_heads, n_kv, rep):
    i = pl.program_id(1)
    n_ch = i + 1
    hd = q_ref.shape[-1]
    rows = rep * tq
    nb = tq // _LANES
    hgrp = 2
    kc2 = 2 * tq

    def chunk(c):
        return pl.ds(pl.multiple_of(c * tq, tq), tq)

    def causal(c):
        key_pos = lax.broadcasted_iota(I32, (tq, tq), 0) + c * tq
        return key_pos <= lax.broadcasted_iota(I32, (tq, tq), 1) + i * tq

    wt_ref[...] = wk_ref[...].T

    def score_body(c2, carry):
        ks = pl.ds(pl.multiple_of(c2 * kc2, kc2), kc2)
        keys = ki_ref[ks, :]
        a = jnp.zeros((kc2, tq), F32)
        for hg in range(idx_heads // hgrp):
            qi2 = qi_ref[hg * hgrp:(hg + 1) * hgrp].reshape(hgrp * tq, -1)
            s = _nt_dot(keys, qi2)
            for r in range(hgrp):
                h = hg * hgrp + r
                a = a + jnp.maximum(s[:, r * tq:(r + 1) * tq], 0.0) * wt_ref[h:h + 1, :]
        sc_ref[ks, :] = a
        return carry

    lax.fori_loop(0, (n_ch + 1) // 2, score_body, 0)

    def key_body(c, carry):
        keys_ref[chunk(c), :] = jnp.where(causal(c), _sort_key(sc_ref[chunk(c), :]), _INT_MIN)
        return carry

    lax.fori_loop(0, n_ch, key_body, 0)

    def count(pred):
        def c_body(c, acc):
            hit = pred(keys_ref[chunk(c), :]).astype(F32)
            return acc + jnp.sum(hit.reshape(tq // 8, 8, tq), axis=0)

        return jnp.sum(lax.fori_loop(0, n_ch, c_body, jnp.zeros((8, tq), F32)), axis=0, keepdims=True)

    def radix_step(it, thr):
        cand = thr + jnp.left_shift(jnp.int32(1), 31 - it)
        return jnp.where(count(lambda k: k >= cand) >= topk, cand, thr)

    thr = lax.fori_loop(0, 32, radix_step, jnp.full((1, tq), _INT_MIN, I32))
    need = topk - count(lambda k: k > thr)
    n_eq = count(lambda k: k == thr)
    tied = jnp.max(jnp.where((n_eq > need) & (thr != _INT_MIN), 1.0, 0.0)) > 0.0

    def madd_body(c, carry):
        keep = (keys_ref[chunk(c), :] >= thr) & causal(c)
        madd_ref[c] = jnp.where(keep, 0.0, _NEG).T
        return carry

    lax.fori_loop(0, n_ch, madd_body, 0)

    @pl.when(tied)
    def _():
        tri = (lax.broadcasted_iota(I32, (tq, tq), 1) <= lax.broadcasted_iota(I32, (tq, tq), 0)).astype(BF16)

        def tie_body(c, run):
            kc = keys_ref[chunk(c), :]
            eq = kc == thr
            pre = jnp.dot(tri, eq.astype(F32).astype(BF16), preferred_element_type=F32) + run
            keep = ((kc > thr) | (eq & (pre <= need))) & causal(c)
            madd_ref[c] = jnp.where(keep, 0.0, _NEG).T
            return run + jnp.sum(eq.astype(F32), axis=0, keepdims=True)

        lax.fori_loop(0, n_ch, tie_body, jnp.zeros((1, tq), F32))

    gpb = lg_ref.shape[0]

    def gp_body(gp, carry):
        groups = [gp * gpb + u for u in range(gpb)]
        q4 = [q_ref[pl.ds(g * rep, rep)].reshape(rows, hd) for g in groups]
        mp_ref[...] = jnp.full(mp_ref.shape, _NEG, F32)

        def logits(c, near):
            madd = madd_ref[c]
            for u, g in enumerate(groups):
                s = _nt_dot(q4[u], k_ref[g, chunk(c), :])
                for r in range(rep):
                    for a in range(nb):
                        rs = slice(r * tq + a * _LANES, r * tq + (a + 1) * _LANES)
                        x = s[rs, :] + madd[a * _LANES:(a + 1) * _LANES, :]
                        if near is not None:
                            cols = []
                            for b in range(nb):
                                xb = x[:, b * _LANES:(b + 1) * _LANES]
                                kind = near * nb + a - b
                                if kind in (0, 1):
                                    xb = xb + tb_ref[g * rep + r, kind]
                                cols.append(xb)
                            x = jnp.concatenate(cols, axis=1)
                        lg_ref[u, c, rs, :] = x
                        xm = x[:, 0:_LANES]
                        for b in range(1, nb):
                            xm = jnp.maximum(xm, x[:, b * _LANES:(b + 1) * _LANES])
                        mp_ref[u, rs, :] = jnp.maximum(mp_ref[u, rs, :], xm)

        def far_body(c, carry2):
            logits(c, None)
            return carry2

        lax.fori_loop(0, jnp.maximum(i - 1, 0), far_body, 0)

        @pl.when(i >= 1)
        def _():
            logits(i - 1, 1)

        logits(i, 0)

        mb_ref[...] = jnp.broadcast_to(jnp.max(mp_ref[...], axis=-1, keepdims=True), mb_ref.shape)
        lp_ref[...] = jnp.zeros(lp_ref.shape, F32)
        acc_ref[...] = jnp.zeros(acc_ref.shape, F32)

        def pv_body(c, carry2):
            for u, g in enumerate(groups):
                m = mb_ref[u]
                e = jnp.exp(lg_ref[u, c] - jnp.concatenate([m] * nb, axis=1))
                es = e[:, 0:_LANES]
                for b in range(1, nb):
                    es = es + e[:, b * _LANES:(b + 1) * _LANES]
                lp_ref[u] += es
                acc_ref[u] += jnp.dot(e.astype(_MXU_DTYPE), v_ref[g, chunk(c), :], preferred_element_type=F32)
            return carry2

        lax.fori_loop(0, n_ch, pv_body, 0)
        for u, g in enumerate(groups):
            o = acc_ref[u] / jnp.sum(lp_ref[u], axis=-1, keepdims=True)
            o = o * sg_ref[pl.ds(g * rep, rep)].reshape(rows, hd).astype(F32)
            o_ref[pl.ds(g * rep, rep)] = o.reshape(rep, tq, hd).astype(o_ref.dtype)
        return carry

    lax.fori_loop(0, n_kv // gpb, gp_body, 0)


def _prompt_attention(qi_hm, wk, ki_bf, q_hm, k_hm, v_hm, sg_hm, tb_p, *, batch, seq, tq):
    idx_heads, m, idx_dim = qi_hm.shape
    n_heads, _, hd = q_hm.shape
    n_kv = k_hm.shape[0]
    rep = n_heads // n_kv
    n_q = seq // tq
    topk = min(_TOPK_MAX, seq // 4)
    gpb = 2 if n_kv % 2 == 0 else 1
    qrow = lambda b, i: (0, b * n_q + i, 0)
    body = functools.partial(_prompt_attn_body, tq=tq, topk=topk, idx_heads=idx_heads, n_kv=n_kv, rep=rep)
    return pl.pallas_call(
        body,
        grid=(batch, n_q),
        in_specs=[pl.BlockSpec((idx_heads, tq, idx_dim), qrow),
                  pl.BlockSpec((tq, _LANES), lambda b, i: (b * n_q + i, 0)),
                  pl.BlockSpec((seq, idx_dim), lambda b, i: (b, 0)),
                  pl.BlockSpec((n_heads, tq, hd), qrow),
                  pl.BlockSpec((n_kv, seq, hd), lambda b, i: (0, b, 0)),
                  pl.BlockSpec((n_kv, seq, hd), lambda b, i: (0, b, 0)),
                  pl.BlockSpec((n_heads, tq, hd), qrow),
                  pl.BlockSpec((n_heads, 2, _LANES, _LANES), lambda b, i: (0, 0, 0, 0))],
        out_specs=pl.BlockSpec((n_heads, tq, hd), qrow),
        out_shape=jax.ShapeDtypeStruct((n_heads, m, hd), _MXU_DTYPE),
        scratch_shapes=[pltpu.VMEM((_LANES, tq), F32),
                        pltpu.VMEM((seq, tq), F32),
                        pltpu.VMEM((seq, tq), I32),
                        pltpu.VMEM((n_q, tq, tq), F32),
                        pltpu.VMEM((gpb, n_q, rep * tq, tq), F32),
                        pltpu.VMEM((gpb, rep * tq, _LANES), F32),
                        pltpu.VMEM((gpb, rep * tq, _LANES), F32),
                        pltpu.VMEM((gpb, rep * tq, _LANES), F32),
                        pltpu.VMEM((gpb, rep * tq, hd), F32)],
        compiler_params=_cparams(2),
        name="prompt_attention",
    )(qi_hm, wk, ki_bf, q_hm, k_hm, v_hm, sg_hm, tb_p)


def _page_copies(pt_ref, cache_ref, buf_ref, sem_ref, b, first_page, n_copy, slot, page):
    return [pltpu.make_async_copy(cache_ref.at[pt_ref[b, first_page + p]],
                                  buf_ref.at[slot, pl.ds(p * page, page)],
                                  sem_ref.at[slot]) for p in range(n_copy)]


def _sample_scores_body(pt_ref, qi_ref, wk_ref, cache_ref, o_ref, buf_ref, wb_ref, sem_ref,
                        *, n_pages, page, idx_heads, t_dec):
    b = pl.program_id(0)
    nb = pl.num_programs(0)
    slot = b % 2
    copies = functools.partial(_page_copies, pt_ref, cache_ref, buf_ref, sem_ref,
                               first_page=0, n_copy=n_pages, page=page)

    @pl.when(b == 0)
    def _():
        for c in copies(b=0, slot=0):
            c.start()

    @pl.when(b + 1 < nb)
    def _():
        for c in copies(b=b + 1, slot=1 - slot):
            c.start()

    qi = qi_ref[...].reshape(idx_heads * t_dec, -1).astype(_MXU_DTYPE)
    w_blk = wk_ref[:, 0:_LANES]
    for h in range(idx_heads):
        wb_ref[h * t_dec:(h + 1) * t_dec, :] = jnp.broadcast_to(w_blk[:, h:h + 1], (t_dec, _LANES))

    def page_scores(keys):
        s = _nt_dot(qi, keys.astype(_MXU_DTYPE))
        return jnp.sum((jnp.maximum(s, 0.0) * wb_ref[...]).reshape(idx_heads, t_dec, page), axis=0)

    past = n_pages * page
    ki_new = jnp.concatenate([wk_ref[:, _LANES:], jnp.zeros((page - t_dec, wk_ref.shape[1] - _LANES), F32)], axis=0)
    row = lax.broadcasted_iota(I32, (t_dec, page), 0)
    col = lax.broadcasted_iota(I32, (t_dec, page), 1)
    o_ref[0, :, past:past + page] = jnp.where(col <= row, page_scores(ki_new), -jnp.inf)

    for c in copies(b=b, slot=slot):
        c.wait()
    for p in range(n_pages):
        o_ref[0, :, p * page:(p + 1) * page] = page_scores(buf_ref[slot, p * page:(p + 1) * page, :])


def _sample_scores(page_table, qi_s, wk_s, cache_idx, *, t_dec):
    bd, n_pages = page_table.shape
    _, page, idx_dim = cache_idx.shape
    idx_heads = qi_s.shape[0]
    past = n_pages * page
    grid_spec = pltpu.PrefetchScalarGridSpec(
        num_scalar_prefetch=1,
        grid=(bd,),
        in_specs=[pl.BlockSpec((idx_heads, t_dec, idx_dim), lambda b, pt: (0, b, 0)),
                  pl.BlockSpec((t_dec, wk_s.shape[1]), lambda b, pt: (b, 0)),
                  pl.BlockSpec(memory_space=pl.ANY)],
        out_specs=pl.BlockSpec((1, t_dec, past + page), lambda b, pt: (b, 0, 0)),
        scratch_shapes=[pltpu.VMEM((2, past, idx_dim), F32),
                        pltpu.VMEM((idx_heads * t_dec, _LANES), F32),
                        pltpu.SemaphoreType.DMA((2,))],
    )
    return pl.pallas_call(
        functools.partial(_sample_scores_body, n_pages=n_pages, page=page, idx_heads=idx_heads, t_dec=t_dec),
        grid_spec=grid_spec,
        out_shape=jax.ShapeDtypeStruct((bd, t_dec, past + page), F32),
        compiler_params=_cparams(1),
        name="sample_scores",
    )(page_table, qi_s, wk_s, cache_idx)


def _sample_select_body(sc_ref, o_ref, keys_ref, *, topk, past):
    bd, t_dec, n_cols = sc_ref.shape

    def valid(c0, width):
        col = lax.broadcasted_iota(I32, (bd, t_dec, width), 2) + c0
        return col <= past + lax.broadcasted_iota(I32, (bd, t_dec, width), 1)

    keys_ref[...] = jnp.where(valid(0, n_cols), _sort_key(sc_ref[...]), _INT_MIN)

    def step(it, thr):
        cand = thr + jnp.left_shift(jnp.int32(1), 31 - it)
        cnt = jnp.sum((keys_ref[...] >= cand).astype(F32), axis=-1, keepdims=True)
        return jnp.where(cnt >= topk, cand, thr)

    thr = lax.fori_loop(0, 32, step, jnp.full((bd, t_dec, 1), _INT_MIN, I32))
    keys = keys_ref[...]
    cnt_gt = jnp.sum((keys > thr).astype(F32), axis=-1, keepdims=True)
    cnt_eq = jnp.sum((keys == thr).astype(F32), axis=-1, keepdims=True)
    need = topk - cnt_gt
    o_ref[...] = ((keys >= thr) & valid(0, n_cols)).astype(F32)
    tied = jnp.max(jnp.where((cnt_eq > need) & (thr != _INT_MIN), 1.0, 0.0)) > 0.0

    @pl.when(tied)
    def _():
        tri = (lax.broadcasted_iota(I32, (_LANES, _LANES), 0)
               <= lax.broadcasted_iota(I32, (_LANES, _LANES), 1)).astype(BF16)
        run = jnp.zeros((bd, t_dec, 1), F32)
        for c0 in range(0, n_cols, _LANES):
            kc = keys_ref[:, :, c0:c0 + _LANES]
            eq = kc == thr
            eq2 = eq.astype(F32).reshape(bd * t_dec, _LANES).astype(BF16)
            pre = jnp.dot(eq2, tri, preferred_element_type=F32).reshape(bd, t_dec, _LANES) + run
            run = run + jnp.sum(eq.astype(F32), axis=-1, keepdims=True)
            keep = (kc > thr) | (eq & (pre <= need))
            o_ref[:, :, c0:c0 + _LANES] = (keep & valid(c0, _LANES)).astype(F32)


def _sample_select(scores, *, topk, past):
    return pl.pallas_call(
        functools.partial(_sample_select_body, topk=topk, past=past),
        out_shape=jax.ShapeDtypeStruct(scores.shape, F32),
        scratch_shapes=[pltpu.VMEM(scores.shape, I32)],
        compiler_params=pltpu.CompilerParams(vmem_limit_bytes=_VMEM_LIMIT),
        name="sample_select",
    )(scores)


def _sample_attn_body(pt_ref, rb_ref, q_ref, mask_ref, mtail_ref, kn_ref, vn_ref, sg_ref, ts_ref,
                      ck_ref, cv_ref, o_ref, kbuf_ref, vbuf_ref, m_ref, l_ref, acc_ref, sem_ref,
                      *, n_chunks, cpages, page, n_kv, rep, t_dec, n_buckets):
    b = pl.program_id(0)
    c = pl.program_id(1)
    step = b * n_chunks + c
    n_steps = pl.num_programs(0) * n_chunks
    slot = step % 2
    hd = q_ref.shape[-1]
    rows = rep * t_dec
    ck = cpages * page

    prow = page * n_kv

    def copies(bb, cc, sl):
        return (_page_copies(pt_ref, ck_ref, kbuf_ref, sem_ref.at[0], bb, cc * cpages, cpages, sl, prow)
                + _page_copies(pt_ref, cv_ref, vbuf_ref, sem_ref.at[1], bb, cc * cpages, cpages, sl, prow))

    def kv_head(buf_ref, g):
        return buf_ref[slot, pl.ds(g, ck, stride=n_kv), :]

    @pl.when(step == 0)
    def _():
        for cp in copies(0, 0, 0):
            cp.start()

    @pl.when(step + 1 < n_steps)
    def _():
        nxt = step + 1
        for cp in copies(nxt // n_chunks, nxt % n_chunks, 1 - slot):
            cp.start()

    @pl.when(c == 0)
    def _():
        m_ref[...] = jnp.full(m_ref.shape, -jnp.inf, F32)
        l_ref[...] = jnp.zeros(l_ref.shape, F32)
        acc_ref[...] = jnp.zeros(acc_ref.shape, F32)

    def update(g, logits, mask, values):
        n = logits.shape[-1]
        x = jnp.where(jnp.broadcast_to(mask[None], (rep, t_dec, n)).reshape(rows, n) > 0.5, logits, -jnp.inf)
        rs = slice(g * rows, (g + 1) * rows)
        m_old = m_ref[rs, :]
        m_new = jnp.maximum(m_old, jnp.max(x, axis=-1, keepdims=True))
        m_safe = jnp.where(m_new == -jnp.inf, 0.0, m_new)
        alpha = jnp.exp(m_old - m_safe)
        p = jnp.exp(x - m_safe[:, 0:1])
        l_ref[rs, :] = alpha * l_ref[rs, :] + jnp.sum(p, axis=-1, keepdims=True)
        acc_ref[rs, :] = alpha * acc_ref[rs, :] + jnp.dot(p.astype(_MXU_DTYPE), values.astype(_MXU_DTYPE),
                                                          preferred_element_type=F32)
        m_ref[rs, :] = m_new

    def head_bias(g, kind):
        return ts_ref[g * rep:(g + 1) * rep, kind].reshape(rows, _LANES)

    def far_bias(g):
        r_idx = lax.broadcasted_iota(I32, (rows, 1), 0) // t_dec
        out = jnp.zeros((rows, 1), F32)
        for r in range(rep):
            out = jnp.where(r_idx == r, rb_ref[n_buckets - 1, g * rep + r], out)
        return out

    for cp in copies(b, c, slot):
        cp.wait()

    is_last = (c == n_chunks - 1).astype(F32)
    for g in range(n_kv):
        qg = q_ref[g * rep:(g + 1) * rep].reshape(rows, hd).astype(_MXU_DTYPE)
        kc = kv_head(kbuf_ref, g)
        far = far_bias(g)
        lg = _nt_dot(qg, kc.astype(_MXU_DTYPE)) + far
        near = lg[:, ck - page:] + (head_bias(g, 0) - far) * is_last
        lg = jnp.concatenate([lg[:, :ck - page], near], axis=1)
        update(g, lg, mask_ref[...], kv_head(vbuf_ref, g))

    @pl.when(c == n_chunks - 1)
    def _():
        zpad = jnp.zeros((page - t_dec, hd), F32)
        for g in range(n_kv):
            qg = q_ref[g * rep:(g + 1) * rep].reshape(rows, hd).astype(_MXU_DTYPE)
            kn = jnp.concatenate([kn_ref[:, g * hd:(g + 1) * hd], zpad], axis=0)
            vn = jnp.concatenate([vn_ref[:, g * hd:(g + 1) * hd], zpad], axis=0)
            lg = _nt_dot(qg, kn.astype(_MXU_DTYPE)) + head_bias(g, 1)
            update(g, lg, mtail_ref[...], vn)
            rs = slice(g * rows, (g + 1) * rows)
            o = acc_ref[rs, :] / l_ref[rs, :] * sg_ref[g * rep:(g + 1) * rep].reshape(rows, hd)
            o_ref[g * rep:(g + 1) * rep] = o.reshape(rep, t_dec, hd)


def _sample_attention(page_table, rel_bias, q_s, mask, kn, vn, sg_s, ts, cache_k, cache_v, *, t_dec, cpages, page):
    bd, n_pages = page_table.shape
    n_heads, _, hd = q_s.shape
    n_kv = cache_k.shape[1] // page
    kvw = n_kv * hd
    rep = n_heads // n_kv
    n_chunks = n_pages // cpages
    ck = cpages * page
    past = n_pages * page
    hrow = lambda b, c, pt: (0, b, 0)
    grid_spec = pltpu.PrefetchScalarGridSpec(
        num_scalar_prefetch=1,
        grid=(bd, n_chunks),
        in_specs=[pl.BlockSpec(memory_space=pltpu.SMEM),
                  pl.BlockSpec((n_heads, t_dec, hd), hrow),
                  pl.BlockSpec((t_dec, ck), lambda b, c, pt: (b, c)),
                  pl.BlockSpec((t_dec, page), lambda b, c, pt: (b, past // page)),
                  pl.BlockSpec((t_dec, kvw), lambda b, c, pt: (b, 0)),
                  pl.BlockSpec((t_dec, kvw), lambda b, c, pt: (b, 0)),
                  pl.BlockSpec((n_heads, t_dec, hd), hrow),
                  pl.BlockSpec((n_heads, 2, t_dec, _LANES), lambda b, c, pt: (0, 0, 0, 0)),
                  pl.BlockSpec(memory_space=pl.ANY),
                  pl.BlockSpec(memory_space=pl.ANY)],
        out_specs=pl.BlockSpec((n_heads, t_dec, hd), hrow),
        scratch_shapes=[pltpu.VMEM((2, ck * n_kv, hd), F32),
                        pltpu.VMEM((2, ck * n_kv, hd), F32),
                        pltpu.VMEM((n_heads * t_dec, _LANES), F32),
                        pltpu.VMEM((n_heads * t_dec, _LANES), F32),
                        pltpu.VMEM((n_heads * t_dec, hd), F32),
                        pltpu.SemaphoreType.DMA((2, 2))],
    )
    return pl.pallas_call(
        functools.partial(_sample_attn_body, n_chunks=n_chunks, cpages=cpages, page=page, n_kv=n_kv,
                          rep=rep, t_dec=t_dec, n_buckets=rel_bias.shape[0]),
        grid_spec=grid_spec,
        out_shape=jax.ShapeDtypeStruct((n_heads, bd * t_dec, hd), F32),
        compiler_params=_cparams(2),
        name="sample_attention",
    )(page_table, rel_bias.astype(F32), q_s, mask, mask, kn, vn, sg_s, ts, cache_k, cache_v)


def _pool_body(u_ref, halo_ref, gb_ref, wp_ref, ps_ref, o_ref, ext_ref, *, tm, pos_base, own_halo, pg):
    i = pl.program_id(1)
    halo = halo_ref[...].reshape(_POOL_HALO, -1)
    if own_halo:
        halo = jnp.where(i > 0, halo, 0.0)
    ext_ref[0:_POOL_HALO, :] = halo
    ext_ref[_POOL_HALO:, :] = u_ref[...]
    pos = pos_base + i * tm + lax.broadcasted_iota(I32, (tm, 1), 0)
    for g, win in enumerate(_POOL_WINDOWS):
        cols = slice(g * pg, (g + 1) * pg)
        tot = ext_ref[_POOL_HALO:, cols]
        for k in range(1, win):
            tot = tot + ext_ref[_POOL_HALO - k:_POOL_HALO - k + tm, cols]
        cnt = jnp.minimum(pos + 1, win).astype(F32)
        d = tot / cnt - ext_ref[_POOL_HALO:, cols]
        y = jnp.dot(d.astype(_MXU_DTYPE), wp_ref[g], preferred_element_type=F32)
        o_ref[:, cols] = (y * ps_ref[:, cols] * gb_ref[:, cols].astype(F32)).astype(o_ref.dtype)


def _pool(u, halo, gb, wp, ps, *, batch, rows, tm, pos_base, own_halo, out_dtype):
    m, pw = u.shape
    nt = rows // tm
    pg = pw // len(_POOL_WINDOWS)
    if own_halo:
        hb = tm // _POOL_HALO
        halo_spec = pl.BlockSpec((_POOL_HALO, pw), lambda b, i: (jnp.maximum((b * nt + i) * hb - 1, 0), 0))
    else:
        halo_spec = pl.BlockSpec((1, _POOL_HALO, pw), lambda b, i: (b, 0, 0))
    return pl.pallas_call(
        functools.partial(_pool_body, tm=tm, pos_base=pos_base, own_halo=own_halo, pg=pg),
        grid=(batch, nt),
        in_specs=[pl.BlockSpec((tm, pw), lambda b, i: (b * nt + i, 0)),
                  halo_spec,
                  pl.BlockSpec((tm, pw), lambda b, i: (b * nt + i, 0)),
                  pl.BlockSpec(wp.shape, lambda b, i: (0, 0, 0)),
                  pl.BlockSpec((1, pw), lambda b, i: (0, 0))],
        out_specs=pl.BlockSpec((tm, pw), lambda b, i: (b * nt + i, 0)),
        out_shape=jax.ShapeDtypeStruct((m, pw), out_dtype),
        scratch_shapes=[pltpu.VMEM((_POOL_HALO + tm, pw), F32)],
        compiler_params=_cparams(2),
        name="pool",
    )(u, halo, gb, wp, ps)


def _merge_body(ma_ref, mb_ref, w_ref, x_ref, g_ref, o_ref, mix_ref, h_ref, *, n_j, tn):
    j = pl.program_id(1)
    n_heads, _, hd = ma_ref.shape

    @pl.when(j == 0)
    def _():
        for h in range(n_heads):
            mix_ref[:, h * hd:(h + 1) * hd] = ma_ref[h].astype(mix_ref.dtype)
        mix_ref[:, n_heads * hd:] = mb_ref[...].astype(mix_ref.dtype)

    h_ref[j] = x_ref[...] + jnp.dot(mix_ref[...], w_ref[...], preferred_element_type=F32)

    @pl.when(j == n_j - 1)
    def _():
        ssq = jnp.zeros((h_ref.shape[1], 1), F32)
        for jj in range(n_j):
            hv = h_ref[jj]
            ssq = ssq + jnp.sum(hv * hv, axis=-1, keepdims=True)
        inv = lax.rsqrt(ssq / (n_j * tn) + _RMS_EPS)
        for jj in range(n_j):
            o_ref[:, jj * tn:(jj + 1) * tn] = h_ref[jj] * inv * g_ref[:, jj * tn:(jj + 1) * tn]


def _merge(mix_a, mix_b, w_out, x2d, final_g, *, tm, tn):
    n_heads, m, hd = mix_a.shape
    pw = mix_b.shape[1]
    mixw, d = w_out.shape
    n_j = d // tn
    return pl.pallas_call(
        functools.partial(_merge_body, n_j=n_j, tn=tn),
        grid=(m // tm, n_j),
        in_specs=[pl.BlockSpec((n_heads, tm, hd), lambda i, j: (0, i, 0)),
                  pl.BlockSpec((tm, pw), lambda i, j: (i, 0)),
                  pl.BlockSpec((mixw, tn), lambda i, j: (0, j)),
                  pl.BlockSpec((tm, tn), lambda i, j: (i, j)),
                  pl.BlockSpec((1, d), lambda i, j: (0, 0))],
        out_specs=pl.BlockSpec((tm, d), lambda i, j: (i, 0)),
        out_shape=jax.ShapeDtypeStruct((m, d), F32),
        scratch_shapes=[pltpu.VMEM((tm, mixw), _MXU_DTYPE), pltpu.VMEM((n_j, tm, tn), F32)],
        compiler_params=_cparams(2),
        name="merge",
    )(mix_a, mix_b, w_out, x2d, final_g.reshape(1, d).astype(F32))


def kernel(x_prompt, x_sample, cache_k, cache_v, cache_idx_k, state_pool, page_table,
           rel_bias, ln_g, w_in, w_pool, pool_scale, w_out, final_g):
    depth = w_in.shape[0]
    assert depth == 1, "single-layer step only"
    batch, seq, d_model = x_prompt.shape
    bd, t_dec, _ = x_sample.shape
    _, n_phys, page, n_kv, hd = cache_k.shape
    idx_dim = cache_idx_k.shape[-1]
    n_buckets, n_heads = rel_bias.shape
    pw = pool_scale.shape[-1]
    n_pg = w_pool.shape[1]
    attn_w, kv_w = n_heads * hd, n_kv * hd
    in_w = w_in.shape[-1]
    idx_heads = (in_w - 2 * attn_w - 2 * kv_w - idx_dim - 2 * pw) // (idx_dim + 1)
    n_pages = page_table.shape[1]
    past = n_pages * page
    mp, ms = batch * seq, bd * t_dec
    tq = 2 * _LANES
    assert hd == _LANES and idx_dim == _LANES and idx_heads <= _LANES and n_pg == len(_POOL_WINDOWS)
    assert page >= _MAX_DISTANCE and _LANES >= _MAX_DISTANCE and seq % (2 * tq) == 0 and t_dec <= _POOL_HALO - 1
    assert state_pool.shape[2] == _POOL_HALO - 1

    o_q, o_k, o_v = 0, attn_w, attn_w + kv_w
    o_ga = o_v + kv_w
    o_qi = o_ga + attn_w
    o_wi = o_qi + idx_heads * idx_dim
    o_ki = o_wi + idx_heads
    o_u = o_ki + idx_dim
    o_gb = o_u + pw
    w2d = w_in[0]
    wcast = lambda lo, hi: w2d[:, lo:hi].astype(_MXU_DTYPE)

    xn_p = _rmsnorm(x_prompt.reshape(mp, d_model), ln_g[0], _MXU_DTYPE)
    xn_s = _rmsnorm(x_sample.reshape(ms, d_model), ln_g[0], _MXU_DTYPE)

    tm = min(1024, mp)
    tn = lambda n: min(512, n)
    rm = lambda n, t, dt, s: _row_major(ms if s else mp, n, tm, t, dt, s)
    hm = lambda n, t, dt, s: _head_major(ms if s else mp, n, tm, t, dt, s)

    def proj(lo, hi, t, **kw):
        if lo % t == 0:
            return _proj_call(xn_p, xn_s, w2d, lo, hi - lo, t, tm, **kw)
        return _proj_call(xn_p, xn_s, wcast(lo, hi), 0, hi - lo, t, tm, **kw)

    def scaled_heads(scale):
        def epi(acc, refs):
            _store_heads(refs[0], acc * scale)
        return epi

    def silu_heads(acc, refs):
        _store_heads(refs[0], _silu(acc))

    def silu_rows(acc, refs):
        refs[0][...] = _silu(acc).astype(refs[0].dtype)

    def plain_rows(acc, refs):
        refs[0][...] = acc.astype(refs[0].dtype)

    def kv_p(acc, refs):
        refs[0][...] = acc
        _store_heads(refs[1], acc)

    def wk_p(acc, refs):
        refs[0][...] = jnp.concatenate([acc[:, :_LANES] * idx_heads ** -0.5, acc[:, _LANES:]], axis=1)
        refs[1][...] = acc[:, _LANES:].astype(refs[1].dtype)

    def wk_s(acc, refs):
        refs[0][...] = jnp.concatenate([acc[:, :_LANES] * idx_heads ** -0.5, acc[:, _LANES:]], axis=1)

    t = tn(attn_w)
    q_hm, q_s = proj(o_q, o_k, t, outs_p=[hm(attn_w, t, _MXU_DTYPE, False)], outs_s=[hm(attn_w, t, F32, True)],
                     epi_p=scaled_heads(hd ** -0.5), epi_s=scaled_heads(hd ** -0.5), name="proj_q")
    kv_outs = dict(outs_p=[((mp, kv_w), F32, (tm, kv_w), lambda j, i: (i, 0)),
                           ((n_kv, mp, hd), _MXU_DTYPE, (n_kv, tm, hd), lambda j, i: (0, i, 0))],
                   outs_s=[((ms, kv_w), F32, (ms, kv_w), lambda j, i: (0, 0))], epi_p=kv_p, epi_s=plain_rows)
    k_p, k_hm, k_s = proj(o_k, o_v, kv_w, name="proj_k", **kv_outs)
    v_p, v_hm, v_s = proj(o_v, o_ga, kv_w, name="proj_v", **kv_outs)
    sga_hm, sga_s = proj(o_ga, o_qi, t, outs_p=[hm(attn_w, t, _MXU_DTYPE, False)],
                         outs_s=[hm(attn_w, t, F32, True)], epi_p=silu_heads, epi_s=silu_heads, name="proj_ga")
    t = tn(idx_heads * idx_dim)
    qi_hm, qi_s = proj(o_qi, o_wi, t, outs_p=[hm(idx_heads * idx_dim, t, _MXU_DTYPE, False)],
                       outs_s=[hm(idx_heads * idx_dim, t, F32, True)],
                       epi_p=scaled_heads(idx_dim ** -0.5), epi_s=scaled_heads(idx_dim ** -0.5), name="proj_qi")
    w_wk = jnp.concatenate([wcast(o_wi, o_wi + _LANES), wcast(o_ki, o_u)], axis=1)
    t = 2 * _LANES
    wk_pr, ki_bf, wk_sm = _proj_call(
        xn_p, xn_s, w_wk, 0, t, t, tm,
        outs_p=[((mp, t), F32, (tm, t), lambda j, i: (i, 0)), ((mp, idx_dim), _MXU_DTYPE, (tm, idx_dim), lambda j, i: (i, 0))],
        outs_s=[((ms, t), F32, (ms, t), lambda j, i: (0, 0))],
        epi_p=wk_p, epi_s=wk_s, name="proj_wk")
    t = tn(pw)
    u_p, u_s = proj(o_u, o_gb, t, outs_p=[rm(pw, t, F32, False)], outs_s=[rm(pw, t, F32, True)],
                    epi_p=plain_rows, epi_s=plain_rows, name="proj_u")
    sgb_p, sgb_s = proj(o_gb, in_w, t, outs_p=[rm(pw, t, _MXU_DTYPE, False)], outs_s=[rm(pw, t, F32, True)],
                        epi_p=silu_rows, epi_s=silu_rows, name="proj_gb")

    tb_p, tb_s = _bias_tables(rel_bias, _LANES, t_dec, page)

    mixa_p = _prompt_attention(qi_hm, wk_pr, ki_bf, q_hm, k_hm, v_hm, sga_hm, tb_p, batch=batch, seq=seq, tq=tq)

    scores = _sample_scores(page_table, qi_s, wk_sm, cache_idx_k[0], t_dec=t_dec)
    topk_s = min(_TOPK_MAX, (past + t_dec) // 4)
    mask_s = _sample_select(scores, topk=topk_s, past=past).reshape(ms, past + page)
    cpages = max(1, min(16, n_pages // 4))
    assert n_pages % cpages == 0
    mixa_s = _sample_attention(page_table, rel_bias, q_s, mask_s, k_s, v_s, sga_s, tb_s,
                               cache_k[0].reshape(n_phys, page * n_kv, hd), cache_v[0].reshape(n_phys, page * n_kv, hd),
                               t_dec=t_dec, cpages=cpages, page=page)

    wp = w_pool[0].astype(_MXU_DTYPE)
    ps = pool_scale[0].reshape(1, pw).astype(F32)
    mixb_p = _pool(u_p, u_p, sgb_p, wp, ps, batch=batch, rows=seq, tm=min(512, seq), pos_base=0,
                   own_halo=True, out_dtype=_MXU_DTYPE)
    halo_s = jnp.concatenate([jnp.zeros((bd, 1, pw), F32), state_pool[0].astype(F32)], axis=1)
    mixb_s = _pool(u_s, halo_s, sgb_s, wp, ps, batch=bd, rows=t_dec, tm=t_dec, pos_base=past,
                   own_halo=False, out_dtype=F32)

    w_o = w_out[0].astype(_MXU_DTYPE)
    tn_o = min(512, d_model)
    y_p = _merge(mixa_p, mixb_p, w_o, x_prompt.reshape(mp, d_model), final_g, tm=min(512, mp), tn=tn_o)
    y_s = _merge(mixa_s, mixb_s, w_o, x_sample.reshape(ms, d_model), final_g, tm=ms, tn=tn_o)

    n_state = state_pool.shape[2]
    u_full_s = jnp.concatenate([state_pool[0].astype(F32), u_s.reshape(bd, t_dec, pw)], axis=1)
    return (y_p.reshape(batch, seq, d_model), y_s.reshape(bd, t_dec, d_model),
            k_p.reshape(1, batch, seq, n_kv, hd), v_p.reshape(1, batch, seq, n_kv, hd),
            wk_pr[:, _LANES:].reshape(1, batch, seq, idx_dim),
            u_p.reshape(batch, seq, pw)[:, seq - n_state:][None],
            k_s.reshape(1, bd, t_dec, n_kv, hd), v_s.reshape(1, bd, t_dec, n_kv, hd),
            wk_sm[:, _LANES:].reshape(1, bd, t_dec, idx_dim),
            u_full_s[:, t_dec:][None])
```

```python
import functools
import math

import jax
import jax.numpy as jnp
from jax import lax
from jax.experimental import pallas as pl
from jax.experimental.pallas import tpu as pltpu

F32 = jnp.float32
BF16 = jnp.bfloat16
I32 = jnp.int32

_TOPK_MAX = 256
_POOL_WINDOWS = (2, 4, 8, 16)
_POOL_HALO = 16
_MAX_DISTANCE = 128
_RMS_EPS = 1e-6
_NEG = -1e30
_INT_MIN = -(2 ** 31)
_LANES = 128
_VMEM_LIMIT = 56 * 1024 * 1024
_MXU_DTYPE = BF16


def _cparams(n_axes):
    return pltpu.CompilerParams(dimension_semantics=("arbitrary",) * n_axes,
                                vmem_limit_bytes=_VMEM_LIMIT)


def _nt_dot(a, b):
    return lax.dot_general(a, b, (((1,), (1,)), ((), ())), preferred_element_type=F32)


def _sort_key(x):
    bits = lax.bitcast_convert_type(x + 0.0, I32)
    return bits ^ ((bits >> 31) & 0x7FFFFFFF)


def _rmsnorm_body(x_ref, g_ref, o_ref):
    x = x_ref[...].astype(F32)
    var = jnp.mean(x * x, axis=-1, keepdims=True)
    o_ref[...] = (x * lax.rsqrt(var + _RMS_EPS) * g_ref[...]).astype(o_ref.dtype)


def _rmsnorm(x2d, g, out_dtype):
    m, d = x2d.shape
    tm = min(256, m)
    return pl.pallas_call(
        _rmsnorm_body,
        grid=(m // tm,),
        in_specs=[pl.BlockSpec((tm, d), lambda i: (i, 0)), pl.BlockSpec((1, d), lambda i: (0, 0))],
        out_specs=pl.BlockSpec((tm, d), lambda i: (i, 0)),
        out_shape=jax.ShapeDtypeStruct((m, d), out_dtype),
        compiler_params=_cparams(1),
        name="rmsnorm",
    )(x2d, g.reshape(1, d).astype(F32))


def _proj_body(xp_ref, xs_ref, *refs, n_w, n_p, n_out, epi_p, epi_s):
    w_refs, out_refs, wc_ref = refs[:n_w], refs[n_w:n_w + n_out], refs[n_w + n_out]
    first_row_tile = pl.program_id(1) == 0

    @pl.when(first_row_tile)
    def _():
        r0 = 0
        for w_ref in w_refs:
            wc_ref[r0:r0 + w_ref.shape[0], :] = w_ref[...].astype(wc_ref.dtype)
            r0 += w_ref.shape[0]

    epi_p(_nt_dot(xp_ref[...], wc_ref[...]), out_refs[:n_p])

    @pl.when(first_row_tile)
    def _():
        epi_s(_nt_dot(xs_ref[...], wc_ref[...]), out_refs[n_p:])


def _proj_call(xn_p, xn_s, wt, w_rows, n_tiles, tm, outs_p, outs_s, epi_p, epi_s, name):
    mp, d = xn_p.shape
    ms = xn_s.shape[0]
    assert mp % tm == 0
    tn = sum(cnt for _, cnt in w_rows)
    outs = list(outs_p) + list(outs_s)

    def w_spec(first, cnt):
        assert first % 8 == 0 and cnt % 8 == 0
        return pl.BlockSpec((pl.Element(cnt), pl.Element(d)), lambda j, i: (pl.multiple_of(first + j * cnt, 8), 0))

    return pl.pallas_call(
        functools.partial(_proj_body, n_w=len(w_rows), n_p=len(outs_p), n_out=len(outs), epi_p=epi_p, epi_s=epi_s),
        grid=(n_tiles, mp // tm),
        in_specs=[pl.BlockSpec((tm, d), lambda j, i: (i, 0)),
                  pl.BlockSpec((ms, d), lambda j, i: (0, 0))] + [w_spec(*r) for r in w_rows],
        out_specs=[pl.BlockSpec(o[2], o[3]) for o in outs],
        out_shape=[jax.ShapeDtypeStruct(o[0], o[1]) for o in outs],
        scratch_shapes=[pltpu.VMEM((tn, d), xn_p.dtype)],
        compiler_params=_cparams(2),
        name=name,
    )(xn_p, xn_s, *([wt] * len(w_rows)))


def _row_major(m, ncols, tm, tn, dtype, sample):
    if sample:
        return ((m, ncols), dtype, (m, tn), lambda j, i: (0, j))
    return ((m, ncols), dtype, (tm, tn), lambda j, i: (i, j))


def _head_major(m, ncols, tm, tn, dtype, sample):
    nh, hb = ncols // _LANES, tn // _LANES
    if sample:
        return ((nh, m, _LANES), dtype, (hb, m, _LANES), lambda j, i: (j, 0, 0))
    return ((nh, m, _LANES), dtype, (hb, tm, _LANES), lambda j, i: (j, i, 0))


def _store_heads(ref, val):
    for hh in range(ref.shape[0]):
        ref[hh] = val[:, hh * _LANES:(hh + 1) * _LANES].astype(ref.dtype)


def _silu(x):
    return x * jax.nn.sigmoid(x)


def _bias_tables_body(rb_ref, tp_ref, ts_ref, *, n_buckets, tq, t_dec, page):
    h = pl.program_id(0)
    max_exact = n_buckets // 2

    def table(dist):
        d = jnp.maximum(dist, 0)
        df = jnp.maximum(d.astype(F32), 1.0)
        large = max_exact + (jnp.log(df / max_exact) / math.log(_MAX_DISTANCE / max_exact)
                             * (n_buckets - max_exact)).astype(I32)
        bucket = jnp.where(d < max_exact, d, jnp.minimum(large, n_buckets - 1))
        out = jnp.zeros(dist.shape, F32)
        for b in range(n_buckets):
            out = jnp.where(bucket == b, rb_ref[b, h], out)
        return out

    rp = lax.broadcasted_iota(I32, (tq, _LANES), 0)
    cp = lax.broadcasted_iota(I32, (tq, _LANES), 1)
    far = rb_ref[n_buckets - 1, h]
    tp_ref[0, 0] = table(rp - cp) - far
    tp_ref[0, 1] = table(rp - cp + tq) - far
    rs = lax.broadcasted_iota(I32, (t_dec, _LANES), 0)
    cs = lax.broadcasted_iota(I32, (t_dec, _LANES), 1)
    ts_ref[0, 0] = table(rs - cs + page)
    ts_ref[0, 1] = table(rs - cs)


def _bias_tables(rel_bias, tq, t_dec, page):
    n_buckets, n_heads = rel_bias.shape
    return pl.pallas_call(
        functools.partial(_bias_tables_body, n_buckets=n_buckets, tq=tq, t_dec=t_dec, page=page),
        grid=(n_heads,),
        in_specs=[pl.BlockSpec(memory_space=pltpu.SMEM)],
        out_specs=[pl.BlockSpec((1, 2, tq, _LANES), lambda h: (h, 0, 0, 0)),
                   pl.BlockSpec((1, 2, t_dec, _LANES), lambda h: (h, 0, 0, 0))],
        out_shape=[jax.ShapeDtypeStruct((n_heads, 2, tq, _LANES), F32),
                   jax.ShapeDtypeStruct((n_heads, 2, t_dec, _LANES), F32)],
        compiler_params=_cparams(1),
        name="bias_tables",
    )(rel_bias.astype(F32))


def _prompt_attn_body(qi_ref, wk_ref, ki_ref, q_ref, k_ref, v_ref, sg_ref, tb_ref, o_ref,
                      wt_ref, sc_ref, keys_ref, madd_ref, lg_ref, mp_ref, mb_ref, lp_ref, acc_ref,
                      *, tq, topk, idx_heads, n_kv, rep):
    i = pl.program_id(1)
    n_ch = i + 1
    hd = q_ref.shape[-1]
    rows = rep * tq
    nb = tq // _LANES
    hgrp = 2
    kc2 = 2 * tq

    def chunk(c):
        return pl.ds(pl.multiple_of(c * tq, tq), tq)

    def causal(c):
        key_pos = lax.broadcasted_iota(I32, (tq, tq), 0) + c * tq
        return key_pos <= lax.broadcasted_iota(I32, (tq, tq), 1) + i * tq

    wt_ref[...] = wk_ref[...].T

    def score_body(c2, carry):
        ks = pl.ds(pl.multiple_of(c2 * kc2, kc2), kc2)
        keys = ki_ref[ks, :]
        a = jnp.zeros((kc2, tq), F32)
        for hg in range(idx_heads // hgrp):
            qi2 = qi_ref[hg * hgrp:(hg + 1) * hgrp].reshape(hgrp * tq, -1)
            s = _nt_dot(keys, qi2)
            for r in range(hgrp):
                h = hg * hgrp + r
                a = a + jnp.maximum(s[:, r * tq:(r + 1) * tq], 0.0) * wt_ref[h:h + 1, :]
        sc_ref[ks, :] = a
        return carry

    lax.fori_loop(0, (n_ch + 1) // 2, score_body, 0)

    def key_body(c, carry):
        keys_ref[chunk(c), :] = jnp.where(causal(c), _sort_key(sc_ref[chunk(c), :]), _INT_MIN)
        return carry

    lax.fori_loop(0, n_ch, key_body, 0)

    def count(pred):
        def c_body(c, acc):
            hit = pred(keys_ref[chunk(c), :]).astype(F32)
            return acc + jnp.sum(hit.reshape(tq // 8, 8, tq), axis=0)

        return jnp.sum(lax.fori_loop(0, n_ch, c_body, jnp.zeros((8, tq), F32)), axis=0, keepdims=True)

    def radix_step(it, thr):
        cand = thr + jnp.left_shift(jnp.int32(1), 31 - it)
        return jnp.where(count(lambda k: k >= cand) >= topk, cand, thr)

    thr = lax.fori_loop(0, 32, radix_step, jnp.full((1, tq), _INT_MIN, I32))
    need = topk - count(lambda k: k > thr)
    n_eq = count(lambda k: k == thr)
    tied = jnp.max(jnp.where((n_eq > need) & (thr != _INT_MIN), 1.0, 0.0)) > 0.0

    def madd_body(c, carry):
        keep = (keys_ref[chunk(c), :] >= thr) & causal(c)
        madd_ref[c] = jnp.where(keep, 0.0, _NEG).T
        return carry

    lax.fori_loop(0, n_ch, madd_body, 0)

    @pl.when(tied)
    def _():
        tri = (lax.broadcasted_iota(I32, (tq, tq), 1) <= lax.broadcasted_iota(I32, (tq, tq), 0)).astype(BF16)

        def tie_body(c, run):
            kc = keys_ref[chunk(c), :]
            eq = kc == thr
            pre = jnp.dot(tri, eq.astype(F32).astype(BF16), preferred_element_type=F32) + run
            keep = ((kc > thr) | (eq & (pre <= need))) & causal(c)
            madd_ref[c] = jnp.where(keep, 0.0, _NEG).T
            return run + jnp.sum(eq.astype(F32), axis=0, keepdims=True)

        lax.fori_loop(0, n_ch, tie_body, jnp.zeros((1, tq), F32))

    gpb = lg_ref.shape[0]

    def gp_body(gp, carry):
        groups = [gp * gpb + u for u in range(gpb)]
        q4 = [q_ref[pl.ds(g * rep, rep)].reshape(rows, hd) for g in groups]
        mp_ref[...] = jnp.full(mp_ref.shape, _NEG, F32)

        def logits(c, near):
            madd = madd_ref[c]
            for u, g in enumerate(groups):
                s = _nt_dot(q4[u], k_ref[g, chunk(c), :])
                for r in range(rep):
                    for a in range(nb):
                        rs = slice(r * tq + a * _LANES, r * tq + (a + 1) * _LANES)
                        x = s[rs, :] + madd[a * _LANES:(a + 1) * _LANES, :]
                        if near is not None:
                            cols = []
                            for b in range(nb):
                                xb = x[:, b * _LANES:(b + 1) * _LANES]
                                kind = near * nb + a - b
                                if kind in (0, 1):
                                    xb = xb + tb_ref[g * rep + r, kind]
                                cols.append(xb)
                            x = jnp.concatenate(cols, axis=1)
                        lg_ref[u, c, rs, :] = x
                        xm = x[:, 0:_LANES]
                        for b in range(1, nb):
                            xm = jnp.maximum(xm, x[:, b * _LANES:(b + 1) * _LANES])
                        mp_ref[u, rs, :] = jnp.maximum(mp_ref[u, rs, :], xm)

        def far_body(c, carry2):
            logits(c, None)
            return carry2

        lax.fori_loop(0, jnp.maximum(i - 1, 0), far_body, 0)

        @pl.when(i >= 1)
        def _():
            logits(i - 1, 1)

        logits(i, 0)

        mb_ref[...] = jnp.broadcast_to(jnp.max(mp_ref[...], axis=-1, keepdims=True), mb_ref.shape)
        lp_ref[...] = jnp.zeros(lp_ref.shape, F32)
        acc_ref[...] = jnp.zeros(acc_ref.shape, F32)

        def pv_body(c, carry2):
            for u, g in enumerate(groups):
                m = mb_ref[u]
                e = jnp.exp(lg_ref[u, c] - jnp.concatenate([m] * nb, axis=1))
                es = e[:, 0:_LANES]
                for b in range(1, nb):
                    es = es + e[:, b * _LANES:(b + 1) * _LANES]
                lp_ref[u] += es
                acc_ref[u] += jnp.dot(e.astype(_MXU_DTYPE), v_ref[g, chunk(c), :], preferred_element_type=F32)
            return carry2

        lax.fori_loop(0, n_ch, pv_body, 0)
        for u, g in enumerate(groups):
            o = acc_ref[u] / jnp.sum(lp_ref[u], axis=-1, keepdims=True)
            o = o * sg_ref[pl.ds(g * rep, rep)].reshape(rows, hd).astype(F32)
            o_ref[pl.ds(g * rep, rep)] = o.reshape(rep, tq, hd).astype(o_ref.dtype)
        return carry

    lax.fori_loop(0, n_kv // gpb, gp_body, 0)


def _prompt_attention(qi_hm, wk, ki_bf, q_hm, k_hm, v_hm, sg_hm, tb_p, *, batch, seq, tq):
    idx_heads, m, idx_dim = qi_hm.shape
    n_heads, _, hd = q_hm.shape
    n_kv = k_hm.shape[0]
    rep = n_heads // n_kv
    n_q = seq // tq
    topk = min(_TOPK_MAX, seq // 4)
    gpb = 2 if n_kv % 2 == 0 else 1
    qrow = lambda b, i: (0, b * n_q + i, 0)
    body = functools.partial(_prompt_attn_body, tq=tq, topk=topk, idx_heads=idx_heads, n_kv=n_kv, rep=rep)
    return pl.pallas_call(
        body,
        grid=(batch, n_q),
        in_specs=[pl.BlockSpec((idx_heads, tq, idx_dim), qrow),
                  pl.BlockSpec((tq, _LANES), lambda b, i: (b * n_q + i, 0)),
                  pl.BlockSpec((seq, idx_dim), lambda b, i: (b, 0)),
                  pl.BlockSpec((n_heads, tq, hd), qrow),
                  pl.BlockSpec((n_kv, seq, hd), lambda b, i: (0, b, 0)),
                  pl.BlockSpec((n_kv, seq, hd), lambda b, i: (0, b, 0)),
                  pl.BlockSpec((n_heads, tq, hd), qrow),
                  pl.BlockSpec((n_heads, 2, _LANES, _LANES), lambda b, i: (0, 0, 0, 0))],
        out_specs=pl.BlockSpec((n_heads, tq, hd), qrow),
        out_shape=jax.ShapeDtypeStruct((n_heads, m, hd), _MXU_DTYPE),
        scratch_shapes=[pltpu.VMEM((_LANES, tq), F32),
                        pltpu.VMEM((seq, tq), F32),
                        pltpu.VMEM((seq, tq), I32),
                        pltpu.VMEM((n_q, tq, tq), F32),
                        pltpu.VMEM((gpb, n_q, rep * tq, tq), F32),
                        pltpu.VMEM((gpb, rep * tq, _LANES), F32),
                        pltpu.VMEM((gpb, rep * tq, _LANES), F32),
                        pltpu.VMEM((gpb, rep * tq, _LANES), F32),
                        pltpu.VMEM((gpb, rep * tq, hd), F32)],
        compiler_params=_cparams(2),
        name="prompt_attention",
    )(qi_hm, wk, ki_bf, q_hm, k_hm, v_hm, sg_hm, tb_p)


def _page_copies(pt_ref, cache_ref, buf_ref, sem_ref, b, first_page, n_copy, slot, page):
    return [pltpu.make_async_copy(cache_ref.at[pt_ref[b, first_page + p]],
                                  buf_ref.at[slot, pl.ds(p * page, page)],
                                  sem_ref.at[slot]) for p in range(n_copy)]


def _sample_scores_body(pt_ref, qi_ref, wk_ref, cache_ref, o_ref, buf_ref, wb_ref, sem_ref,
                        *, n_pages, page, idx_heads, t_dec):
    b = pl.program_id(0)
    nb = pl.num_programs(0)
    slot = b % 2
    copies = functools.partial(_page_copies, pt_ref, cache_ref, buf_ref, sem_ref,
                               first_page=0, n_copy=n_pages, page=page)

    @pl.when(b == 0)
    def _():
        for c in copies(b=0, slot=0):
            c.start()

    @pl.when(b + 1 < nb)
    def _():
        for c in copies(b=b + 1, slot=1 - slot):
            c.start()

    qi = qi_ref[...].reshape(idx_heads * t_dec, -1).astype(_MXU_DTYPE)
    w_blk = wk_ref[:, 0:_LANES]
    for h in range(idx_heads):
        wb_ref[h * t_dec:(h + 1) * t_dec, :] = jnp.broadcast_to(w_blk[:, h:h + 1], (t_dec, _LANES))

    def page_scores(keys):
        s = _nt_dot(qi, keys.astype(_MXU_DTYPE))
        return jnp.sum((jnp.maximum(s, 0.0) * wb_ref[...]).reshape(idx_heads, t_dec, page), axis=0)

    past = n_pages * page
    ki_new = jnp.concatenate([wk_ref[:, _LANES:], jnp.zeros((page - t_dec, wk_ref.shape[1] - _LANES), F32)], axis=0)
    row = lax.broadcasted_iota(I32, (t_dec, page), 0)
    col = lax.broadcasted_iota(I32, (t_dec, page), 1)
    o_ref[0, :, past:past + page] = jnp.where(col <= row, page_scores(ki_new), -jnp.inf)

    for c in copies(b=b, slot=slot):
        c.wait()
    for p in range(n_pages):
        o_ref[0, :, p * page:(p + 1) * page] = page_scores(buf_ref[slot, p * page:(p + 1) * page, :])


def _sample_scores(page_table, qi_s, wk_s, cache_idx, *, t_dec):
    bd, n_pages = page_table.shape
    _, page, idx_dim = cache_idx.shape
    idx_heads = qi_s.shape[0]
    past = n_pages * page
    grid_spec = pltpu.PrefetchScalarGridSpec(
        num_scalar_prefetch=1,
        grid=(bd,),
        in_specs=[pl.BlockSpec((idx_heads, t_dec, idx_dim), lambda b, pt: (0, b, 0)),
                  pl.BlockSpec((t_dec, wk_s.shape[1]), lambda b, pt: (b, 0)),
                  pl.BlockSpec(memory_space=pl.ANY)],
        out_specs=pl.BlockSpec((1, t_dec, past + page), lambda b, pt: (b, 0, 0)),
        scratch_shapes=[pltpu.VMEM((2, past, idx_dim), F32),
                        pltpu.VMEM((idx_heads * t_dec, _LANES), F32),
                        pltpu.SemaphoreType.DMA((2,))],
    )
    return pl.pallas_call(
        functools.partial(_sample_scores_body, n_pages=n_pages, page=page, idx_heads=idx_heads, t_dec=t_dec),
        grid_spec=grid_spec,
        out_shape=jax.ShapeDtypeStruct((bd, t_dec, past + page), F32),
        compiler_params=_cparams(1),
        name="sample_scores",
    )(page_table, qi_s, wk_s, cache_idx)


def _sample_select_body(sc_ref, o_ref, keys_ref, *, topk, past):
    bd, t_dec, n_cols = sc_ref.shape

    def valid(c0, width):
        col = lax.broadcasted_iota(I32, (bd, t_dec, width), 2) + c0
        return col <= past + lax.broadcasted_iota(I32, (bd, t_dec, width), 1)

    keys_ref[...] = jnp.where(valid(0, n_cols), _sort_key(sc_ref[...]), _INT_MIN)

    def step(it, thr):
        cand = thr + jnp.left_shift(jnp.int32(1), 31 - it)
        cnt = jnp.sum((keys_ref[...] >= cand).astype(F32), axis=-1, keepdims=True)
        return jnp.where(cnt >= topk, cand, thr)

    thr = lax.fori_loop(0, 32, step, jnp.full((bd, t_dec, 1), _INT_MIN, I32))
    keys = keys_ref[...]
    cnt_gt = jnp.sum((keys > thr).astype(F32), axis=-1, keepdims=True)
    cnt_eq = jnp.sum((keys == thr).astype(F32), axis=-1, keepdims=True)
    need = topk - cnt_gt
    o_ref[...] = ((keys >= thr) & valid(0, n_cols)).astype(F32)
    tied = jnp.max(jnp.where((cnt_eq > need) & (thr != _INT_MIN), 1.0, 0.0)) > 0.0

    @pl.when(tied)
    def _():
        tri = (lax.broadcasted_iota(I32, (_LANES, _LANES), 0)
               <= lax.broadcasted_iota(I32, (_LANES, _LANES), 1)).astype(BF16)
        run = jnp.zeros((bd, t_dec, 1), F32)
        for c0 in range(0, n_cols, _LANES):
            kc = keys_ref[:, :, c0:c0 + _LANES]
            eq = kc == thr
            eq2 = eq.astype(F32).reshape(bd * t_dec, _LANES).astype(BF16)
            pre = jnp.dot(eq2, tri, preferred_element_type=F32).reshape(bd, t_dec, _LANES) + run
            run = run + jnp.sum(eq.astype(F32), axis=-1, keepdims=True)
            keep = (kc > thr) | (eq & (pre <= need))
            o_ref[:, :, c0:c0 + _LANES] = (keep & valid(c0, _LANES)).astype(F32)


def _sample_select(scores, *, topk, past):
    return pl.pallas_call(
        functools.partial(_sample_select_body, topk=topk, past=past),
        out_shape=jax.ShapeDtypeStruct(scores.shape, F32),
        scratch_shapes=[pltpu.VMEM(scores.shape, I32)],
        compiler_params=pltpu.CompilerParams(vmem_limit_bytes=_VMEM_LIMIT),
        name="sample_select",
    )(scores)


def _sample_attn_body(pt_ref, rb_ref, q_ref, mask_ref, mtail_ref, kn_ref, vn_ref, sg_ref, ts_ref,
                      ck_ref, cv_ref, o_ref, kbuf_ref, vbuf_ref, m_ref, l_ref, acc_ref, sem_ref,
                      *, n_chunks, cpages, page, n_kv, rep, t_dec, n_buckets):
    b = pl.program_id(0)
    c = pl.program_id(1)
    step = b * n_chunks + c
    n_steps = pl.num_programs(0) * n_chunks
    slot = step % 2
    hd = q_ref.shape[-1]
    rows = rep * t_dec
    ck = cpages * page

    prow = page * n_kv

    def copies(bb, cc, sl):
        return (_page_copies(pt_ref, ck_ref, kbuf_ref, sem_ref.at[0], bb, cc * cpages, cpages, sl, prow)
                + _page_copies(pt_ref, cv_ref, vbuf_ref, sem_ref.at[1], bb, cc * cpages, cpages, sl, prow))

    def kv_head(buf_ref, g):
        return buf_ref[slot, pl.ds(g, ck, stride=n_kv), :]

    @pl.when(step == 0)
    def _():
        for cp in copies(0, 0, 0):
            cp.start()

    @pl.when(step + 1 < n_steps)
    def _():
        nxt = step + 1
        for cp in copies(nxt // n_chunks, nxt % n_chunks, 1 - slot):
            cp.start()

    @pl.when(c == 0)
    def _():
        m_ref[...] = jnp.full(m_ref.shape, -jnp.inf, F32)
        l_ref[...] = jnp.zeros(l_ref.shape, F32)
        acc_ref[...] = jnp.zeros(acc_ref.shape, F32)

    def update(g, logits, mask, values):
        n = logits.shape[-1]
        x = jnp.where(jnp.broadcast_to(mask[None], (rep, t_dec, n)).reshape(rows, n) > 0.5, logits, -jnp.inf)
        rs = slice(g * rows, (g + 1) * rows)
        m_old = m_ref[rs, :]
        m_new = jnp.maximum(m_old, jnp.max(x, axis=-1, keepdims=True))
        m_safe = jnp.where(m_new == -jnp.inf, 0.0, m_new)
        alpha = jnp.exp(m_old - m_safe)
        p = jnp.exp(x - m_safe[:, 0:1])
        l_ref[rs, :] = alpha * l_ref[rs, :] + jnp.sum(p, axis=-1, keepdims=True)
        acc_ref[rs, :] = alpha * acc_ref[rs, :] + jnp.dot(p.astype(_MXU_DTYPE), values.astype(_MXU_DTYPE),
                                                          preferred_element_type=F32)
        m_ref[rs, :] = m_new

    def head_bias(g, kind):
        return ts_ref[g * rep:(g + 1) * rep, kind].reshape(rows, _LANES)

    def far_bias(g):
        r_idx = lax.broadcasted_iota(I32, (rows, 1), 0) // t_dec
        out = jnp.zeros((rows, 1), F32)
        for r in range(rep):
            out = jnp.where(r_idx == r, rb_ref[n_buckets - 1, g * rep + r], out)
        return out

    for cp in copies(b, c, slot):
        cp.wait()

    is_last = (c == n_chunks - 1).astype(F32)
    for g in range(n_kv):
        qg = q_ref[g * rep:(g + 1) * rep].reshape(rows, hd).astype(_MXU_DTYPE)
        kc = kv_head(kbuf_ref, g)
        far = far_bias(g)
        lg = _nt_dot(qg, kc.astype(_MXU_DTYPE)) + far
        near = lg[:, ck - page:] + (head_bias(g, 0) - far) * is_last
        lg = jnp.concatenate([lg[:, :ck - page], near], axis=1)
        update(g, lg, mask_ref[...], kv_head(vbuf_ref, g))

    @pl.when(c == n_chunks - 1)
    def _():
        zpad = jnp.zeros((page - t_dec, hd), F32)
        for g in range(n_kv):
            qg = q_ref[g * rep:(g + 1) * rep].reshape(rows, hd).astype(_MXU_DTYPE)
            kn = jnp.concatenate([kn_ref[pl.ds(g, t_dec, stride=n_kv), :], zpad], axis=0)
            vn = jnp.concatenate([vn_ref[pl.ds(g, t_dec, stride=n_kv), :], zpad], axis=0)
            lg = _nt_dot(qg, kn.astype(_MXU_DTYPE)) + head_bias(g, 1)
            update(g, lg, mtail_ref[...], vn)
            rs = slice(g * rows, (g + 1) * rows)
            o = acc_ref[rs, :] / l_ref[rs, :] * sg_ref[g * rep:(g + 1) * rep].reshape(rows, hd)
            o_ref[g * rep:(g + 1) * rep] = o.reshape(rep, t_dec, hd)


def _sample_attention(page_table, rel_bias, q_s, mask, kn, vn, sg_s, ts, cache_k, cache_v, *, t_dec, cpages, page):
    bd, n_pages = page_table.shape
    n_heads, _, hd = q_s.shape
    n_kv = cache_k.shape[1] // page
    rep = n_heads // n_kv
    n_chunks = n_pages // cpages
    ck = cpages * page
    past = n_pages * page
    hrow = lambda b, c, pt: (0, b, 0)
    grid_spec = pltpu.PrefetchScalarGridSpec(
        num_scalar_prefetch=1,
        grid=(bd, n_chunks),
        in_specs=[pl.BlockSpec(memory_space=pltpu.SMEM),
                  pl.BlockSpec((n_heads, t_dec, hd), hrow),
                  pl.BlockSpec((t_dec, ck), lambda b, c, pt: (b, c)),
                  pl.BlockSpec((t_dec, page), lambda b, c, pt: (b, past // page)),
                  pl.BlockSpec((t_dec * n_kv, hd), lambda b, c, pt: (b, 0)),
                  pl.BlockSpec((t_dec * n_kv, hd), lambda b, c, pt: (b, 0)),
                  pl.BlockSpec((n_heads, t_dec, hd), hrow),
                  pl.BlockSpec((n_heads, 2, t_dec, _LANES), lambda b, c, pt: (0, 0, 0, 0)),
                  pl.BlockSpec(memory_space=pl.ANY),
                  pl.BlockSpec(memory_space=pl.ANY)],
        out_specs=pl.BlockSpec((n_heads, t_dec, hd), hrow),
        scratch_shapes=[pltpu.VMEM((2, ck * n_kv, hd), F32),
                        pltpu.VMEM((2, ck * n_kv, hd), F32),
                        pltpu.VMEM((n_heads * t_dec, _LANES), F32),
                        pltpu.VMEM((n_heads * t_dec, _LANES), F32),
                        pltpu.VMEM((n_heads * t_dec, hd), F32),
                        pltpu.SemaphoreType.DMA((2, 2))],
    )
    return pl.pallas_call(
        functools.partial(_sample_attn_body, n_chunks=n_chunks, cpages=cpages, page=page, n_kv=n_kv,
                          rep=rep, t_dec=t_dec, n_buckets=rel_bias.shape[0]),
        grid_spec=grid_spec,
        out_shape=jax.ShapeDtypeStruct((n_heads, bd * t_dec, hd), F32),
        compiler_params=_cparams(2),
        name="sample_attention",
    )(page_table, rel_bias.astype(F32), q_s, mask, mask, kn, vn, sg_s, ts, cache_k, cache_v)


def _pool_body(u_ref, halo_ref, gb_ref, wp_ref, ps_ref, o_ref, ext_ref, *, tm, pos_base, own_halo, pg):
    i = pl.program_id(1)
    halo = halo_ref[...].reshape(_POOL_HALO, -1)
    if own_halo:
        halo = jnp.where(i > 0, halo, 0.0)
    ext_ref[0:_POOL_HALO, :] = halo
    ext_ref[_POOL_HALO:, :] = u_ref[...]
    pos = pos_base + i * tm + lax.broadcasted_iota(I32, (tm, 1), 0)
    for g, win in enumerate(_POOL_WINDOWS):
        cols = slice(g * pg, (g + 1) * pg)
        tot = ext_ref[_POOL_HALO:, cols]
        for k in range(1, win):
            tot = tot + ext_ref[_POOL_HALO - k:_POOL_HALO - k + tm, cols]
        cnt = jnp.minimum(pos + 1, win).astype(F32)
        d = tot / cnt - ext_ref[_POOL_HALO:, cols]
        y = jnp.dot(d.astype(_MXU_DTYPE), wp_ref[g], preferred_element_type=F32)
        o_ref[:, cols] = (y * ps_ref[:, cols] * gb_ref[:, cols].astype(F32)).astype(o_ref.dtype)


def _pool(u, halo, gb, wp, ps, *, batch, rows, tm, pos_base, own_halo, out_dtype):
    m, pw = u.shape
    nt = rows // tm
    pg = pw // len(_POOL_WINDOWS)
    if own_halo:
        hb = tm // _POOL_HALO
        halo_spec = pl.BlockSpec((_POOL_HALO, pw), lambda b, i: (jnp.maximum((b * nt + i) * hb - 1, 0), 0))
    else:
        halo_spec = pl.BlockSpec((1, _POOL_HALO, pw), lambda b, i: (b, 0, 0))
    return pl.pallas_call(
        functools.partial(_pool_body, tm=tm, pos_base=pos_base, own_halo=own_halo, pg=pg),
        grid=(batch, nt),
        in_specs=[pl.BlockSpec((tm, pw), lambda b, i: (b * nt + i, 0)),
                  halo_spec,
                  pl.BlockSpec((tm, pw), lambda b, i: (b * nt + i, 0)),
                  pl.BlockSpec(wp.shape, lambda b, i: (0, 0, 0)),
                  pl.BlockSpec((1, pw), lambda b, i: (0, 0))],
        out_specs=pl.BlockSpec((tm, pw), lambda b, i: (b * nt + i, 0)),
        out_shape=jax.ShapeDtypeStruct((m, pw), out_dtype),
        scratch_shapes=[pltpu.VMEM((_POOL_HALO + tm, pw), F32)],
        compiler_params=_cparams(2),
        name="pool",
    )(u, halo, gb, wp, ps)


def _merge_body(ma_ref, mb_ref, w_ref, x_ref, g_ref, o_ref, mix_ref, h_ref, *, n_j, tn):
    j = pl.program_id(1)
    n_heads, _, hd = ma_ref.shape

    @pl.when(j == 0)
    def _():
        for h in range(n_heads):
            mix_ref[:, h * hd:(h + 1) * hd] = ma_ref[h].astype(mix_ref.dtype)
        mix_ref[:, n_heads * hd:] = mb_ref[...].astype(mix_ref.dtype)

    h_ref[j] = x_ref[...] + jnp.dot(mix_ref[...], w_ref[...], preferred_element_type=F32)

    @pl.when(j == n_j - 1)
    def _():
        ssq = jnp.zeros((h_ref.shape[1], 1), F32)
        for jj in range(n_j):
            hv = h_ref[jj]
            ssq = ssq + jnp.sum(hv * hv, axis=-1, keepdims=True)
        inv = lax.rsqrt(ssq / (n_j * tn) + _RMS_EPS)
        for jj in range(n_j):
            o_ref[:, jj * tn:(jj + 1) * tn] = h_ref[jj] * inv * g_ref[:, jj * tn:(jj + 1) * tn]


def _merge(mix_a, mix_b, w_out, x2d, final_g, *, tm, tn):
    n_heads, m, hd = mix_a.shape
    pw = mix_b.shape[1]
    mixw, d = w_out.shape
    n_j = d // tn
    return pl.pallas_call(
        functools.partial(_merge_body, n_j=n_j, tn=tn),
        grid=(m // tm, n_j),
        in_specs=[pl.BlockSpec((n_heads, tm, hd), lambda i, j: (0, i, 0)),
                  pl.BlockSpec((tm, pw), lambda i, j: (i, 0)),
                  pl.BlockSpec((mixw, tn), lambda i, j: (0, j)),
                  pl.BlockSpec((tm, tn), lambda i, j: (i, j)),
                  pl.BlockSpec((1, d), lambda i, j: (0, 0))],
        out_specs=pl.BlockSpec((tm, d), lambda i, j: (i, 0)),
        out_shape=jax.ShapeDtypeStruct((m, d), F32),
        scratch_shapes=[pltpu.VMEM((tm, mixw), _MXU_DTYPE), pltpu.VMEM((n_j, tm, tn), F32)],
        compiler_params=_cparams(2),
        name="merge",
    )(mix_a, mix_b, w_out, x2d, final_g.reshape(1, d).astype(F32))


def kernel(x_prompt, x_sample, cache_k, cache_v, cache_idx_k, state_pool, page_table,
           rel_bias, ln_g, w_in, w_pool, pool_scale, w_out, final_g):
    depth = w_in.shape[0]
    assert depth == 1, "single-layer step only"
    batch, seq, d_model = x_prompt.shape
    bd, t_dec, _ = x_sample.shape
    _, n_phys, page, n_kv, hd = cache_k.shape
    idx_dim = cache_idx_k.shape[-1]
    n_buckets, n_heads = rel_bias.shape
    pw = pool_scale.shape[-1]
    n_pg = w_pool.shape[1]
    attn_w, kv_w = n_heads * hd, n_kv * hd
    in_w = w_in.shape[-1]
    idx_heads = (in_w - 2 * attn_w - 2 * kv_w - idx_dim - 2 * pw) // (idx_dim + 1)
    n_pages = page_table.shape[1]
    past = n_pages * page
    mp, ms = batch * seq, bd * t_dec
    tq = 2 * _LANES
    assert hd == _LANES and idx_dim == _LANES and idx_heads <= _LANES and n_pg == len(_POOL_WINDOWS)
    assert page >= _MAX_DISTANCE and _LANES >= _MAX_DISTANCE and seq % (2 * tq) == 0 and t_dec <= _POOL_HALO - 1
    assert state_pool.shape[2] == _POOL_HALO - 1

    o_q, o_k, o_v = 0, attn_w, attn_w + kv_w
    o_ga = o_v + kv_w
    o_qi = o_ga + attn_w
    o_wi = o_qi + idx_heads * idx_dim
    o_ki = o_wi + idx_heads
    o_u = o_ki + idx_dim
    o_gb = o_u + pw
    wt = jnp.transpose(w_in[0])

    xn_p = _rmsnorm(x_prompt.reshape(mp, d_model), ln_g[0], _MXU_DTYPE)
    xn_s = _rmsnorm(x_sample.reshape(ms, d_model), ln_g[0], _MXU_DTYPE)

    tm = min(1024, mp)
    tn = lambda n: min(512, n)
    rm = lambda n, t, dt, s: _row_major(ms if s else mp, n, tm, t, dt, s)
    hm = lambda n, t, dt, s: _head_major(ms if s else mp, n, tm, t, dt, s)

    def proj(lo, hi, t, **kw):
        assert (hi - lo) % t == 0 and lo % 8 == 0
        return _proj_call(xn_p, xn_s, wt, [(lo, t)], (hi - lo) // t, tm, **kw)

    def scaled_heads(scale):
        def epi(acc, refs):
            _store_heads(refs[0], acc * scale)
        return epi

    def silu_heads(acc, refs):
        _store_heads(refs[0], _silu(acc))

    def silu_rows(acc, refs):
        refs[0][...] = _silu(acc).astype(refs[0].dtype)

    def plain_rows(acc, refs):
        refs[0][...] = acc.astype(refs[0].dtype)

    def kv_rows(acc, ref):
        for g in range(n_kv):
            ref[pl.ds(g, acc.shape[0], stride=n_kv), :] = acc[:, g * hd:(g + 1) * hd]

    def kv_p(acc, refs):
        kv_rows(acc, refs[0])
        _store_heads(refs[1], acc)

    def kv_s(acc, refs):
        kv_rows(acc, refs[0])

    def wk_p(acc, refs):
        refs[0][...] = jnp.concatenate([acc[:, :_LANES] * idx_heads ** -0.5, acc[:, _LANES:]], axis=1)
        refs[1][...] = acc[:, _LANES:].astype(refs[1].dtype)

    def wk_s(acc, refs):
        refs[0][...] = jnp.concatenate([acc[:, :_LANES] * idx_heads ** -0.5, acc[:, _LANES:]], axis=1)

    t = tn(attn_w)
    q_hm, q_s = proj(o_q, o_k, t, outs_p=[hm(attn_w, t, _MXU_DTYPE, False)], outs_s=[hm(attn_w, t, F32, True)],
                     epi_p=scaled_heads(hd ** -0.5), epi_s=scaled_heads(hd ** -0.5), name="proj_q")
    kv_outs = dict(outs_p=[((mp * n_kv, hd), F32, (tm * n_kv, hd), lambda j, i: (i, 0)),
                           ((n_kv, mp, hd), _MXU_DTYPE, (n_kv, tm, hd), lambda j, i: (0, i, 0))],
                   outs_s=[((ms * n_kv, hd), F32, (ms * n_kv, hd), lambda j, i: (0, 0))], epi_p=kv_p, epi_s=kv_s)
    k_p, k_hm, k_s = proj(o_k, o_v, kv_w, name="proj_k", **kv_outs)
    v_p, v_hm, v_s = proj(o_v, o_ga, kv_w, name="proj_v", **kv_outs)
    sga_hm, sga_s = proj(o_ga, o_qi, t, outs_p=[hm(attn_w, t, _MXU_DTYPE, False)],
                         outs_s=[hm(attn_w, t, F32, True)], epi_p=silu_heads, epi_s=silu_heads, name="proj_ga")
    t = tn(idx_heads * idx_dim)
    qi_hm, qi_s = proj(o_qi, o_wi, t, outs_p=[hm(idx_heads * idx_dim, t, _MXU_DTYPE, False)],
                       outs_s=[hm(idx_heads * idx_dim, t, F32, True)],
                       epi_p=scaled_heads(idx_dim ** -0.5), epi_s=scaled_heads(idx_dim ** -0.5), name="proj_qi")
    t = 2 * _LANES
    wk_pr, ki_bf, wk_sm = _proj_call(
        xn_p, xn_s, wt, [(o_wi, _LANES), (o_ki, idx_dim)], 1, tm,
        outs_p=[((mp, t), F32, (tm, t), lambda j, i: (i, 0)), ((mp, idx_dim), _MXU_DTYPE, (tm, idx_dim), lambda j, i: (i, 0))],
        outs_s=[((ms, t), F32, (ms, t), lambda j, i: (0, 0))],
        epi_p=wk_p, epi_s=wk_s, name="proj_wk")
    t = tn(pw)
    u_p, u_s = proj(o_u, o_gb, t, outs_p=[rm(pw, t, F32, False)], outs_s=[rm(pw, t, F32, True)],
                    epi_p=plain_rows, epi_s=plain_rows, name="proj_u")
    sgb_p, sgb_s = proj(o_gb, in_w, t, outs_p=[rm(pw, t, _MXU_DTYPE, False)], outs_s=[rm(pw, t, F32, True)],
                        epi_p=silu_rows, epi_s=silu_rows, name="proj_gb")

    tb_p, tb_s = _bias_tables(rel_bias, _LANES, t_dec, page)

    mixa_p = _prompt_attention(qi_hm, wk_pr, ki_bf, q_hm, k_hm, v_hm, sga_hm, tb_p, batch=batch, seq=seq, tq=tq)

    scores = _sample_scores(page_table, qi_s, wk_sm, cache_idx_k[0], t_dec=t_dec)
    topk_s = min(_TOPK_MAX, (past + t_dec) // 4)
    mask_s = _sample_select(scores, topk=topk_s, past=past).reshape(ms, past + page)
    cpages = max(1, min(32, n_pages // 2))
    assert n_pages % cpages == 0
    mixa_s = _sample_attention(page_table, rel_bias, q_s, mask_s, k_s, v_s, sga_s, tb_s,
                               cache_k[0].reshape(n_phys, page * n_kv, hd), cache_v[0].reshape(n_phys, page * n_kv, hd),
                               t_dec=t_dec, cpages=cpages, page=page)

    wp = w_pool[0].astype(_MXU_DTYPE)
    ps = pool_scale[0].reshape(1, pw).astype(F32)
    mixb_p = _pool(u_p, u_p, sgb_p, wp, ps, batch=batch, rows=seq, tm=min(512, seq), pos_base=0,
                   own_halo=True, out_dtype=_MXU_DTYPE)
    halo_s = jnp.concatenate([jnp.zeros((bd, 1, pw), F32), state_pool[0].astype(F32)], axis=1)
    mixb_s = _pool(u_s, halo_s, sgb_s, wp, ps, batch=bd, rows=t_dec, tm=t_dec, pos_base=past,
                   own_halo=False, out_dtype=F32)

    w_o = w_out[0].astype(_MXU_DTYPE)
    tn_o = min(512, d_model)
    y_p = _merge(mixa_p, mixb_p, w_o, x_prompt.reshape(mp, d_model), final_g, tm=min(512, mp), tn=tn_o)
    y_s = _merge(mixa_s, mixb_s, w_o, x_sample.reshape(ms, d_model), final_g, tm=ms, tn=tn_o)

    n_state = state_pool.shape[2]
    u_full_s = jnp.concatenate([state_pool[0].astype(F32), u_s.reshape(bd, t_dec, pw)], axis=1)
    return (y_p.reshape(batch, seq, d_model), y_s.reshape(bd, t_dec, d_model),
            k_p.reshape(1, batch, seq, n_kv, hd), v_p.reshape(1, batch, seq, n_kv, hd),
            wk_pr[:, _LANES:].reshape(1, batch, seq, idx_dim),
            u_p.reshape(batch, seq, pw)[:, seq - n_state:][None],
            k_s.reshape(1, bd, t_dec, n_kv, hd), v_s.reshape(1, bd, t_dec, n_kv, hd),
            wk_sm[:, _LANES:].reshape(1, bd, t_dec, idx_dim),
            u_full_s[:, t_dec:][None])
```

```python
import functools
import math

import jax
import jax.numpy as jnp
from jax import lax
from jax.experimental import pallas as pl
from jax.experimental.pallas import tpu as pltpu

F32 = jnp.float32
BF16 = jnp.bfloat16
I32 = jnp.int32

_TOPK_MAX = 256
_POOL_WINDOWS = (2, 4, 8, 16)
_POOL_HALO = 16
_MAX_DISTANCE = 128
_RMS_EPS = 1e-6
_NEG = -1e30
_LOG2E = math.log2(math.e)
_INT_MIN = -(2 ** 31)
_LANES = 128
_VMEM_LIMIT = 56 * 1024 * 1024
_MXU_DTYPE = BF16


def _cparams(n_axes):
    return pltpu.CompilerParams(dimension_semantics=("arbitrary",) * n_axes,
                                vmem_limit_bytes=_VMEM_LIMIT)


def _nt_dot(a, b):
    return lax.dot_general(a, b, (((1,), (1,)), ((), ())), preferred_element_type=F32)


def _sort_key(x):
    bits = lax.bitcast_convert_type(x + 0.0, I32)
    return bits ^ ((bits >> 31) & 0x7FFFFFFF)


def _rmsnorm_body(x_ref, g_ref, o_ref):
    x = x_ref[...].astype(F32)
    var = jnp.mean(x * x, axis=-1, keepdims=True)
    o_ref[...] = (x * lax.rsqrt(var + _RMS_EPS) * g_ref[...]).astype(o_ref.dtype)


def _rmsnorm(x2d, g, out_dtype):
    m, d = x2d.shape
    tm = min(256, m)
    return pl.pallas_call(
        _rmsnorm_body,
        grid=(m // tm,),
        in_specs=[pl.BlockSpec((tm, d), lambda i: (i, 0)), pl.BlockSpec((1, d), lambda i: (0, 0))],
        out_specs=pl.BlockSpec((tm, d), lambda i: (i, 0)),
        out_shape=jax.ShapeDtypeStruct((m, d), out_dtype),
        compiler_params=_cparams(1),
        name="rmsnorm",
    )(x2d, g.reshape(1, d).astype(F32))


def _proj_body(xp_ref, xs_ref, *refs, n_w, n_p, n_out, epi_p, epi_s):
    w_refs, out_refs, wc_ref = refs[:n_w], refs[n_w:n_w + n_out], refs[n_w + n_out]
    first_row_tile = pl.program_id(1) == 0

    @pl.when(first_row_tile)
    def _():
        r0 = 0
        for w_ref in w_refs:
            wc_ref[r0:r0 + w_ref.shape[0], :] = w_ref[...].astype(wc_ref.dtype)
            r0 += w_ref.shape[0]

    epi_p(_nt_dot(xp_ref[...], wc_ref[...]), out_refs[:n_p])

    @pl.when(first_row_tile)
    def _():
        epi_s(_nt_dot(xs_ref[...], wc_ref[...]), out_refs[n_p:])


def _proj_call(xn_p, xn_s, wt, w_rows, n_tiles, tm, outs_p, outs_s, epi_p, epi_s, name):
    mp, d = xn_p.shape
    ms = xn_s.shape[0]
    assert mp % tm == 0
    tn = sum(cnt for _, cnt in w_rows)
    outs = list(outs_p) + list(outs_s)

    def w_spec(first, cnt):
        assert first % 8 == 0 and cnt % 8 == 0
        return pl.BlockSpec((pl.Element(cnt), pl.Element(d)), lambda j, i: (pl.multiple_of(first + j * cnt, 8), 0))

    return pl.pallas_call(
        functools.partial(_proj_body, n_w=len(w_rows), n_p=len(outs_p), n_out=len(outs), epi_p=epi_p, epi_s=epi_s),
        grid=(n_tiles, mp // tm),
        in_specs=[pl.BlockSpec((tm, d), lambda j, i: (i, 0)),
                  pl.BlockSpec((ms, d), lambda j, i: (0, 0))] + [w_spec(*r) for r in w_rows],
        out_specs=[pl.BlockSpec(o[2], o[3]) for o in outs],
        out_shape=[jax.ShapeDtypeStruct(o[0], o[1]) for o in outs],
        scratch_shapes=[pltpu.VMEM((tn, d), xn_p.dtype)],
        compiler_params=_cparams(2),
        name=name,
    )(xn_p, xn_s, *([wt] * len(w_rows)))


def _row_major(m, ncols, tm, tn, dtype, sample):
    if sample:
        return ((m, ncols), dtype, (m, tn), lambda j, i: (0, j))
    return ((m, ncols), dtype, (tm, tn), lambda j, i: (i, j))


def _head_major(m, ncols, tm, tn, dtype, sample):
    nh, hb = ncols // _LANES, tn // _LANES
    if sample:
        return ((nh, m, _LANES), dtype, (hb, m, _LANES), lambda j, i: (j, 0, 0))
    return ((nh, m, _LANES), dtype, (hb, tm, _LANES), lambda j, i: (j, i, 0))


def _store_heads(ref, val):
    for hh in range(ref.shape[0]):
        ref[hh] = val[:, hh * _LANES:(hh + 1) * _LANES].astype(ref.dtype)


def _silu(x):
    return x * jax.nn.sigmoid(x)


def _bias_tables_body(rb_ref, tp_ref, ts_ref, *, n_buckets, tq, t_dec, page):
    h = pl.program_id(0)
    max_exact = n_buckets // 2

    def table(dist):
        d = jnp.maximum(dist, 0)
        df = jnp.maximum(d.astype(F32), 1.0)
        large = max_exact + (jnp.log(df / max_exact) / math.log(_MAX_DISTANCE / max_exact)
                             * (n_buckets - max_exact)).astype(I32)
        bucket = jnp.where(d < max_exact, d, jnp.minimum(large, n_buckets - 1))
        out = jnp.zeros(dist.shape, F32)
        for b in range(n_buckets):
            out = jnp.where(bucket == b, rb_ref[b, h], out)
        return out

    rp = lax.broadcasted_iota(I32, (tq, _LANES), 0)
    cp = lax.broadcasted_iota(I32, (tq, _LANES), 1)
    far = rb_ref[n_buckets - 1, h]
    tp_ref[0, 0] = (table(rp - cp) - far) * _LOG2E
    tp_ref[0, 1] = (table(rp - cp + tq) - far) * _LOG2E
    rs = lax.broadcasted_iota(I32, (t_dec, _LANES), 0)
    cs = lax.broadcasted_iota(I32, (t_dec, _LANES), 1)
    ts_ref[0, 0] = table(rs - cs + page)
    ts_ref[0, 1] = table(rs - cs)


def _bias_tables(rel_bias, tq, t_dec, page):
    n_buckets, n_heads = rel_bias.shape
    return pl.pallas_call(
        functools.partial(_bias_tables_body, n_buckets=n_buckets, tq=tq, t_dec=t_dec, page=page),
        grid=(n_heads,),
        in_specs=[pl.BlockSpec(memory_space=pltpu.SMEM)],
        out_specs=[pl.BlockSpec((1, 2, tq, _LANES), lambda h: (h, 0, 0, 0)),
                   pl.BlockSpec((1, 2, t_dec, _LANES), lambda h: (h, 0, 0, 0))],
        out_shape=[jax.ShapeDtypeStruct((n_heads, 2, tq, _LANES), F32),
                   jax.ShapeDtypeStruct((n_heads, 2, t_dec, _LANES), F32)],
        compiler_params=_cparams(1),
        name="bias_tables",
    )(rel_bias.astype(F32))


def _prompt_attn_body(qi_ref, wk_ref, ki_ref, q_ref, k_ref, v_ref, sg_ref, tb_ref, o_ref,
                      wt_ref, sc_ref, keys_ref, madd_ref, m_ref, l_ref, acc_ref,
                      *, tq, topk, idx_heads, n_kv, rep):
    i = pl.program_id(1)
    n_ch = i + 1
    hd = q_ref.shape[-1]
    rows = rep * tq
    nb = tq // _LANES
    hgrp = 2
    kc2 = 2 * tq

    def chunk(c):
        return pl.ds(pl.multiple_of(c * tq, tq), tq)

    def causal(c):
        key_pos = lax.broadcasted_iota(I32, (tq, tq), 0) + c * tq
        return key_pos <= lax.broadcasted_iota(I32, (tq, tq), 1) + i * tq

    wt_ref[...] = wk_ref[...].T

    def score_body(c2, carry):
        ks = pl.ds(pl.multiple_of(c2 * kc2, kc2), kc2)
        keys = ki_ref[ks, :]
        a = jnp.zeros((kc2, tq), F32)
        for hg in range(idx_heads // hgrp):
            qi2 = qi_ref[hg * hgrp:(hg + 1) * hgrp].reshape(hgrp * tq, -1)
            s = _nt_dot(keys, qi2)
            for r in range(hgrp):
                h = hg * hgrp + r
                a = a + jnp.maximum(s[:, r * tq:(r + 1) * tq], 0.0) * wt_ref[h:h + 1, :]
        sc_ref[ks, :] = a
        return carry

    lax.fori_loop(0, (n_ch + 1) // 2, score_body, 0)

    def key_body(c, carry):
        keys_ref[chunk(c), :] = jnp.where(causal(c), _sort_key(sc_ref[chunk(c), :]), _INT_MIN)
        return carry

    lax.fori_loop(0, n_ch, key_body, 0)

    def count(pred):
        def c_body(c, acc):
            hit = pred(keys_ref[chunk(c), :]).astype(F32)
            return acc + jnp.sum(hit.reshape(tq // 8, 8, tq), axis=0)

        return jnp.sum(lax.fori_loop(0, n_ch, c_body, jnp.zeros((8, tq), F32)), axis=0, keepdims=True)

    def radix_step(it, thr):
        cand = thr + jnp.left_shift(jnp.int32(1), 31 - it)
        return jnp.where(count(lambda k: k >= cand) >= topk, cand, thr)

    thr = lax.fori_loop(0, 32, radix_step, jnp.full((1, tq), _INT_MIN, I32))
    need = topk - count(lambda k: k > thr)
    n_eq = count(lambda k: k == thr)
    tied = jnp.max(jnp.where((n_eq > need) & (thr != _INT_MIN), 1.0, 0.0)) > 0.0

    def madd_body(c, carry):
        keep = (keys_ref[chunk(c), :] >= thr) & causal(c)
        madd_ref[c] = jnp.where(keep, 0.0, _NEG).T
        return carry

    lax.fori_loop(0, n_ch, madd_body, 0)

    @pl.when(tied)
    def _():
        tri = (lax.broadcasted_iota(I32, (tq, tq), 1) <= lax.broadcasted_iota(I32, (tq, tq), 0)).astype(BF16)

        def tie_body(c, run):
            kc = keys_ref[chunk(c), :]
            eq = kc == thr
            pre = jnp.dot(tri, eq.astype(F32).astype(BF16), preferred_element_type=F32) + run
            keep = ((kc > thr) | (eq & (pre <= need))) & causal(c)
            madd_ref[c] = jnp.where(keep, 0.0, _NEG).T
            return run + jnp.sum(eq.astype(F32), axis=0, keepdims=True)

        lax.fori_loop(0, n_ch, tie_body, jnp.zeros((1, tq), F32))

    m_ref[...] = jnp.full(m_ref.shape, _NEG, F32)
    l_ref[...] = jnp.zeros(l_ref.shape, F32)
    acc_ref[...] = jnp.zeros(acc_ref.shape, F32)

    def attend(c, near):
        madd = madd_ref[c]
        for g in range(n_kv):
            q4 = q_ref[g * rep:(g + 1) * rep].reshape(rows, hd)
            s = _nt_dot(q4, k_ref[g, chunk(c), :])
            slabs = []
            for r in range(rep):
                for a in range(nb):
                    rs = slice(r * tq + a * _LANES, r * tq + (a + 1) * _LANES)
                    x = s[rs, :] + madd[a * _LANES:(a + 1) * _LANES, :]
                    if near is not None:
                        cols = []
                        for b in range(nb):
                            xb = x[:, b * _LANES:(b + 1) * _LANES]
                            kind = near * nb + a - b
                            if kind in (0, 1):
                                xb = xb + tb_ref[g * rep + r, kind]
                            cols.append(xb)
                        x = jnp.concatenate(cols, axis=1)
                    slabs.append(x)
            x = jnp.concatenate(slabs, axis=0)
            xm = x[:, 0:_LANES]
            for b in range(1, nb):
                xm = jnp.maximum(xm, x[:, b * _LANES:(b + 1) * _LANES])
            m_old = m_ref[g]
            m_new = jnp.maximum(m_old, jnp.max(xm, axis=-1, keepdims=True))
            alpha = jnp.exp2(m_old - m_new)
            p = jnp.exp2(x - jnp.concatenate([m_new] * nb, axis=1))
            ps = p[:, 0:_LANES]
            for b in range(1, nb):
                ps = ps + p[:, b * _LANES:(b + 1) * _LANES]
            l_ref[g] = alpha * l_ref[g] + ps
            acc_ref[g] = alpha * acc_ref[g] + jnp.dot(p.astype(_MXU_DTYPE), v_ref[g, chunk(c), :],
                                                      preferred_element_type=F32)
            m_ref[g] = m_new

    attend(i, 0)

    @pl.when(i >= 1)
    def _():
        attend(i - 1, 1)

    def far_body(c, carry):
        attend(c, None)
        return carry

    lax.fori_loop(0, jnp.maximum(i - 1, 0), far_body, 0)

    for g in range(n_kv):
        o = acc_ref[g] / jnp.sum(l_ref[g], axis=-1, keepdims=True)
        o = o * sg_ref[g * rep:(g + 1) * rep].reshape(rows, hd).astype(F32)
        o_ref[g * rep:(g + 1) * rep] = o.reshape(rep, tq, hd).astype(o_ref.dtype)


def _prompt_attention(qi_hm, wk, ki_bf, q_hm, k_hm, v_hm, sg_hm, tb_p, *, batch, seq, tq):
    idx_heads, m, idx_dim = qi_hm.shape
    n_heads, _, hd = q_hm.shape
    n_kv = k_hm.shape[0]
    rep = n_heads // n_kv
    n_q = seq // tq
    topk = min(_TOPK_MAX, seq // 4)
    qrow = lambda b, i: (0, b * n_q + i, 0)
    body = functools.partial(_prompt_attn_body, tq=tq, topk=topk, idx_heads=idx_heads, n_kv=n_kv, rep=rep)
    return pl.pallas_call(
        body,
        grid=(batch, n_q),
        in_specs=[pl.BlockSpec((idx_heads, tq, idx_dim), qrow),
                  pl.BlockSpec((tq, _LANES), lambda b, i: (b * n_q + i, 0)),
                  pl.BlockSpec((seq, idx_dim), lambda b, i: (b, 0)),
                  pl.BlockSpec((n_heads, tq, hd), qrow),
                  pl.BlockSpec((n_kv, seq, hd), lambda b, i: (0, b, 0)),
                  pl.BlockSpec((n_kv, seq, hd), lambda b, i: (0, b, 0)),
                  pl.BlockSpec((n_heads, tq, hd), qrow),
                  pl.BlockSpec((n_heads, 2, _LANES, _LANES), lambda b, i: (0, 0, 0, 0))],
        out_specs=pl.BlockSpec((n_heads, tq, hd), qrow),
        out_shape=jax.ShapeDtypeStruct((n_heads, m, hd), _MXU_DTYPE),
        scratch_shapes=[pltpu.VMEM((_LANES, tq), F32),
                        pltpu.VMEM((seq, tq), F32),
                        pltpu.VMEM((seq, tq), I32),
                        pltpu.VMEM((n_q, tq, tq), F32),
                        pltpu.VMEM((n_kv, rep * tq, _LANES), F32),
                        pltpu.VMEM((n_kv, rep * tq, _LANES), F32),
                        pltpu.VMEM((n_kv, rep * tq, hd), F32)],
        compiler_params=_cparams(2),
        name="prompt_attention",
    )(qi_hm, wk, ki_bf, q_hm, k_hm, v_hm, sg_hm, tb_p)


def _page_copies(pt_ref, cache_ref, buf_ref, sem_ref, b, first_page, n_copy, slot, page):
    return [pltpu.make_async_copy(cache_ref.at[pt_ref[b, first_page + p]],
                                  buf_ref.at[slot, pl.ds(p * page, page)],
                                  sem_ref.at[slot]) for p in range(n_copy)]


def _sample_scores_body(pt_ref, qi_ref, wk_ref, cache_ref, o_ref, buf_ref, wb_ref, sem_ref,
                        *, n_pages, page, idx_heads, t_dec):
    b = pl.program_id(0)
    nb = pl.num_programs(0)
    slot = b % 2
    copies = functools.partial(_page_copies, pt_ref, cache_ref, buf_ref, sem_ref,
                               first_page=0, n_copy=n_pages, page=page)

    @pl.when(b == 0)
    def _():
        for c in copies(b=0, slot=0):
            c.start()

    @pl.when(b + 1 < nb)
    def _():
        for c in copies(b=b + 1, slot=1 - slot):
            c.start()

    qi = qi_ref[...].reshape(idx_heads * t_dec, -1).astype(_MXU_DTYPE)
    w_blk = wk_ref[:, 0:_LANES]
    for h in range(idx_heads):
        wb_ref[h * t_dec:(h + 1) * t_dec, :] = jnp.broadcast_to(w_blk[:, h:h + 1], (t_dec, _LANES))

    def page_scores(keys):
        s = _nt_dot(qi, keys.astype(_MXU_DTYPE))
        return jnp.sum((jnp.maximum(s, 0.0) * wb_ref[...]).reshape(idx_heads, t_dec, page), axis=0)

    past = n_pages * page
    ki_new = jnp.concatenate([wk_ref[:, _LANES:], jnp.zeros((page - t_dec, wk_ref.shape[1] - _LANES), F32)], axis=0)
    row = lax.broadcasted_iota(I32, (t_dec, page), 0)
    col = lax.broadcasted_iota(I32, (t_dec, page), 1)
    o_ref[0, :, past:past + page] = jnp.where(col <= row, page_scores(ki_new), -jnp.inf)

    for c in copies(b=b, slot=slot):
        c.wait()
    for p in range(n_pages):
        o_ref[0, :, p * page:(p + 1) * page] = page_scores(buf_ref[slot, p * page:(p + 1) * page, :])


def _sample_scores(page_table, qi_s, wk_s, cache_idx, *, t_dec):
    bd, n_pages = page_table.shape
    _, page, idx_dim = cache_idx.shape
    idx_heads = qi_s.shape[0]
    past = n_pages * page
    grid_spec = pltpu.PrefetchScalarGridSpec(
        num_scalar_prefetch=1,
        grid=(bd,),
        in_specs=[pl.BlockSpec((idx_heads, t_dec, idx_dim), lambda b, pt: (0, b, 0)),
                  pl.BlockSpec((t_dec, wk_s.shape[1]), lambda b, pt: (b, 0)),
                  pl.BlockSpec(memory_space=pl.ANY)],
        out_specs=pl.BlockSpec((1, t_dec, past + page), lambda b, pt: (b, 0, 0)),
        scratch_shapes=[pltpu.VMEM((2, past, idx_dim), F32),
                        pltpu.VMEM((idx_heads * t_dec, _LANES), F32),
                        pltpu.SemaphoreType.DMA((2,))],
    )
    return pl.pallas_call(
        functools.partial(_sample_scores_body, n_pages=n_pages, page=page, idx_heads=idx_heads, t_dec=t_dec),
        grid_spec=grid_spec,
        out_shape=jax.ShapeDtypeStruct((bd, t_dec, past + page), F32),
        compiler_params=_cparams(1),
        name="sample_scores",
    )(page_table, qi_s, wk_s, cache_idx)


def _sample_select_body(sc_ref, o_ref, keys_ref, *, topk, past):
    bd, t_dec, n_cols = sc_ref.shape

    def valid(c0, width):
        col = lax.broadcasted_iota(I32, (bd, t_dec, width), 2) + c0
        return col <= past + lax.broadcasted_iota(I32, (bd, t_dec, width), 1)

    keys_ref[...] = jnp.where(valid(0, n_cols), _sort_key(sc_ref[...]), _INT_MIN)

    def step(it, thr):
        cand = thr + jnp.left_shift(jnp.int32(1), 31 - it)
        cnt = jnp.sum((keys_ref[...] >= cand).astype(F32), axis=-1, keepdims=True)
        return jnp.where(cnt >= topk, cand, thr)

    thr = lax.fori_loop(0, 32, step, jnp.full((bd, t_dec, 1), _INT_MIN, I32))
    keys = keys_ref[...]
    cnt_gt = jnp.sum((keys > thr).astype(F32), axis=-1, keepdims=True)
    cnt_eq = jnp.sum((keys == thr).astype(F32), axis=-1, keepdims=True)
    need = topk - cnt_gt
    o_ref[...] = ((keys >= thr) & valid(0, n_cols)).astype(F32)
    tied = jnp.max(jnp.where((cnt_eq > need) & (thr != _INT_MIN), 1.0, 0.0)) > 0.0

    @pl.when(tied)
    def _():
        tri = (lax.broadcasted_iota(I32, (_LANES, _LANES), 0)
               <= lax.broadcasted_iota(I32, (_LANES, _LANES), 1)).astype(BF16)
        run = jnp.zeros((bd, t_dec, 1), F32)
        for c0 in range(0, n_cols, _LANES):
            kc = keys_ref[:, :, c0:c0 + _LANES]
            eq = kc == thr
            eq2 = eq.astype(F32).reshape(bd * t_dec, _LANES).astype(BF16)
            pre = jnp.dot(eq2, tri, preferred_element_type=F32).reshape(bd, t_dec, _LANES) + run
            run = run + jnp.sum(eq.astype(F32), axis=-1, keepdims=True)
            keep = (kc > thr) | (eq & (pre <= need))
            o_ref[:, :, c0:c0 + _LANES] = (keep & valid(c0, _LANES)).astype(F32)


def _sample_select(scores, *, topk, past):
    return pl.pallas_call(
        functools.partial(_sample_select_body, topk=topk, past=past),
        out_shape=jax.ShapeDtypeStruct(scores.shape, F32),
        scratch_shapes=[pltpu.VMEM(scores.shape, I32)],
        compiler_params=pltpu.CompilerParams(vmem_limit_bytes=_VMEM_LIMIT),
        name="sample_select",
    )(scores)


def _sample_attn_body(pt_ref, rb_ref, q_ref, mask_ref, mtail_ref, kn_ref, vn_ref, sg_ref, ts_ref,
                      ck_ref, cv_ref, o_ref, kbuf_ref, vbuf_ref, m_ref, l_ref, acc_ref, ksem_ref, vsem_ref,
                      *, n_chunks, cpages, page, n_kv, rep, t_dec, n_buckets):
    b = pl.program_id(0)
    c = pl.program_id(1)
    step = b * n_chunks + c
    n_steps = pl.num_programs(0) * n_chunks
    slot = step % 2
    hd = q_ref.shape[-1]
    rows = rep * t_dec
    ck = cpages * page

    prow = page * n_kv

    def copies(bb, cc, sl):
        return (_page_copies(pt_ref, ck_ref, kbuf_ref, ksem_ref, bb, cc * cpages, cpages, sl, prow)
                + _page_copies(pt_ref, cv_ref, vbuf_ref, vsem_ref, bb, cc * cpages, cpages, sl, prow))

    def kv_head(buf_ref, g):
        return buf_ref[slot, pl.ds(g, ck, stride=n_kv), :]

    @pl.when(step == 0)
    def _():
        for cp in copies(0, 0, 0):
            cp.start()

    @pl.when(step + 1 < n_steps)
    def _():
        nxt = step + 1
        for cp in copies(nxt // n_chunks, nxt % n_chunks, 1 - slot):
            cp.start()

    @pl.when(c == 0)
    def _():
        m_ref[...] = jnp.full(m_ref.shape, -jnp.inf, F32)
        l_ref[...] = jnp.zeros(l_ref.shape, F32)
        acc_ref[...] = jnp.zeros(acc_ref.shape, F32)

    def update(g, logits, mask, values):
        n = logits.shape[-1]
        x = jnp.where(jnp.broadcast_to(mask[None], (rep, t_dec, n)).reshape(rows, n) > 0.5, logits, -jnp.inf)
        rs = slice(g * rows, (g + 1) * rows)
        m_old = m_ref[rs, :]
        m_new = jnp.maximum(m_old, jnp.max(x, axis=-1, keepdims=True))
        m_safe = jnp.where(m_new == -jnp.inf, 0.0, m_new)
        alpha = jnp.exp(m_old - m_safe)
        p = jnp.exp(x - m_safe[:, 0:1])
        l_ref[rs, :] = alpha * l_ref[rs, :] + jnp.sum(p, axis=-1, keepdims=True)
        acc_ref[rs, :] = alpha * acc_ref[rs, :] + jnp.dot(p.astype(_MXU_DTYPE), values.astype(_MXU_DTYPE),
                                                          preferred_element_type=F32)
        m_ref[rs, :] = m_new

    def head_bias(g, kind):
        return ts_ref[g * rep:(g + 1) * rep, kind].reshape(rows, _LANES)

    def far_bias(g):
        r_idx = lax.broadcasted_iota(I32, (rows, 1), 0) // t_dec
        out = jnp.zeros((rows, 1), F32)
        for r in range(rep):
            out = jnp.where(r_idx == r, rb_ref[n_buckets - 1, g * rep + r], out)
        return out

    for cp in copies(b, c, slot):
        cp.wait()

    is_last = (c == n_chunks - 1).astype(F32)
    for g in range(n_kv):
        qg = q_ref[g * rep:(g + 1) * rep].reshape(rows, hd).astype(_MXU_DTYPE)
        kc = kv_head(kbuf_ref, g)
        far = far_bias(g)
        lg = _nt_dot(qg, kc.astype(_MXU_DTYPE)) + far
        near = lg[:, ck - page:] + (head_bias(g, 0) - far) * is_last
        lg = jnp.concatenate([lg[:, :ck - page], near], axis=1)
        update(g, lg, mask_ref[...], kv_head(vbuf_ref, g))

    @pl.when(c == n_chunks - 1)
    def _():
        zpad = jnp.zeros((page - t_dec, hd), F32)
        for g in range(n_kv):
            qg = q_ref[g * rep:(g + 1) * rep].reshape(rows, hd).astype(_MXU_DTYPE)
            kn = jnp.concatenate([kn_ref[pl.ds(g, t_dec, stride=n_kv), :], zpad], axis=0)
            vn = jnp.concatenate([vn_ref[pl.ds(g, t_dec, stride=n_kv), :], zpad], axis=0)
            lg = _nt_dot(qg, kn.astype(_MXU_DTYPE)) + head_bias(g, 1)
            update(g, lg, mtail_ref[...], vn)
            rs = slice(g * rows, (g + 1) * rows)
            o = acc_ref[rs, :] / l_ref[rs, :] * sg_ref[g * rep:(g + 1) * rep].reshape(rows, hd)
            o_ref[g * rep:(g + 1) * rep] = o.reshape(rep, t_dec, hd)


def _sample_attention(page_table, rel_bias, q_s, mask, kn, vn, sg_s, ts, cache_k, cache_v, *, t_dec, cpages, page):
    bd, n_pages = page_table.shape
    n_heads, _, hd = q_s.shape
    n_kv = cache_k.shape[1] // page
    rep = n_heads // n_kv
    n_chunks = n_pages // cpages
    ck = cpages * page
    past = n_pages * page
    hrow = lambda b, c, pt: (0, b, 0)
    grid_spec = pltpu.PrefetchScalarGridSpec(
        num_scalar_prefetch=1,
        grid=(bd, n_chunks),
        in_specs=[pl.BlockSpec(memory_space=pltpu.SMEM),
                  pl.BlockSpec((n_heads, t_dec, hd), hrow),
                  pl.BlockSpec((t_dec, ck), lambda b, c, pt: (b, c)),
                  pl.BlockSpec((t_dec, page), lambda b, c, pt: (b, past // page)),
                  pl.BlockSpec((t_dec * n_kv, hd), lambda b, c, pt: (b, 0)),
                  pl.BlockSpec((t_dec * n_kv, hd), lambda b, c, pt: (b, 0)),
                  pl.BlockSpec((n_heads, t_dec, hd), hrow),
                  pl.BlockSpec((n_heads, 2, t_dec, _LANES), lambda b, c, pt: (0, 0, 0, 0)),
                  pl.BlockSpec(memory_space=pl.ANY),
                  pl.BlockSpec(memory_space=pl.ANY)],
        out_specs=pl.BlockSpec((n_heads, t_dec, hd), hrow),
        scratch_shapes=[pltpu.VMEM((2, ck * n_kv, hd), F32),
                        pltpu.VMEM((2, ck * n_kv, hd), F32),
                        pltpu.VMEM((n_heads * t_dec, _LANES), F32),
                        pltpu.VMEM((n_heads * t_dec, _LANES), F32),
                        pltpu.VMEM((n_heads * t_dec, hd), F32),
                        pltpu.SemaphoreType.DMA((2,)),
                        pltpu.SemaphoreType.DMA((2,))],
    )
    return pl.pallas_call(
        functools.partial(_sample_attn_body, n_chunks=n_chunks, cpages=cpages, page=page, n_kv=n_kv,
                          rep=rep, t_dec=t_dec, n_buckets=rel_bias.shape[0]),
        grid_spec=grid_spec,
        out_shape=jax.ShapeDtypeStruct((n_heads, bd * t_dec, hd), F32),
        compiler_params=_cparams(2),
        name="sample_attention",
    )(page_table, rel_bias.astype(F32), q_s, mask, mask, kn, vn, sg_s, ts, cache_k, cache_v)


def _pool_body(u_ref, halo_ref, gb_ref, wp_ref, ps_ref, o_ref, ext_ref, *, tm, pos_base, own_halo, pg):
    i = pl.program_id(1)
    halo = halo_ref[...].reshape(_POOL_HALO, -1)
    if own_halo:
        halo = jnp.where(i > 0, halo, 0.0)
    ext_ref[0:_POOL_HALO, :] = halo
    ext_ref[_POOL_HALO:, :] = u_ref[...]
    pos = pos_base + i * tm + lax.broadcasted_iota(I32, (tm, 1), 0)
    for g, win in enumerate(_POOL_WINDOWS):
        cols = slice(g * pg, (g + 1) * pg)
        tot = ext_ref[_POOL_HALO:, cols]
        for k in range(1, win):
            tot = tot + ext_ref[_POOL_HALO - k:_POOL_HALO - k + tm, cols]
        cnt = jnp.minimum(pos + 1, win).astype(F32)
        d = tot / cnt - ext_ref[_POOL_HALO:, cols]
        y = jnp.dot(d.astype(_MXU_DTYPE), wp_ref[g], preferred_element_type=F32)
        o_ref[:, cols] = (y * ps_ref[:, cols] * gb_ref[:, cols].astype(F32)).astype(o_ref.dtype)


def _pool(u, halo, gb, wp, ps, *, batch, rows, tm, pos_base, own_halo, out_dtype):
    m, pw = u.shape
    nt = rows // tm
    pg = pw // len(_POOL_WINDOWS)
    if own_halo:
        hb = tm // _POOL_HALO
        halo_spec = pl.BlockSpec((_POOL_HALO, pw), lambda b, i: (jnp.maximum((b * nt + i) * hb - 1, 0), 0))
    else:
        halo_spec = pl.BlockSpec((1, _POOL_HALO, pw), lambda b, i: (b, 0, 0))
    return pl.pallas_call(
        functools.partial(_pool_body, tm=tm, pos_base=pos_base, own_halo=own_halo, pg=pg),
        grid=(batch, nt),
        in_specs=[pl.BlockSpec((tm, pw), lambda b, i: (b * nt + i, 0)),
                  halo_spec,
                  pl.BlockSpec((tm, pw), lambda b, i: (b * nt + i, 0)),
                  pl.BlockSpec(wp.shape, lambda b, i: (0, 0, 0)),
                  pl.BlockSpec((1, pw), lambda b, i: (0, 0))],
        out_specs=pl.BlockSpec((tm, pw), lambda b, i: (b * nt + i, 0)),
        out_shape=jax.ShapeDtypeStruct((m, pw), out_dtype),
        scratch_shapes=[pltpu.VMEM((_POOL_HALO + tm, pw), F32)],
        compiler_params=_cparams(2),
        name="pool",
    )(u, halo, gb, wp, ps)


def _merge_body(ma_ref, mb_ref, w_ref, x_ref, g_ref, o_ref, mix_ref, h_ref, *, n_j, tn):
    j = pl.program_id(1)
    n_heads, _, hd = ma_ref.shape

    @pl.when(j == 0)
    def _():
        for h in range(n_heads):
            mix_ref[:, h * hd:(h + 1) * hd] = ma_ref[h].astype(mix_ref.dtype)
        mix_ref[:, n_heads * hd:] = mb_ref[...].astype(mix_ref.dtype)

    h_ref[j] = x_ref[...] + jnp.dot(mix_ref[...], w_ref[...], preferred_element_type=F32)

    @pl.when(j == n_j - 1)
    def _():
        ssq = jnp.zeros((h_ref.shape[1], 1), F32)
        for jj in range(n_j):
            hv = h_ref[jj]
            ssq = ssq + jnp.sum(hv * hv, axis=-1, keepdims=True)
        inv = lax.rsqrt(ssq / (n_j * tn) + _RMS_EPS)
        for jj in range(n_j):
            o_ref[:, jj * tn:(jj + 1) * tn] = h_ref[jj] * inv * g_ref[:, jj * tn:(jj + 1) * tn]


def _merge(mix_a, mix_b, w_out, x2d, final_g, *, tm, tn):
    n_heads, m, hd = mix_a.shape
    pw = mix_b.shape[1]
    mixw, d = w_out.shape
    n_j = d // tn
    return pl.pallas_call(
        functools.partial(_merge_body, n_j=n_j, tn=tn),
        grid=(m // tm, n_j),
        in_specs=[pl.BlockSpec((n_heads, tm, hd), lambda i, j: (0, i, 0)),
                  pl.BlockSpec((tm, pw), lambda i, j: (i, 0)),
                  pl.BlockSpec((mixw, tn), lambda i, j: (0, j)),
                  pl.BlockSpec((tm, tn), lambda i, j: (i, j)),
                  pl.BlockSpec((1, d), lambda i, j: (0, 0))],
        out_specs=pl.BlockSpec((tm, d), lambda i, j: (i, 0)),
        out_shape=jax.ShapeDtypeStruct((m, d), F32),
        scratch_shapes=[pltpu.VMEM((tm, mixw), _MXU_DTYPE), pltpu.VMEM((n_j, tm, tn), F32)],
        compiler_params=_cparams(2),
        name="merge",
    )(mix_a, mix_b, w_out, x2d, final_g.reshape(1, d).astype(F32))


def kernel(x_prompt, x_sample, cache_k, cache_v, cache_idx_k, state_pool, page_table,
           rel_bias, ln_g, w_in, w_pool, pool_scale, w_out, final_g):
    depth = w_in.shape[0]
    assert depth == 1, "single-layer step only"
    batch, seq, d_model = x_prompt.shape
    bd, t_dec, _ = x_sample.shape
    _, n_phys, page, n_kv, hd = cache_k.shape
    idx_dim = cache_idx_k.shape[-1]
    n_buckets, n_heads = rel_bias.shape
    pw = pool_scale.shape[-1]
    n_pg = w_pool.shape[1]
    attn_w, kv_w = n_heads * hd, n_kv * hd
    in_w = w_in.shape[-1]
    idx_heads = (in_w - 2 * attn_w - 2 * kv_w - idx_dim - 2 * pw) // (idx_dim + 1)
    n_pages = page_table.shape[1]
    past = n_pages * page
    mp, ms = batch * seq, bd * t_dec
    tq = 2 * _LANES
    assert hd == _LANES and idx_dim == _LANES and idx_heads <= _LANES and n_pg == len(_POOL_WINDOWS)
    assert page >= _MAX_DISTANCE and _LANES >= _MAX_DISTANCE and seq % (2 * tq) == 0 and t_dec <= _POOL_HALO - 1
    assert state_pool.shape[2] == _POOL_HALO - 1

    o_q, o_k, o_v = 0, attn_w, attn_w + kv_w
    o_ga = o_v + kv_w
    o_qi = o_ga + attn_w
    o_wi = o_qi + idx_heads * idx_dim
    o_ki = o_wi + idx_heads
    o_u = o_ki + idx_dim
    o_gb = o_u + pw
    wt = jnp.transpose(w_in[0])

    xn_p = _rmsnorm(x_prompt.reshape(mp, d_model), ln_g[0], _MXU_DTYPE)
    xn_s = _rmsnorm(x_sample.reshape(ms, d_model), ln_g[0], _MXU_DTYPE)

    tm = min(1024, mp)
    tn = lambda n: min(512, n)
    rm = lambda n, t, dt, s: _row_major(ms if s else mp, n, tm, t, dt, s)
    hm = lambda n, t, dt, s: _head_major(ms if s else mp, n, tm, t, dt, s)

    def proj(lo, hi, t, **kw):
        assert (hi - lo) % t == 0 and lo % 8 == 0
        return _proj_call(xn_p, xn_s, wt, [(lo, t)], (hi - lo) // t, tm, **kw)

    def scaled_heads(scale):
        def epi(acc, refs):
            _store_heads(refs[0], acc * scale)
        return epi

    def silu_heads(acc, refs):
        _store_heads(refs[0], _silu(acc))

    def silu_rows(acc, refs):
        refs[0][...] = _silu(acc).astype(refs[0].dtype)

    def plain_rows(acc, refs):
        refs[0][...] = acc.astype(refs[0].dtype)

    def kv_rows(acc, ref):
        for g in range(n_kv):
            ref[pl.ds(g, acc.shape[0], stride=n_kv), :] = acc[:, g * hd:(g + 1) * hd]

    def kv_p(acc, refs):
        kv_rows(acc, refs[0])
        _store_heads(refs[1], acc)

    def kv_s(acc, refs):
        kv_rows(acc, refs[0])

    def wk_p(acc, refs):
        refs[0][...] = jnp.concatenate([acc[:, :_LANES] * idx_heads ** -0.5, acc[:, _LANES:]], axis=1)
        refs[1][...] = acc[:, _LANES:].astype(refs[1].dtype)

    def wk_s(acc, refs):
        refs[0][...] = jnp.concatenate([acc[:, :_LANES] * idx_heads ** -0.5, acc[:, _LANES:]], axis=1)

    t = tn(attn_w)
    q_hm, q_s = proj(o_q, o_k, t, outs_p=[hm(attn_w, t, _MXU_DTYPE, False)], outs_s=[hm(attn_w, t, F32, True)],
                     epi_p=scaled_heads(hd ** -0.5 * _LOG2E), epi_s=scaled_heads(hd ** -0.5), name="proj_q")
    kv_outs = dict(outs_p=[((mp * n_kv, hd), F32, (tm * n_kv, hd), lambda j, i: (i, 0)),
                           ((n_kv, mp, hd), _MXU_DTYPE, (n_kv, tm, hd), lambda j, i: (0, i, 0))],
                   outs_s=[((ms * n_kv, hd), F32, (ms * n_kv, hd), lambda j, i: (0, 0))], epi_p=kv_p, epi_s=kv_s)
    k_p, k_hm, k_s = proj(o_k, o_v, kv_w, name="proj_k", **kv_outs)
    v_p, v_hm, v_s = proj(o_v, o_ga, kv_w, name="proj_v", **kv_outs)
    sga_hm, sga_s = proj(o_ga, o_qi, t, outs_p=[hm(attn_w, t, _MXU_DTYPE, False)],
                         outs_s=[hm(attn_w, t, F32, True)], epi_p=silu_heads, epi_s=silu_heads, name="proj_ga")
    t = tn(idx_heads * idx_dim)
    qi_hm, qi_s = proj(o_qi, o_wi, t, outs_p=[hm(idx_heads * idx_dim, t, _MXU_DTYPE, False)],
                       outs_s=[hm(idx_heads * idx_dim, t, F32, True)],
                       epi_p=scaled_heads(idx_dim ** -0.5), epi_s=scaled_heads(idx_dim ** -0.5), name="proj_qi")
    t = 2 * _LANES
    wk_pr, ki_bf, wk_sm = _proj_call(
        xn_p, xn_s, wt, [(o_wi, _LANES), (o_ki, idx_dim)], 1, tm,
        outs_p=[((mp, t), F32, (tm, t), lambda j, i: (i, 0)), ((mp, idx_dim), _MXU_DTYPE, (tm, idx_dim), lambda j, i: (i, 0))],
        outs_s=[((ms, t), F32, (ms, t), lambda j, i: (0, 0))],
        epi_p=wk_p, epi_s=wk_s, name="proj_wk")
    t = tn(pw)
    u_p, u_s = proj(o_u, o_gb, t, outs_p=[rm(pw, t, F32, False)], outs_s=[rm(pw, t, F32, True)],
                    epi_p=plain_rows, epi_s=plain_rows, name="proj_u")
    sgb_p, sgb_s = proj(o_gb, in_w, t, outs_p=[rm(pw, t, _MXU_DTYPE, False)], outs_s=[rm(pw, t, F32, True)],
                        epi_p=silu_rows, epi_s=silu_rows, name="proj_gb")

    tb_p, tb_s = _bias_tables(rel_bias, _LANES, t_dec, page)

    mixa_p = _prompt_attention(qi_hm, wk_pr, ki_bf, q_hm, k_hm, v_hm, sga_hm, tb_p, batch=batch, seq=seq, tq=tq)

    scores = _sample_scores(page_table, qi_s, wk_sm, cache_idx_k[0], t_dec=t_dec)
    topk_s = min(_TOPK_MAX, (past + t_dec) // 4)
    mask_s = _sample_select(scores, topk=topk_s, past=past).reshape(ms, past + page)
    cpages = max(1, min(32, n_pages // 2))
    assert n_pages % cpages == 0
    mixa_s = _sample_attention(page_table, rel_bias, q_s, mask_s, k_s, v_s, sga_s, tb_s,
                               cache_k[0].reshape(n_phys, page * n_kv, hd), cache_v[0].reshape(n_phys, page * n_kv, hd),
                               t_dec=t_dec, cpages=cpages, page=page)

    wp = w_pool[0].astype(_MXU_DTYPE)
    ps = pool_scale[0].reshape(1, pw).astype(F32)
    mixb_p = _pool(u_p, u_p, sgb_p, wp, ps, batch=batch, rows=seq, tm=min(512, seq), pos_base=0,
                   own_halo=True, out_dtype=_MXU_DTYPE)
    halo_s = jnp.concatenate([jnp.zeros((bd, 1, pw), F32), state_pool[0].astype(F32)], axis=1)
    mixb_s = _pool(u_s, halo_s, sgb_s, wp, ps, batch=bd, rows=t_dec, tm=t_dec, pos_base=past,
                   own_halo=False, out_dtype=F32)

    w_o = w_out[0].astype(_MXU_DTYPE)
    tn_o = min(512, d_model)
    y_p = _merge(mixa_p, mixb_p, w_o, x_prompt.reshape(mp, d_model), final_g, tm=min(512, mp), tn=tn_o)
    y_s = _merge(mixa_s, mixb_s, w_o, x_sample.reshape(ms, d_model), final_g, tm=ms, tn=tn_o)

    n_state = state_pool.shape[2]
    u_full_s = jnp.concatenate([state_pool[0].astype(F32), u_s.reshape(bd, t_dec, pw)], axis=1)
    return (y_p.reshape(batch, seq, d_model), y_s.reshape(bd, t_dec, d_model),
            k_p.reshape(1, batch, seq, n_kv, hd), v_p.reshape(1, batch, seq, n_kv, hd),
            wk_pr[:, _LANES:].reshape(1, batch, seq, idx_dim),
            u_p.reshape(batch, seq, pw)[:, seq - n_state:][None],
            k_s.reshape(1, bd, t_dec, n_kv, hd), v_s.reshape(1, bd, t_dec, n_kv, hd),
            wk_sm[:, _LANES:].reshape(1, bd, t_dec, idx_dim),
            u_full_s[:, t_dec:][None])
```

```python
import functools
import math

import jax
import jax.numpy as jnp
from jax import lax
from jax.experimental import pallas as pl
from jax.experimental.pallas import tpu as pltpu

F32 = jnp.float32
BF16 = jnp.bfloat16
I32 = jnp.int32

_TOPK_MAX = 256
_POOL_WINDOWS = (2, 4, 8, 16)
_POOL_HALO = 16
_MAX_DISTANCE = 128
_RMS_EPS = 1e-6
_NEG = -1e30
_LOG2E = math.log2(math.e)
_INT_MIN = -(2 ** 31)
_LANES = 128
_VMEM_LIMIT = 56 * 1024 * 1024
_MXU_DTYPE = BF16


def _cparams(n_axes):
    return pltpu.CompilerParams(dimension_semantics=("arbitrary",) * n_axes,
                                vmem_limit_bytes=_VMEM_LIMIT)


def _nt_dot(a, b):
    return lax.dot_general(a, b, (((1,), (1,)), ((), ())), preferred_element_type=F32)


def _sort_key(x):
    bits = lax.bitcast_convert_type(x + 0.0, I32)
    return bits ^ ((bits >> 31) & 0x7FFFFFFF)


def _rmsnorm_body(x_ref, g_ref, o_ref):
    x = x_ref[...].astype(F32)
    var = jnp.mean(x * x, axis=-1, keepdims=True)
    o_ref[...] = (x * lax.rsqrt(var + _RMS_EPS) * g_ref[...]).astype(o_ref.dtype)


def _rmsnorm(x2d, g, out_dtype):
    m, d = x2d.shape
    tm = min(512, m)
    return pl.pallas_call(
        _rmsnorm_body,
        grid=(m // tm,),
        in_specs=[pl.BlockSpec((tm, d), lambda i: (i, 0)), pl.BlockSpec((1, d), lambda i: (0, 0))],
        out_specs=pl.BlockSpec((tm, d), lambda i: (i, 0)),
        out_shape=jax.ShapeDtypeStruct((m, d), out_dtype),
        compiler_params=_cparams(1),
        name="rmsnorm",
    )(x2d, g.reshape(1, d).astype(F32))


def _proj_body(xp_ref, xs_ref, *refs, n_w, n_p, n_out, epi_p, epi_s):
    w_refs, out_refs, wc_ref = refs[:n_w], refs[n_w:n_w + n_out], refs[n_w + n_out]
    first_row_tile = pl.program_id(1) == 0

    @pl.when(first_row_tile)
    def _():
        r0 = 0
        for w_ref in w_refs:
            wc_ref[r0:r0 + w_ref.shape[0], :] = w_ref[...].astype(wc_ref.dtype)
            r0 += w_ref.shape[0]

    epi_p(_nt_dot(xp_ref[...], wc_ref[...]), out_refs[:n_p])

    @pl.when(first_row_tile)
    def _():
        epi_s(_nt_dot(xs_ref[...], wc_ref[...]), out_refs[n_p:])


def _proj_call(xn_p, xn_s, wt, w_rows, n_tiles, tm, outs_p, outs_s, epi_p, epi_s, name):
    mp, d = xn_p.shape
    ms = xn_s.shape[0]
    assert mp % tm == 0
    tn = sum(cnt for _, cnt in w_rows)
    outs = list(outs_p) + list(outs_s)

    def w_spec(first, cnt):
        assert first % 8 == 0 and cnt % 8 == 0
        return pl.BlockSpec((pl.Element(cnt), pl.Element(d)), lambda j, i: (pl.multiple_of(first + j * cnt, 8), 0))

    return pl.pallas_call(
        functools.partial(_proj_body, n_w=len(w_rows), n_p=len(outs_p), n_out=len(outs), epi_p=epi_p, epi_s=epi_s),
        grid=(n_tiles, mp // tm),
        in_specs=[pl.BlockSpec((tm, d), lambda j, i: (i, 0)),
                  pl.BlockSpec((ms, d), lambda j, i: (0, 0))] + [w_spec(*r) for r in w_rows],
        out_specs=[pl.BlockSpec(o[2], o[3]) for o in outs],
        out_shape=[jax.ShapeDtypeStruct(o[0], o[1]) for o in outs],
        scratch_shapes=[pltpu.VMEM((tn, d), xn_p.dtype)],
        compiler_params=_cparams(2),
        name=name,
    )(xn_p, xn_s, *([wt] * len(w_rows)))


def _row_major(m, ncols, tm, tn, dtype, sample):
    if sample:
        return ((m, ncols), dtype, (m, tn), lambda j, i: (0, j))
    return ((m, ncols), dtype, (tm, tn), lambda j, i: (i, j))


def _head_major(m, ncols, tm, tn, dtype, sample):
    nh, hb = ncols // _LANES, tn // _LANES
    if sample:
        return ((nh, m, _LANES), dtype, (hb, m, _LANES), lambda j, i: (j, 0, 0))
    return ((nh, m, _LANES), dtype, (hb, tm, _LANES), lambda j, i: (j, i, 0))


def _store_heads(ref, val):
    for hh in range(ref.shape[0]):
        ref[hh] = val[:, hh * _LANES:(hh + 1) * _LANES].astype(ref.dtype)


def _silu(x):
    return x * jax.nn.sigmoid(x)


def _bias_tables_body(rb_ref, tp_ref, ts_ref, *, n_buckets, tq, t_dec, page):
    h = pl.program_id(0)
    max_exact = n_buckets // 2

    def table(dist):
        d = jnp.maximum(dist, 0)
        df = jnp.maximum(d.astype(F32), 1.0)
        large = max_exact + (jnp.log(df / max_exact) / math.log(_MAX_DISTANCE / max_exact)
                             * (n_buckets - max_exact)).astype(I32)
        bucket = jnp.where(d < max_exact, d, jnp.minimum(large, n_buckets - 1))
        out = jnp.zeros(dist.shape, F32)
        for b in range(n_buckets):
            out = jnp.where(bucket == b, rb_ref[b, h], out)
        return out

    rp = lax.broadcasted_iota(I32, (tq, _LANES), 0)
    cp = lax.broadcasted_iota(I32, (tq, _LANES), 1)
    far = rb_ref[n_buckets - 1, h]
    tp_ref[0, 0] = (table(rp - cp) - far) * _LOG2E
    tp_ref[0, 1] = (table(rp - cp + tq) - far) * _LOG2E
    rs = lax.broadcasted_iota(I32, (t_dec, _LANES), 0)
    cs = lax.broadcasted_iota(I32, (t_dec, _LANES), 1)
    ts_ref[0, 0] = table(rs - cs + page)
    ts_ref[0, 1] = table(rs - cs)


def _bias_tables(rel_bias, tq, t_dec, page):
    n_buckets, n_heads = rel_bias.shape
    return pl.pallas_call(
        functools.partial(_bias_tables_body, n_buckets=n_buckets, tq=tq, t_dec=t_dec, page=page),
        grid=(n_heads,),
        in_specs=[pl.BlockSpec(memory_space=pltpu.SMEM)],
        out_specs=[pl.BlockSpec((1, 2, tq, _LANES), lambda h: (h, 0, 0, 0)),
                   pl.BlockSpec((1, 2, t_dec, _LANES), lambda h: (h, 0, 0, 0))],
        out_shape=[jax.ShapeDtypeStruct((n_heads, 2, tq, _LANES), F32),
                   jax.ShapeDtypeStruct((n_heads, 2, t_dec, _LANES), F32)],
        compiler_params=_cparams(1),
        name="bias_tables",
    )(rel_bias.astype(F32))


def _prompt_attn_body(qi_ref, wk_ref, ki_ref, q_ref, k_ref, v_ref, sg_ref, tb_ref, o_ref,
                      wt_ref, sc_ref, keys_ref, madd_ref, m_ref, l_ref, acc_ref,
                      *, tq, topk, idx_heads, n_kv, rep):
    i = pl.program_id(1)
    n_ch = i + 1
    hd = q_ref.shape[-1]
    rows = rep * tq
    nb = tq // _LANES
    hgrp = 2
    kc2 = 2 * tq

    def chunk(c):
        return pl.ds(pl.multiple_of(c * tq, tq), tq)

    def causal(c):
        key_pos = lax.broadcasted_iota(I32, (tq, tq), 0) + c * tq
        return key_pos <= lax.broadcasted_iota(I32, (tq, tq), 1) + i * tq

    wt_ref[...] = wk_ref[...].T

    def score_body(c2, carry):
        ks = pl.ds(pl.multiple_of(c2 * kc2, kc2), kc2)
        keys = ki_ref[ks, :]
        a = jnp.zeros((kc2, tq), F32)
        for hg in range(idx_heads // hgrp):
            qi2 = qi_ref[hg * hgrp:(hg + 1) * hgrp].reshape(hgrp * tq, -1)
            s = _nt_dot(keys, qi2)
            for r in range(hgrp):
                h = hg * hgrp + r
                a = a + jnp.maximum(s[:, r * tq:(r + 1) * tq], 0.0) * wt_ref[h:h + 1, :]
        sc_ref[ks, :] = a
        return carry

    lax.fori_loop(0, (n_ch + 1) // 2, score_body, 0)

    def key_body(c, carry):
        keys_ref[chunk(c), :] = jnp.where(causal(c), _sort_key(sc_ref[chunk(c), :]), _INT_MIN)
        return carry

    lax.fori_loop(0, n_ch, key_body, 0)

    def count(pred):
        def hits(c):
            hit = pred(keys_ref[chunk(c), :]).astype(F32)
            return jnp.sum(hit.reshape(tq // 8, 8, tq), axis=0)

        def pair_body(c2, acc):
            return acc[0] + hits(2 * c2), acc[1] + hits(2 * c2 + 1)

        zero = jnp.zeros((8, tq), F32)
        even, odd = lax.fori_loop(0, n_ch // 2, pair_body, (zero, zero))
        last = lax.cond(n_ch % 2 == 1, lambda: hits(n_ch - 1), lambda: zero)
        return jnp.sum(even + odd + last, axis=0, keepdims=True)

    def radix_step(it, thr):
        cand = thr + jnp.left_shift(jnp.int32(1), 31 - it)
        return jnp.where(count(lambda k: k >= cand) >= topk, cand, thr)

    thr = lax.fori_loop(0, 32, radix_step, jnp.full((1, tq), _INT_MIN, I32))
    need = topk - count(lambda k: k > thr)
    n_eq = count(lambda k: k == thr)
    tied = jnp.max(jnp.where((n_eq > need) & (thr != _INT_MIN), 1.0, 0.0)) > 0.0

    def madd_body(c, carry):
        keep = (keys_ref[chunk(c), :] >= thr) & causal(c)
        madd_ref[c] = jnp.where(keep, 0.0, _NEG).T
        return carry

    lax.fori_loop(0, n_ch, madd_body, 0)

    @pl.when(tied)
    def _():
        tri = (lax.broadcasted_iota(I32, (tq, tq), 1) <= lax.broadcasted_iota(I32, (tq, tq), 0)).astype(BF16)

        def tie_body(c, run):
            kc = keys_ref[chunk(c), :]
            eq = kc == thr
            pre = jnp.dot(tri, eq.astype(F32).astype(BF16), preferred_element_type=F32) + run
            keep = ((kc > thr) | (eq & (pre <= need))) & causal(c)
            madd_ref[c] = jnp.where(keep, 0.0, _NEG).T
            return run + jnp.sum(eq.astype(F32), axis=0, keepdims=True)

        lax.fori_loop(0, n_ch, tie_body, jnp.zeros((1, tq), F32))

    m_ref[...] = jnp.full(m_ref.shape, _NEG, F32)
    l_ref[...] = jnp.zeros(l_ref.shape, F32)
    acc_ref[...] = jnp.zeros(acc_ref.shape, F32)

    def attend(c, near):
        madd = madd_ref[c]
        for g in range(n_kv):
            q4 = q_ref[g * rep:(g + 1) * rep].reshape(rows, hd)
            s = _nt_dot(q4, k_ref[g, chunk(c), :])
            slabs = []
            for r in range(rep):
                for a in range(nb):
                    rs = slice(r * tq + a * _LANES, r * tq + (a + 1) * _LANES)
                    x = s[rs, :] + madd[a * _LANES:(a + 1) * _LANES, :]
                    if near is not None:
                        cols = []
                        for b in range(nb):
                            xb = x[:, b * _LANES:(b + 1) * _LANES]
                            kind = near * nb + a - b
                            if kind in (0, 1):
                                xb = xb + tb_ref[g * rep + r, kind]
                            cols.append(xb)
                        x = jnp.concatenate(cols, axis=1)
                    slabs.append(x)
            x = jnp.concatenate(slabs, axis=0)
            xm = x[:, 0:_LANES]
            for b in range(1, nb):
                xm = jnp.maximum(xm, x[:, b * _LANES:(b + 1) * _LANES])
            m_old = m_ref[g]
            m_new = jnp.maximum(m_old, jnp.max(xm, axis=-1, keepdims=True))
            alpha = jnp.exp2(m_old - m_new)
            p = jnp.exp2(x - jnp.concatenate([m_new] * nb, axis=1))
            ps = p[:, 0:_LANES]
            for b in range(1, nb):
                ps = ps + p[:, b * _LANES:(b + 1) * _LANES]
            l_ref[g] = alpha * l_ref[g] + ps
            acc_ref[g] = alpha * acc_ref[g] + jnp.dot(p.astype(_MXU_DTYPE), v_ref[g, chunk(c), :],
                                                      preferred_element_type=F32)
            m_ref[g] = m_new

    attend(i, 0)

    @pl.when(i >= 1)
    def _():
        attend(i - 1, 1)

    def far_body(c, carry):
        attend(c, None)
        return carry

    lax.fori_loop(0, jnp.maximum(i - 1, 0), far_body, 0)

    for g in range(n_kv):
        o = acc_ref[g] / jnp.sum(l_ref[g], axis=-1, keepdims=True)
        o = o * sg_ref[g * rep:(g + 1) * rep].reshape(rows, hd).astype(F32)
        o_ref[g * rep:(g + 1) * rep] = o.reshape(rep, tq, hd).astype(o_ref.dtype)


def _prompt_attention(qi_hm, wk, ki_bf, q_hm, k_hm, v_hm, sg_hm, tb_p, *, batch, seq, tq):
    idx_heads, m, idx_dim = qi_hm.shape
    n_heads, _, hd = q_hm.shape
    n_kv = k_hm.shape[0]
    rep = n_heads // n_kv
    n_q = seq // tq
    topk = min(_TOPK_MAX, seq // 4)
    qrow = lambda b, i: (0, b * n_q + i, 0)
    body = functools.partial(_prompt_attn_body, tq=tq, topk=topk, idx_heads=idx_heads, n_kv=n_kv, rep=rep)
    return pl.pallas_call(
        body,
        grid=(batch, n_q),
        in_specs=[pl.BlockSpec((idx_heads, tq, idx_dim), qrow),
                  pl.BlockSpec((tq, _LANES), lambda b, i: (b * n_q + i, 0)),
                  pl.BlockSpec((seq, idx_dim), lambda b, i: (b, 0)),
                  pl.BlockSpec((n_heads, tq, hd), qrow),
                  pl.BlockSpec((n_kv, seq, hd), lambda b, i: (0, b, 0)),
                  pl.BlockSpec((n_kv, seq, hd), lambda b, i: (0, b, 0)),
                  pl.BlockSpec((n_heads, tq, hd), qrow),
                  pl.BlockSpec((n_heads, 2, _LANES, _LANES), lambda b, i: (0, 0, 0, 0))],
        out_specs=pl.BlockSpec((n_heads, tq, hd), qrow),
        out_shape=jax.ShapeDtypeStruct((n_heads, m, hd), _MXU_DTYPE),
        scratch_shapes=[pltpu.VMEM((_LANES, tq), F32),
                        pltpu.VMEM((seq, tq), F32),
                        pltpu.VMEM((seq, tq), I32),
                        pltpu.VMEM((n_q, tq, tq), F32),
                        pltpu.VMEM((n_kv, rep * tq, _LANES), F32),
                        pltpu.VMEM((n_kv, rep * tq, _LANES), F32),
                        pltpu.VMEM((n_kv, rep * tq, hd), F32)],
        compiler_params=_cparams(2),
        name="prompt_attention",
    )(qi_hm, wk, ki_bf, q_hm, k_hm, v_hm, sg_hm, tb_p)


def _page_copies(pt_ref, cache_ref, buf_ref, sem_ref, b, first_page, n_copy, slot, page):
    return [pltpu.make_async_copy(cache_ref.at[pt_ref[b, first_page + p]],
                                  buf_ref.at[slot, pl.ds(p * page, page)],
                                  sem_ref.at[slot]) for p in range(n_copy)]


def _sample_scores_body(pt_ref, qi_ref, wk_ref, cache_ref, o_ref, buf_ref, wb_ref, sem_ref,
                        *, n_pages, page, idx_heads, t_dec):
    b = pl.program_id(0)
    nb = pl.num_programs(0)
    slot = b % 2
    copies = functools.partial(_page_copies, pt_ref, cache_ref, buf_ref, sem_ref,
                               first_page=0, n_copy=n_pages, page=page)

    @pl.when(b == 0)
    def _():
        for c in copies(b=0, slot=0):
            c.start()

    @pl.when(b + 1 < nb)
    def _():
        for c in copies(b=b + 1, slot=1 - slot):
            c.start()

    qi = qi_ref[...].reshape(idx_heads * t_dec, -1).astype(_MXU_DTYPE)
    w_blk = wk_ref[:, 0:_LANES]
    for h in range(idx_heads):
        wb_ref[h * t_dec:(h + 1) * t_dec, :] = jnp.broadcast_to(w_blk[:, h:h + 1], (t_dec, _LANES))

    def page_scores(keys):
        s = _nt_dot(qi, keys.astype(_MXU_DTYPE))
        return jnp.sum((jnp.maximum(s, 0.0) * wb_ref[...]).reshape(idx_heads, t_dec, page), axis=0)

    past = n_pages * page
    ki_new = jnp.concatenate([wk_ref[:, _LANES:], jnp.zeros((page - t_dec, wk_ref.shape[1] - _LANES), F32)], axis=0)
    row = lax.broadcasted_iota(I32, (t_dec, page), 0)
    col = lax.broadcasted_iota(I32, (t_dec, page), 1)
    o_ref[0, :, past:past + page] = jnp.where(col <= row, page_scores(ki_new), -jnp.inf)

    for c in copies(b=b, slot=slot):
        c.wait()
    for p in range(n_pages):
        o_ref[0, :, p * page:(p + 1) * page] = page_scores(buf_ref[slot, p * page:(p + 1) * page, :])


def _sample_scores(page_table, qi_s, wk_s, cache_idx, *, t_dec):
    bd, n_pages = page_table.shape
    _, page, idx_dim = cache_idx.shape
    idx_heads = qi_s.shape[0]
    past = n_pages * page
    grid_spec = pltpu.PrefetchScalarGridSpec(
        num_scalar_prefetch=1,
        grid=(bd,),
        in_specs=[pl.BlockSpec((idx_heads, t_dec, idx_dim), lambda b, pt: (0, b, 0)),
                  pl.BlockSpec((t_dec, wk_s.shape[1]), lambda b, pt: (b, 0)),
                  pl.BlockSpec(memory_space=pl.ANY)],
        out_specs=pl.BlockSpec((1, t_dec, past + page), lambda b, pt: (b, 0, 0)),
        scratch_shapes=[pltpu.VMEM((2, past, idx_dim), F32),
                        pltpu.VMEM((idx_heads * t_dec, _LANES), F32),
                        pltpu.SemaphoreType.DMA((2,))],
    )
    return pl.pallas_call(
        functools.partial(_sample_scores_body, n_pages=n_pages, page=page, idx_heads=idx_heads, t_dec=t_dec),
        grid_spec=grid_spec,
        out_shape=jax.ShapeDtypeStruct((bd, t_dec, past + page), F32),
        compiler_params=_cparams(1),
        name="sample_scores",
    )(page_table, qi_s, wk_s, cache_idx)


def _sample_select_body(sc_ref, o_ref, keys_ref, *, topk, past):
    bd, t_dec, n_cols = sc_ref.shape

    def valid(c0, width):
        col = lax.broadcasted_iota(I32, (bd, t_dec, width), 2) + c0
        return col <= past + lax.broadcasted_iota(I32, (bd, t_dec, width), 1)

    keys_ref[...] = jnp.where(valid(0, n_cols), _sort_key(sc_ref[...]), _INT_MIN)

    def step(it, thr):
        cand = thr + jnp.left_shift(jnp.int32(1), 31 - it)
        cnt = jnp.sum((keys_ref[...] >= cand).astype(F32), axis=-1, keepdims=True)
        return jnp.where(cnt >= topk, cand, thr)

    thr = lax.fori_loop(0, 32, step, jnp.full((bd, t_dec, 1), _INT_MIN, I32))
    keys = keys_ref[...]
    cnt_gt = jnp.sum((keys > thr).astype(F32), axis=-1, keepdims=True)
    cnt_eq = jnp.sum((keys == thr).astype(F32), axis=-1, keepdims=True)
    need = topk - cnt_gt
    o_ref[...] = ((keys >= thr) & valid(0, n_cols)).astype(F32)
    tied = jnp.max(jnp.where((cnt_eq > need) & (thr != _INT_MIN), 1.0, 0.0)) > 0.0

    @pl.when(tied)
    def _():
        tri = (lax.broadcasted_iota(I32, (_LANES, _LANES), 0)
               <= lax.broadcasted_iota(I32, (_LANES, _LANES), 1)).astype(BF16)
        run = jnp.zeros((bd, t_dec, 1), F32)
        for c0 in range(0, n_cols, _LANES):
            kc = keys_ref[:, :, c0:c0 + _LANES]
            eq = kc == thr
            eq2 = eq.astype(F32).reshape(bd * t_dec, _LANES).astype(BF16)
            pre = jnp.dot(eq2, tri, preferred_element_type=F32).reshape(bd, t_dec, _LANES) + run
            run = run + jnp.sum(eq.astype(F32), axis=-1, keepdims=True)
            keep = (kc > thr) | (eq & (pre <= need))
            o_ref[:, :, c0:c0 + _LANES] = (keep & valid(c0, _LANES)).astype(F32)


def _sample_select(scores, *, topk, past):
    return pl.pallas_call(
        functools.partial(_sample_select_body, topk=topk, past=past),
        out_shape=jax.ShapeDtypeStruct(scores.shape, F32),
        scratch_shapes=[pltpu.VMEM(scores.shape, I32)],
        compiler_params=pltpu.CompilerParams(vmem_limit_bytes=_VMEM_LIMIT),
        name="sample_select",
    )(scores)


def _sample_attn_body(pt_ref, rb_ref, q_ref, mask_ref, mtail_ref, kn_ref, vn_ref, sg_ref, ts_ref,
                      ck_ref, cv_ref, o_ref, kbuf_ref, vbuf_ref, m_ref, l_ref, acc_ref, ksem_ref, vsem_ref,
                      *, n_chunks, cpages, page, n_kv, rep, t_dec, n_buckets):
    b = pl.program_id(0)
    c = pl.program_id(1)
    step = b * n_chunks + c
    n_steps = pl.num_programs(0) * n_chunks
    slot = step % 2
    hd = q_ref.shape[-1]
    rows = rep * t_dec
    ck = cpages * page

    prow = page * n_kv

    def copies(bb, cc, sl):
        return (_page_copies(pt_ref, ck_ref, kbuf_ref, ksem_ref, bb, cc * cpages, cpages, sl, prow)
                + _page_copies(pt_ref, cv_ref, vbuf_ref, vsem_ref, bb, cc * cpages, cpages, sl, prow))

    def kv_head(buf_ref, g):
        return buf_ref[slot, pl.ds(g, ck, stride=n_kv), :]

    @pl.when(step == 0)
    def _():
        for cp in copies(0, 0, 0):
            cp.start()

    @pl.when(step + 1 < n_steps)
    def _():
        nxt = step + 1
        for cp in copies(nxt // n_chunks, nxt % n_chunks, 1 - slot):
            cp.start()

    @pl.when(c == 0)
    def _():
        m_ref[...] = jnp.full(m_ref.shape, -jnp.inf, F32)
        l_ref[...] = jnp.zeros(l_ref.shape, F32)
        acc_ref[...] = jnp.zeros(acc_ref.shape, F32)

    def update(g, logits, mask, values):
        n = logits.shape[-1]
        x = jnp.where(jnp.broadcast_to(mask[None], (rep, t_dec, n)).reshape(rows, n) > 0.5, logits, -jnp.inf)
        rs = slice(g * rows, (g + 1) * rows)
        m_old = m_ref[rs, :]
        m_new = jnp.maximum(m_old, jnp.max(x, axis=-1, keepdims=True))
        m_safe = jnp.where(m_new == -jnp.inf, 0.0, m_new)
        alpha = jnp.exp(m_old - m_safe)
        p = jnp.exp(x - m_safe[:, 0:1])
        l_ref[rs, :] = alpha * l_ref[rs, :] + jnp.sum(p, axis=-1, keepdims=True)
        acc_ref[rs, :] = alpha * acc_ref[rs, :] + jnp.dot(p.astype(_MXU_DTYPE), values.astype(_MXU_DTYPE),
                                                          preferred_element_type=F32)
        m_ref[rs, :] = m_new

    def head_bias(g, kind):
        return ts_ref[g * rep:(g + 1) * rep, kind].reshape(rows, _LANES)

    def far_bias(g):
        r_idx = lax.broadcasted_iota(I32, (rows, 1), 0) // t_dec
        out = jnp.zeros((rows, 1), F32)
        for r in range(rep):
            out = jnp.where(r_idx == r, rb_ref[n_buckets - 1, g * rep + r], out)
        return out

    for cp in copies(b, c, slot):
        cp.wait()

    is_last = (c == n_chunks - 1).astype(F32)
    for g in range(n_kv):
        qg = q_ref[g * rep:(g + 1) * rep].reshape(rows, hd).astype(_MXU_DTYPE)
        kc = kv_head(kbuf_ref, g)
        far = far_bias(g)
        lg = _nt_dot(qg, kc.astype(_MXU_DTYPE)) + far
        near = lg[:, ck - page:] + (head_bias(g, 0) - far) * is_last
        lg = jnp.concatenate([lg[:, :ck - page], near], axis=1)
        update(g, lg, mask_ref[...], kv_head(vbuf_ref, g))

    @pl.when(c == n_chunks - 1)
    def _():
        zpad = jnp.zeros((page - t_dec, hd), F32)
        for g in range(n_kv):
            qg = q_ref[g * rep:(g + 1) * rep].reshape(rows, hd).astype(_MXU_DTYPE)
            kn = jnp.concatenate([kn_ref[pl.ds(g, t_dec, stride=n_kv), :], zpad], axis=0)
            vn = jnp.concatenate([vn_ref[pl.ds(g, t_dec, stride=n_kv), :], zpad], axis=0)
            lg = _nt_dot(qg, kn.astype(_MXU_DTYPE)) + head_bias(g, 1)
            update(g, lg, mtail_ref[...], vn)
            rs = slice(g * rows, (g + 1) * rows)
            o = acc_ref[rs, :] / l_ref[rs, :] * sg_ref[g * rep:(g + 1) * rep].reshape(rows, hd)
            o_ref[g * rep:(g + 1) * rep] = o.reshape(rep, t_dec, hd)


def _sample_attention(page_table, rel_bias, q_s, mask, kn, vn, sg_s, ts, cache_k, cache_v, *, t_dec, cpages, page):
    bd, n_pages = page_table.shape
    n_heads, _, hd = q_s.shape
    n_kv = cache_k.shape[1] // page
    rep = n_heads // n_kv
    n_chunks = n_pages // cpages
    ck = cpages * page
    past = n_pages * page
    hrow = lambda b, c, pt: (0, b, 0)
    grid_spec = pltpu.PrefetchScalarGridSpec(
        num_scalar_prefetch=1,
        grid=(bd, n_chunks),
        in_specs=[pl.BlockSpec(memory_space=pltpu.SMEM),
                  pl.BlockSpec((n_heads, t_dec, hd), hrow),
                  pl.BlockSpec((t_dec, ck), lambda b, c, pt: (b, c)),
                  pl.BlockSpec((t_dec, page), lambda b, c, pt: (b, past // page)),
                  pl.BlockSpec((t_dec * n_kv, hd), lambda b, c, pt: (b, 0)),
                  pl.BlockSpec((t_dec * n_kv, hd), lambda b, c, pt: (b, 0)),
                  pl.BlockSpec((n_heads, t_dec, hd), hrow),
                  pl.BlockSpec((n_heads, 2, t_dec, _LANES), lambda b, c, pt: (0, 0, 0, 0)),
                  pl.BlockSpec(memory_space=pl.ANY),
                  pl.BlockSpec(memory_space=pl.ANY)],
        out_specs=pl.BlockSpec((n_heads, t_dec, hd), hrow),
        scratch_shapes=[pltpu.VMEM((2, ck * n_kv, hd), F32),
                        pltpu.VMEM((2, ck * n_kv, hd), F32),
                        pltpu.VMEM((n_heads * t_dec, _LANES), F32),
                        pltpu.VMEM((n_heads * t_dec, _LANES), F32),
                        pltpu.VMEM((n_heads * t_dec, hd), F32),
                        pltpu.SemaphoreType.DMA((2,)),
                        pltpu.SemaphoreType.DMA((2,))],
    )
    return pl.pallas_call(
        functools.partial(_sample_attn_body, n_chunks=n_chunks, cpages=cpages, page=page, n_kv=n_kv,
                          rep=rep, t_dec=t_dec, n_buckets=rel_bias.shape[0]),
        grid_spec=grid_spec,
        out_shape=jax.ShapeDtypeStruct((n_heads, bd * t_dec, hd), F32),
        compiler_params=_cparams(2),
        name="sample_attention",
    )(page_table, rel_bias.astype(F32), q_s, mask, mask, kn, vn, sg_s, ts, cache_k, cache_v)


def _pool_body(u_ref, halo_ref, gb_ref, wp_ref, ps_ref, o_ref, ext_ref, *, tm, pos_base, own_halo, pg):
    i = pl.program_id(1)
    bb = u_ref.shape[0]
    halo = halo_ref[...]
    if own_halo:
        halo = jnp.where(i > 0, halo, 0.0)
    ext_ref[:, 0:_POOL_HALO, :] = halo
    ext_ref[:, _POOL_HALO:, :] = u_ref[...]
    pos = pos_base + i * tm + lax.broadcasted_iota(I32, (1, tm, 1), 1)
    for g, win in enumerate(_POOL_WINDOWS):
        cols = slice(g * pg, (g + 1) * pg)
        tot = ext_ref[:, _POOL_HALO:, cols]
        for k in range(1, win):
            tot = tot + ext_ref[:, _POOL_HALO - k:_POOL_HALO - k + tm, cols]
        cnt = jnp.minimum(pos + 1, win).astype(F32)
        d = tot / cnt - ext_ref[:, _POOL_HALO:, cols]
        y = jnp.dot(d.reshape(bb * tm, pg).astype(_MXU_DTYPE), wp_ref[g], preferred_element_type=F32)
        y = y.reshape(bb, tm, pg) * ps_ref[:, cols] * gb_ref[:, :, cols].astype(F32)
        o_ref[:, :, cols] = y.astype(o_ref.dtype)


def _pool(u, halo, gb, wp, ps, *, bb, tm, pos_base, own_halo, out_dtype):
    n_seq, rows, pw = u.shape
    pg = pw // len(_POOL_WINDOWS)
    if own_halo:
        hb = tm // _POOL_HALO
        halo_spec = pl.BlockSpec((bb, _POOL_HALO, pw), lambda b, i: (b, jnp.maximum(i * hb - 1, 0), 0))
    else:
        halo_spec = pl.BlockSpec((bb, _POOL_HALO, pw), lambda b, i: (b, 0, 0))
    return pl.pallas_call(
        functools.partial(_pool_body, tm=tm, pos_base=pos_base, own_halo=own_halo, pg=pg),
        grid=(n_seq // bb, rows // tm),
        in_specs=[pl.BlockSpec((bb, tm, pw), lambda b, i: (b, i, 0)),
                  halo_spec,
                  pl.BlockSpec((bb, tm, pw), lambda b, i: (b, i, 0)),
                  pl.BlockSpec(wp.shape, lambda b, i: (0, 0, 0)),
                  pl.BlockSpec((1, pw), lambda b, i: (0, 0))],
        out_specs=pl.BlockSpec((bb, tm, pw), lambda b, i: (b, i, 0)),
        out_shape=jax.ShapeDtypeStruct((n_seq, rows, pw), out_dtype),
        scratch_shapes=[pltpu.VMEM((bb, _POOL_HALO + tm, pw), F32)],
        compiler_params=_cparams(2),
        name="pool",
    )(u, halo, gb, wp, ps)


def _merge_body(ma_ref, mb_ref, w_ref, x_ref, g_ref, o_ref, mix_ref, *, n_j, tn):
    j = pl.program_id(1)
    n_heads, _, hd = ma_ref.shape

    @pl.when(j == 0)
    def _():
        for h in range(n_heads):
            mix_ref[:, h * hd:(h + 1) * hd] = ma_ref[h].astype(mix_ref.dtype)
        mix_ref[:, n_heads * hd:] = mb_ref[...].astype(mix_ref.dtype)

    hv = x_ref[...] + jnp.dot(mix_ref[...], w_ref[...], preferred_element_type=F32)
    for jj in range(n_j):
        @pl.when(j == jj)
        def _(jj=jj):
            o_ref[:, jj * tn:(jj + 1) * tn] = hv

    @pl.when(j == n_j - 1)
    def _():
        ssq = jnp.zeros((o_ref.shape[0], 1), F32)
        for jj in range(n_j):
            hj = o_ref[:, jj * tn:(jj + 1) * tn]
            ssq = ssq + jnp.sum(hj * hj, axis=-1, keepdims=True)
        inv = lax.rsqrt(ssq / (n_j * tn) + _RMS_EPS)
        for jj in range(n_j):
            cols = slice(jj * tn, (jj + 1) * tn)
            o_ref[:, cols] = o_ref[:, cols] * inv * g_ref[:, cols]


def _merge(mix_a, mix_b, w_out, x2d, final_g, *, tm, tn):
    n_heads, m, hd = mix_a.shape
    pw = mix_b.shape[1]
    mixw, d = w_out.shape
    n_j = d // tn
    return pl.pallas_call(
        functools.partial(_merge_body, n_j=n_j, tn=tn),
        grid=(m // tm, n_j),
        in_specs=[pl.BlockSpec((n_heads, tm, hd), lambda i, j: (0, i, 0)),
                  pl.BlockSpec((tm, pw), lambda i, j: (i, 0)),
                  pl.BlockSpec((mixw, tn), lambda i, j: (0, j)),
                  pl.BlockSpec((tm, tn), lambda i, j: (i, j)),
                  pl.BlockSpec((1, d), lambda i, j: (0, 0))],
        out_specs=pl.BlockSpec((tm, d), lambda i, j: (i, 0)),
        out_shape=jax.ShapeDtypeStruct((m, d), F32),
        scratch_shapes=[pltpu.VMEM((tm, mixw), _MXU_DTYPE)],
        compiler_params=_cparams(2),
        name="merge",
    )(mix_a, mix_b, w_out, x2d, final_g.reshape(1, d).astype(F32))


def kernel(x_prompt, x_sample, cache_k, cache_v, cache_idx_k, state_pool, page_table,
           rel_bias, ln_g, w_in, w_pool, pool_scale, w_out, final_g):
    depth = w_in.shape[0]
    assert depth == 1, "single-layer step only"
    batch, seq, d_model = x_prompt.shape
    bd, t_dec, _ = x_sample.shape
    _, n_phys, page, n_kv, hd = cache_k.shape
    idx_dim = cache_idx_k.shape[-1]
    n_buckets, n_heads = rel_bias.shape
    pw = pool_scale.shape[-1]
    n_pg = w_pool.shape[1]
    attn_w, kv_w = n_heads * hd, n_kv * hd
    in_w = w_in.shape[-1]
    idx_heads = (in_w - 2 * attn_w - 2 * kv_w - idx_dim - 2 * pw) // (idx_dim + 1)
    n_pages = page_table.shape[1]
    past = n_pages * page
    mp, ms = batch * seq, bd * t_dec
    tq = 2 * _LANES
    assert hd == _LANES and idx_dim == _LANES and idx_heads <= _LANES and n_pg == len(_POOL_WINDOWS)
    assert page >= _MAX_DISTANCE and _LANES >= _MAX_DISTANCE and seq % (2 * tq) == 0 and t_dec <= _POOL_HALO - 1
    assert state_pool.shape[2] == _POOL_HALO - 1

    o_q, o_k, o_v = 0, attn_w, attn_w + kv_w
    o_ga = o_v + kv_w
    o_qi = o_ga + attn_w
    o_wi = o_qi + idx_heads * idx_dim
    o_ki = o_wi + idx_heads
    o_u = o_ki + idx_dim
    o_gb = o_u + pw
    wt = jnp.transpose(w_in[0])

    xn_p = _rmsnorm(x_prompt.reshape(mp, d_model), ln_g[0], _MXU_DTYPE)
    xn_s = _rmsnorm(x_sample.reshape(ms, d_model), ln_g[0], _MXU_DTYPE)

    tm = min(1024, mp)
    tn = lambda n: min(512, n)
    rm = lambda n, t, dt, s: _row_major(ms if s else mp, n, tm, t, dt, s)
    hm = lambda n, t, dt, s: _head_major(ms if s else mp, n, tm, t, dt, s)

    def proj(lo, hi, t, **kw):
        assert (hi - lo) % t == 0 and lo % 8 == 0
        return _proj_call(xn_p, xn_s, wt, [(lo, t)], (hi - lo) // t, tm, **kw)

    def scaled_heads(scale):
        def epi(acc, refs):
            _store_heads(refs[0], acc * scale)
        return epi

    def silu_heads(acc, refs):
        _store_heads(refs[0], _silu(acc))

    def silu_rows(acc, refs):
        refs[0][...] = _silu(acc).astype(refs[0].dtype)

    def plain_rows(acc, refs):
        refs[0][...] = acc.astype(refs[0].dtype)

    def kv_rows(acc, ref):
        for g in range(n_kv):
            ref[pl.ds(g, acc.shape[0], stride=n_kv), :] = acc[:, g * hd:(g + 1) * hd]

    def kv_p(acc, refs):
        kv_rows(acc, refs[0])
        _store_heads(refs[1], acc)

    def kv_s(acc, refs):
        kv_rows(acc, refs[0])

    def wk_p(acc, refs):
        refs[0][...] = jnp.concatenate([acc[:, :_LANES] * idx_heads ** -0.5, acc[:, _LANES:]], axis=1)
        refs[1][...] = acc[:, _LANES:].astype(refs[1].dtype)

    def wk_s(acc, refs):
        refs[0][...] = jnp.concatenate([acc[:, :_LANES] * idx_heads ** -0.5, acc[:, _LANES:]], axis=1)

    t = tn(attn_w)
    q_hm, q_s = proj(o_q, o_k, t, outs_p=[hm(attn_w, t, _MXU_DTYPE, False)], outs_s=[hm(attn_w, t, F32, True)],
                     epi_p=scaled_heads(hd ** -0.5 * _LOG2E), epi_s=scaled_heads(hd ** -0.5), name="proj_q")
    kv_outs = dict(outs_p=[((mp * n_kv, hd), F32, (tm * n_kv, hd), lambda j, i: (i, 0)),
                           ((n_kv, mp, hd), _MXU_DTYPE, (n_kv, tm, hd), lambda j, i: (0, i, 0))],
                   outs_s=[((ms * n_kv, hd), F32, (ms * n_kv, hd), lambda j, i: (0, 0))], epi_p=kv_p, epi_s=kv_s)
    k_p, k_hm, k_s = proj(o_k, o_v, kv_w, name="proj_k", **kv_outs)
    v_p, v_hm, v_s = proj(o_v, o_ga, kv_w, name="proj_v", **kv_outs)
    sga_hm, sga_s = proj(o_ga, o_qi, t, outs_p=[hm(attn_w, t, _MXU_DTYPE, False)],
                         outs_s=[hm(attn_w, t, F32, True)], epi_p=silu_heads, epi_s=silu_heads, name="proj_ga")
    t = tn(idx_heads * idx_dim)
    qi_hm, qi_s = proj(o_qi, o_wi, t, outs_p=[hm(idx_heads * idx_dim, t, _MXU_DTYPE, False)],
                       outs_s=[hm(idx_heads * idx_dim, t, F32, True)],
                       epi_p=scaled_heads(idx_dim ** -0.5), epi_s=scaled_heads(idx_dim ** -0.5), name="proj_qi")
    t = 2 * _LANES
    wk_pr, ki_bf, wk_sm = _proj_call(
        xn_p, xn_s, wt, [(o_wi, _LANES), (o_ki, idx_dim)], 1, tm,
        outs_p=[((mp, t), F32, (tm, t), lambda j, i: (i, 0)), ((mp, idx_dim), _MXU_DTYPE, (tm, idx_dim), lambda j, i: (i, 0))],
        outs_s=[((ms, t), F32, (ms, t), lambda j, i: (0, 0))],
        epi_p=wk_p, epi_s=wk_s, name="proj_wk")
    t = tn(pw)
    u_p, u_s = proj(o_u, o_gb, t, outs_p=[rm(pw, t, F32, False)], outs_s=[rm(pw, t, F32, True)],
                    epi_p=plain_rows, epi_s=plain_rows, name="proj_u")
    sgb_p, sgb_s = proj(o_gb, in_w, t, outs_p=[rm(pw, t, _MXU_DTYPE, False)], outs_s=[rm(pw, t, F32, True)],
                        epi_p=silu_rows, epi_s=silu_rows, name="proj_gb")

    tb_p, tb_s = _bias_tables(rel_bias, _LANES, t_dec, page)

    mixa_p = _prompt_attention(qi_hm, wk_pr, ki_bf, q_hm, k_hm, v_hm, sga_hm, tb_p, batch=batch, seq=seq, tq=tq)

    scores = _sample_scores(page_table, qi_s, wk_sm, cache_idx_k[0], t_dec=t_dec)
    topk_s = min(_TOPK_MAX, (past + t_dec) // 4)
    mask_s = _sample_select(scores, topk=topk_s, past=past).reshape(ms, past + page)
    cpages = max(1, min(32, n_pages // 2))
    assert n_pages % cpages == 0
    mixa_s = _sample_attention(page_table, rel_bias, q_s, mask_s, k_s, v_s, sga_s, tb_s,
                               cache_k[0].reshape(n_phys, page * n_kv, hd), cache_v[0].reshape(n_phys, page * n_kv, hd),
                               t_dec=t_dec, cpages=cpages, page=page)

    wp = w_pool[0].astype(_MXU_DTYPE)
    ps = pool_scale[0].reshape(1, pw).astype(F32)
    u_p3 = u_p.reshape(batch, seq, pw)
    mixb_p = _pool(u_p3, u_p3, sgb_p.reshape(batch, seq, pw), wp, ps, bb=1, tm=min(512, seq), pos_base=0,
                   own_halo=True, out_dtype=_MXU_DTYPE).reshape(mp, pw)
    halo_s = jnp.concatenate([jnp.zeros((bd, 1, pw), F32), state_pool[0].astype(F32)], axis=1)
    mixb_s = _pool(u_s.reshape(bd, t_dec, pw), halo_s, sgb_s.reshape(bd, t_dec, pw), wp, ps, bb=bd, tm=t_dec,
                   pos_base=past, own_halo=False, out_dtype=F32).reshape(ms, pw)

    w_o = w_out[0].astype(_MXU_DTYPE)
    tn_o = min(1024, d_model)
    y_p = _merge(mixa_p, mixb_p, w_o, x_prompt.reshape(mp, d_model), final_g, tm=min(512, mp), tn=tn_o)
    y_s = _merge(mixa_s, mixb_s, w_o, x_sample.reshape(ms, d_model), final_g, tm=ms, tn=tn_o)

    n_state = state_pool.shape[2]
    u_full_s = jnp.concatenate([state_pool[0].astype(F32), u_s.reshape(bd, t_dec, pw)], axis=1)
    return (y_p.reshape(batch, seq, d_model), y_s.reshape(bd, t_dec, d_model),
            k_p.reshape(1, batch, seq, n_kv, hd), v_p.reshape(1, batch, seq, n_kv, hd),
            wk_pr[:, _LANES:].reshape(1, batch, seq, idx_dim),
            u_p.reshape(batch, seq, pw)[:, seq - n_state:][None],
            k_s.reshape(1, bd, t_dec, n_kv, hd), v_s.reshape(1, bd, t_dec, n_kv, hd),
            wk_sm[:, _LANES:].reshape(1, bd, t_dec, idx_dim),
            u_full_s[:, t_dec:][None])
```

```python
import functools
import math

import jax
import jax.numpy as jnp
from jax import lax
from jax.experimental import pallas as pl
from jax.experimental.pallas import tpu as pltpu

F32 = jnp.float32
BF16 = jnp.bfloat16
I32 = jnp.int32

_TOPK_MAX = 256
_POOL_WINDOWS = (2, 4, 8, 16)
_POOL_HALO = 16
_MAX_DISTANCE = 128
_RMS_EPS = 1e-6
_NEG = -1e30
_LOG2E = math.log2(math.e)
_INT_MIN = -(2 ** 31)
_LANES = 128
_SUBLANES = 8
_VMEM_LIMIT = 56 * 1024 * 1024
_MXU_DTYPE = BF16

_RMS_ROWS = 512
_PROJ_ROWS = 1024
_PROJ_COLS = 512
_PROMPT_Q = 2 * _LANES
_PAGES_PER_STEP = 32
_POOL_ROWS = 512
_MERGE_ROWS = 512
_MERGE_COLS = 1024


def _cparams(n_axes):
    return pltpu.CompilerParams(dimension_semantics=("arbitrary",) * n_axes,
                                vmem_limit_bytes=_VMEM_LIMIT)


def _nt_dot(a, b):
    return lax.dot_general(a, b, (((1,), (1,)), ((), ())), preferred_element_type=F32)


def _sort_key(x):
    bits = lax.bitcast_convert_type(x + 0.0, I32)
    return bits ^ ((bits >> 31) & 0x7FFFFFFF)


def _rmsnorm_body(x_ref, g_ref, o_ref):
    x = x_ref[...].astype(F32)
    var = jnp.mean(x * x, axis=-1, keepdims=True)
    o_ref[...] = (x * lax.rsqrt(var + _RMS_EPS) * g_ref[...]).astype(o_ref.dtype)


def _rmsnorm(x2d, g, out_dtype):
    m, d = x2d.shape
    tm = min(_RMS_ROWS, m)
    return pl.pallas_call(
        _rmsnorm_body,
        grid=(m // tm,),
        in_specs=[pl.BlockSpec((tm, d), lambda i: (i, 0)), pl.BlockSpec((1, d), lambda i: (0, 0))],
        out_specs=pl.BlockSpec((tm, d), lambda i: (i, 0)),
        out_shape=jax.ShapeDtypeStruct((m, d), out_dtype),
        compiler_params=_cparams(1),
        name="rmsnorm",
    )(x2d, g.reshape(1, d).astype(F32))


def _proj_body(xp_ref, xs_ref, *refs, n_w, n_p, n_out, epi_p, epi_s):
    w_refs, out_refs, wc_ref = refs[:n_w], refs[n_w:n_w + n_out], refs[n_w + n_out]
    first_row_tile = pl.program_id(1) == 0

    @pl.when(first_row_tile)
    def _():
        r0 = 0
        for w_ref in w_refs:
            wc_ref[r0:r0 + w_ref.shape[0], :] = w_ref[...].astype(wc_ref.dtype)
            r0 += w_ref.shape[0]

    epi_p(_nt_dot(xp_ref[...], wc_ref[...]), out_refs[:n_p])

    @pl.when(first_row_tile)
    def _():
        epi_s(_nt_dot(xs_ref[...], wc_ref[...]), out_refs[n_p:])


def _proj_call(xn_p, xn_s, wt, w_rows, n_tiles, tm, outs_p, outs_s, epi_p, epi_s, name):
    mp, d = xn_p.shape
    ms = xn_s.shape[0]
    assert mp % tm == 0
    tn = sum(cnt for _, cnt in w_rows)
    outs = list(outs_p) + list(outs_s)

    def w_spec(first, cnt):
        assert first % _SUBLANES == 0 and cnt % _SUBLANES == 0
        return pl.BlockSpec((pl.Element(cnt), pl.Element(d)),
                            lambda j, i: (pl.multiple_of(first + j * cnt, _SUBLANES), 0))

    return pl.pallas_call(
        functools.partial(_proj_body, n_w=len(w_rows), n_p=len(outs_p), n_out=len(outs), epi_p=epi_p, epi_s=epi_s),
        grid=(n_tiles, mp // tm),
        in_specs=[pl.BlockSpec((tm, d), lambda j, i: (i, 0)),
                  pl.BlockSpec((ms, d), lambda j, i: (0, 0))] + [w_spec(*r) for r in w_rows],
        out_specs=[pl.BlockSpec(o[2], o[3]) for o in outs],
        out_shape=[jax.ShapeDtypeStruct(o[0], o[1]) for o in outs],
        scratch_shapes=[pltpu.VMEM((tn, d), xn_p.dtype)],
        compiler_params=_cparams(2),
        name=name,
    )(xn_p, xn_s, *([wt] * len(w_rows)))


def _row_major(m, ncols, tm, tn, dtype, sample):
    if sample:
        return ((m, ncols), dtype, (m, tn), lambda j, i: (0, j))
    return ((m, ncols), dtype, (tm, tn), lambda j, i: (i, j))


def _head_major(m, ncols, tm, tn, dtype, sample):
    nh, hb = ncols // _LANES, tn // _LANES
    if sample:
        return ((nh, m, _LANES), dtype, (hb, m, _LANES), lambda j, i: (j, 0, 0))
    return ((nh, m, _LANES), dtype, (hb, tm, _LANES), lambda j, i: (j, i, 0))


def _store_heads(ref, val):
    for hh in range(ref.shape[0]):
        ref[hh] = val[:, hh * _LANES:(hh + 1) * _LANES].astype(ref.dtype)


def _silu(x):
    return x * jax.nn.sigmoid(x)


def _bias_tables_body(rb_ref, tp_ref, ts_ref, *, n_buckets, tq, t_dec, page):
    h = pl.program_id(0)
    max_exact = n_buckets // 2

    def table(dist):
        d = jnp.maximum(dist, 0)
        df = jnp.maximum(d.astype(F32), 1.0)
        large = max_exact + (jnp.log(df / max_exact) / math.log(_MAX_DISTANCE / max_exact)
                             * (n_buckets - max_exact)).astype(I32)
        bucket = jnp.where(d < max_exact, d, jnp.minimum(large, n_buckets - 1))
        out = jnp.zeros(dist.shape, F32)
        for b in range(n_buckets):
            out = jnp.where(bucket == b, rb_ref[b, h], out)
        return out

    rp = lax.broadcasted_iota(I32, (tq, _LANES), 0)
    cp = lax.broadcasted_iota(I32, (tq, _LANES), 1)
    far = rb_ref[n_buckets - 1, h]
    tp_ref[0, 0] = (table(rp - cp) - far) * _LOG2E
    tp_ref[0, 1] = (table(rp - cp + tq) - far) * _LOG2E
    rs = lax.broadcasted_iota(I32, (t_dec, _LANES), 0)
    cs = lax.broadcasted_iota(I32, (t_dec, _LANES), 1)
    ts_ref[0, 0] = table(rs - cs + page)
    ts_ref[0, 1] = table(rs - cs)


def _bias_tables(rel_bias, tq, t_dec, page):
    n_buckets, n_heads = rel_bias.shape
    return pl.pallas_call(
        functools.partial(_bias_tables_body, n_buckets=n_buckets, tq=tq, t_dec=t_dec, page=page),
        grid=(n_heads,),
        in_specs=[pl.BlockSpec(memory_space=pltpu.SMEM)],
        out_specs=[pl.BlockSpec((1, 2, tq, _LANES), lambda h: (h, 0, 0, 0)),
                   pl.BlockSpec((1, 2, t_dec, _LANES), lambda h: (h, 0, 0, 0))],
        out_shape=[jax.ShapeDtypeStruct((n_heads, 2, tq, _LANES), F32),
                   jax.ShapeDtypeStruct((n_heads, 2, t_dec, _LANES), F32)],
        compiler_params=_cparams(1),
        name="bias_tables",
    )(rel_bias.astype(F32))


def _prompt_attn_body(qi_ref, wk_ref, ki_ref, q_ref, k_ref, v_ref, sg_ref, tb_ref, o_ref,
                      wt_ref, sc_ref, keys_ref, madd_ref, m_ref, l_ref, acc_ref,
                      *, tq, topk, idx_heads, n_kv, rep):
    i = pl.program_id(1)
    n_ch = i + 1
    hd = q_ref.shape[-1]
    rows = rep * tq
    nb = tq // _LANES
    hgrp = 2
    kc2 = 2 * tq

    def chunk(c):
        return pl.ds(pl.multiple_of(c * tq, tq), tq)

    def causal(c):
        key_pos = lax.broadcasted_iota(I32, (tq, tq), 0) + c * tq
        return key_pos <= lax.broadcasted_iota(I32, (tq, tq), 1) + i * tq

    wt_ref[...] = wk_ref[...].T

    def score_body(c2, carry):
        ks = pl.ds(pl.multiple_of(c2 * kc2, kc2), kc2)
        keys = ki_ref[ks, :]
        a = jnp.zeros((kc2, tq), F32)
        for hg in range(idx_heads // hgrp):
            qi2 = qi_ref[hg * hgrp:(hg + 1) * hgrp].reshape(hgrp * tq, -1)
            s = _nt_dot(keys, qi2)
            for r in range(hgrp):
                h = hg * hgrp + r
                a = a + jnp.maximum(s[:, r * tq:(r + 1) * tq], 0.0) * wt_ref[h:h + 1, :]
        sc_ref[ks, :] = a
        return carry

    lax.fori_loop(0, (n_ch + 1) // 2, score_body, 0)

    def key_body(c, carry):
        keys_ref[chunk(c), :] = jnp.where(causal(c), _sort_key(sc_ref[chunk(c), :]), _INT_MIN)
        return carry

    lax.fori_loop(0, n_ch, key_body, 0)

    def count(pred):
        def hits(c):
            hit = pred(keys_ref[chunk(c), :]).astype(F32)
            return jnp.sum(hit.reshape(tq // 8, 8, tq), axis=0)

        def pair_body(c2, acc):
            return acc[0] + hits(2 * c2), acc[1] + hits(2 * c2 + 1)

        zero = jnp.zeros((8, tq), F32)
        even, odd = lax.fori_loop(0, n_ch // 2, pair_body, (zero, zero))
        last = lax.cond(n_ch % 2 == 1, lambda: hits(n_ch - 1), lambda: zero)
        return jnp.sum(even + odd + last, axis=0, keepdims=True)

    def radix_step(it, thr):
        cand = thr + jnp.left_shift(jnp.int32(1), 31 - it)
        return jnp.where(count(lambda k: k >= cand) >= topk, cand, thr)

    thr = lax.fori_loop(0, 32, radix_step, jnp.full((1, tq), _INT_MIN, I32))
    need = topk - count(lambda k: k > thr)
    n_eq = count(lambda k: k == thr)
    tied = jnp.max(jnp.where((n_eq > need) & (thr != _INT_MIN), 1.0, 0.0)) > 0.0

    def madd_body(c, carry):
        keep = (keys_ref[chunk(c), :] >= thr) & causal(c)
        madd_ref[c] = jnp.where(keep, 0.0, _NEG).T
        return carry

    lax.fori_loop(0, n_ch, madd_body, 0)

    @pl.when(tied)
    def _():
        tri = (lax.broadcasted_iota(I32, (tq, tq), 1) <= lax.broadcasted_iota(I32, (tq, tq), 0)).astype(BF16)

        def tie_body(c, run):
            kc = keys_ref[chunk(c), :]
            eq = kc == thr
            pre = jnp.dot(tri, eq.astype(F32).astype(BF16), preferred_element_type=F32) + run
            keep = ((kc > thr) | (eq & (pre <= need))) & causal(c)
            madd_ref[c] = jnp.where(keep, 0.0, _NEG).T
            return run + jnp.sum(eq.astype(F32), axis=0, keepdims=True)

        lax.fori_loop(0, n_ch, tie_body, jnp.zeros((1, tq), F32))

    m_ref[...] = jnp.full(m_ref.shape, _NEG, F32)
    l_ref[...] = jnp.zeros(l_ref.shape, F32)
    acc_ref[...] = jnp.zeros(acc_ref.shape, F32)

    def attend(c, near):
        madd = madd_ref[c]
        for g in range(n_kv):
            q4 = q_ref[g * rep:(g + 1) * rep].reshape(rows, hd)
            s = _nt_dot(q4, k_ref[g, chunk(c), :])
            slabs = []
            for r in range(rep):
                for a in range(nb):
                    rs = slice(r * tq + a * _LANES, r * tq + (a + 1) * _LANES)
                    x = s[rs, :] + madd[a * _LANES:(a + 1) * _LANES, :]
                    if near is not None:
                        cols = []
                        for b in range(nb):
                            xb = x[:, b * _LANES:(b + 1) * _LANES]
                            kind = near * nb + a - b
                            if kind in (0, 1):
                                xb = xb + tb_ref[g * rep + r, kind]
                            cols.append(xb)
                        x = jnp.concatenate(cols, axis=1)
                    slabs.append(x)
            x = jnp.concatenate(slabs, axis=0)
            xm = x[:, 0:_LANES]
            for b in range(1, nb):
                xm = jnp.maximum(xm, x[:, b * _LANES:(b + 1) * _LANES])
            m_old = m_ref[g]
            m_new = jnp.maximum(m_old, jnp.max(xm, axis=-1, keepdims=True))
            alpha = jnp.exp2(m_old - m_new)
            p = jnp.exp2(x - jnp.concatenate([m_new] * nb, axis=1))
            ps = p[:, 0:_LANES]
            for b in range(1, nb):
                ps = ps + p[:, b * _LANES:(b + 1) * _LANES]
            l_ref[g] = alpha * l_ref[g] + ps
            acc_ref[g] = alpha * acc_ref[g] + jnp.dot(p.astype(_MXU_DTYPE), v_ref[g, chunk(c), :],
                                                      preferred_element_type=F32)
            m_ref[g] = m_new

    attend(i, 0)

    @pl.when(i >= 1)
    def _():
        attend(i - 1, 1)

    def far_body(c, carry):
        attend(c, None)
        return carry

    lax.fori_loop(0, jnp.maximum(i - 1, 0), far_body, 0)

    for g in range(n_kv):
        o = acc_ref[g] / jnp.sum(l_ref[g], axis=-1, keepdims=True)
        o = o * sg_ref[g * rep:(g + 1) * rep].reshape(rows, hd).astype(F32)
        o_ref[g * rep:(g + 1) * rep] = o.reshape(rep, tq, hd).astype(o_ref.dtype)


def _prompt_attention(qi_hm, wk, ki_bf, q_hm, k_hm, v_hm, sg_hm, tb_p, *, batch, seq, tq):
    idx_heads, m, idx_dim = qi_hm.shape
    n_heads, _, hd = q_hm.shape
    n_kv = k_hm.shape[0]
    rep = n_heads // n_kv
    n_q = seq // tq
    topk = min(_TOPK_MAX, seq // 4)
    qrow = lambda b, i: (0, b * n_q + i, 0)
    body = functools.partial(_prompt_attn_body, tq=tq, topk=topk, idx_heads=idx_heads, n_kv=n_kv, rep=rep)
    return pl.pallas_call(
        body,
        grid=(batch, n_q),
        in_specs=[pl.BlockSpec((idx_heads, tq, idx_dim), qrow),
                  pl.BlockSpec((tq, _LANES), lambda b, i: (b * n_q + i, 0)),
                  pl.BlockSpec((seq, idx_dim), lambda b, i: (b, 0)),
                  pl.BlockSpec((n_heads, tq, hd), qrow),
                  pl.BlockSpec((n_kv, seq, hd), lambda b, i: (0, b, 0)),
                  pl.BlockSpec((n_kv, seq, hd), lambda b, i: (0, b, 0)),
                  pl.BlockSpec((n_heads, tq, hd), qrow),
                  pl.BlockSpec((n_heads, 2, _LANES, _LANES), lambda b, i: (0, 0, 0, 0))],
        out_specs=pl.BlockSpec((n_heads, tq, hd), qrow),
        out_shape=jax.ShapeDtypeStruct((n_heads, m, hd), _MXU_DTYPE),
        scratch_shapes=[pltpu.VMEM((_LANES, tq), F32),
                        pltpu.VMEM((seq, tq), F32),
                        pltpu.VMEM((seq, tq), I32),
                        pltpu.VMEM((n_q, tq, tq), F32),
                        pltpu.VMEM((n_kv, rep * tq, _LANES), F32),
                        pltpu.VMEM((n_kv, rep * tq, _LANES), F32),
                        pltpu.VMEM((n_kv, rep * tq, hd), F32)],
        compiler_params=_cparams(2),
        name="prompt_attention",
    )(qi_hm, wk, ki_bf, q_hm, k_hm, v_hm, sg_hm, tb_p)


def _page_copies(pt_ref, cache_ref, buf_ref, sem_ref, b, first_page, n_copy, slot, page):
    return [pltpu.make_async_copy(cache_ref.at[pt_ref[b, first_page + p]],
                                  buf_ref.at[slot, pl.ds(p * page, page)],
                                  sem_ref.at[slot]) for p in range(n_copy)]


def _sample_scores_body(pt_ref, qi_ref, wk_ref, cache_ref, o_ref, buf_ref, wb_ref, sem_ref,
                        *, n_pages, page, idx_heads, t_dec):
    b = pl.program_id(0)
    nb = pl.num_programs(0)
    slot = b % 2
    copies = functools.partial(_page_copies, pt_ref, cache_ref, buf_ref, sem_ref,
                               first_page=0, n_copy=n_pages, page=page)

    @pl.when(b == 0)
    def _():
        for c in copies(b=0, slot=0):
            c.start()

    @pl.when(b + 1 < nb)
    def _():
        for c in copies(b=b + 1, slot=1 - slot):
            c.start()

    qi = qi_ref[...].reshape(idx_heads * t_dec, -1).astype(_MXU_DTYPE)
    w_blk = wk_ref[:, 0:_LANES]
    for h in range(idx_heads):
        wb_ref[h * t_dec:(h + 1) * t_dec, :] = jnp.broadcast_to(w_blk[:, h:h + 1], (t_dec, _LANES))

    def page_scores(keys):
        s = _nt_dot(qi, keys.astype(_MXU_DTYPE))
        return jnp.sum((jnp.maximum(s, 0.0) * wb_ref[...]).reshape(idx_heads, t_dec, page), axis=0)

    past = n_pages * page
    ki_new = jnp.concatenate([wk_ref[:, _LANES:], jnp.zeros((page - t_dec, wk_ref.shape[1] - _LANES), F32)], axis=0)
    row = lax.broadcasted_iota(I32, (t_dec, page), 0)
    col = lax.broadcasted_iota(I32, (t_dec, page), 1)
    o_ref[0, :, past:past + page] = jnp.where(col <= row, page_scores(ki_new), -jnp.inf)

    for c in copies(b=b, slot=slot):
        c.wait()
    for p in range(n_pages):
        o_ref[0, :, p * page:(p + 1) * page] = page_scores(buf_ref[slot, p * page:(p + 1) * page, :])


def _sample_scores(page_table, qi_s, wk_s, cache_idx, *, t_dec):
    bd, n_pages = page_table.shape
    _, page, idx_dim = cache_idx.shape
    idx_heads = qi_s.shape[0]
    past = n_pages * page
    grid_spec = pltpu.PrefetchScalarGridSpec(
        num_scalar_prefetch=1,
        grid=(bd,),
        in_specs=[pl.BlockSpec((idx_heads, t_dec, idx_dim), lambda b, pt: (0, b, 0)),
                  pl.BlockSpec((t_dec, wk_s.shape[1]), lambda b, pt: (b, 0)),
                  pl.BlockSpec(memory_space=pl.ANY)],
        out_specs=pl.BlockSpec((1, t_dec, past + page), lambda b, pt: (b, 0, 0)),
        scratch_shapes=[pltpu.VMEM((2, past, idx_dim), F32),
                        pltpu.VMEM((idx_heads * t_dec, _LANES), F32),
                        pltpu.SemaphoreType.DMA((2,))],
    )
    return pl.pallas_call(
        functools.partial(_sample_scores_body, n_pages=n_pages, page=page, idx_heads=idx_heads, t_dec=t_dec),
        grid_spec=grid_spec,
        out_shape=jax.ShapeDtypeStruct((bd, t_dec, past + page), F32),
        compiler_params=_cparams(1),
        name="sample_scores",
    )(page_table, qi_s, wk_s, cache_idx)


def _sample_select_body(sc_ref, o_ref, keys_ref, *, topk, past):
    bd, t_dec, n_cols = sc_ref.shape

    def valid(c0, width):
        col = lax.broadcasted_iota(I32, (bd, t_dec, width), 2) + c0
        return col <= past + lax.broadcasted_iota(I32, (bd, t_dec, width), 1)

    keys_ref[...] = jnp.where(valid(0, n_cols), _sort_key(sc_ref[...]), _INT_MIN)

    def step(it, thr):
        cand = thr + jnp.left_shift(jnp.int32(1), 31 - it)
        cnt = jnp.sum((keys_ref[...] >= cand).astype(F32), axis=-1, keepdims=True)
        return jnp.where(cnt >= topk, cand, thr)

    thr = lax.fori_loop(0, 32, step, jnp.full((bd, t_dec, 1), _INT_MIN, I32))
    keys = keys_ref[...]
    cnt_gt = jnp.sum((keys > thr).astype(F32), axis=-1, keepdims=True)
    cnt_eq = jnp.sum((keys == thr).astype(F32), axis=-1, keepdims=True)
    need = topk - cnt_gt
    o_ref[...] = ((keys >= thr) & valid(0, n_cols)).astype(F32)
    tied = jnp.max(jnp.where((cnt_eq > need) & (thr != _INT_MIN), 1.0, 0.0)) > 0.0

    @pl.when(tied)
    def _():
        tri = (lax.broadcasted_iota(I32, (_LANES, _LANES), 0)
               <= lax.broadcasted_iota(I32, (_LANES, _LANES), 1)).astype(BF16)
        run = jnp.zeros((bd, t_dec, 1), F32)
        for c0 in range(0, n_cols, _LANES):
            kc = keys_ref[:, :, c0:c0 + _LANES]
            eq = kc == thr
            eq2 = eq.astype(F32).reshape(bd * t_dec, _LANES).astype(BF16)
            pre = jnp.dot(eq2, tri, preferred_element_type=F32).reshape(bd, t_dec, _LANES) + run
            run = run + jnp.sum(eq.astype(F32), axis=-1, keepdims=True)
            keep = (kc > thr) | (eq & (pre <= need))
            o_ref[:, :, c0:c0 + _LANES] = (keep & valid(c0, _LANES)).astype(F32)


def _sample_select(scores, *, topk, past):
    return pl.pallas_call(
        functools.partial(_sample_select_body, topk=topk, past=past),
        out_shape=jax.ShapeDtypeStruct(scores.shape, F32),
        scratch_shapes=[pltpu.VMEM(scores.shape, I32)],
        compiler_params=pltpu.CompilerParams(vmem_limit_bytes=_VMEM_LIMIT),
        name="sample_select",
    )(scores)


def _sample_attn_body(pt_ref, rb_ref, q_ref, mask_ref, mtail_ref, kn_ref, vn_ref, sg_ref, ts_ref,
                      ck_ref, cv_ref, o_ref, kbuf_ref, vbuf_ref, m_ref, l_ref, acc_ref, ksem_ref, vsem_ref,
                      *, n_chunks, cpages, page, n_kv, rep, t_dec, n_buckets):
    b = pl.program_id(0)
    c = pl.program_id(1)
    step = b * n_chunks + c
    n_steps = pl.num_programs(0) * n_chunks
    slot = step % 2
    hd = q_ref.shape[-1]
    rows = rep * t_dec
    ck = cpages * page

    prow = page * n_kv

    def copies(bb, cc, sl):
        return (_page_copies(pt_ref, ck_ref, kbuf_ref, ksem_ref, bb, cc * cpages, cpages, sl, prow)
                + _page_copies(pt_ref, cv_ref, vbuf_ref, vsem_ref, bb, cc * cpages, cpages, sl, prow))

    def kv_head(buf_ref, g):
        return buf_ref[slot, pl.ds(g, ck, stride=n_kv), :]

    @pl.when(step == 0)
    def _():
        for cp in copies(0, 0, 0):
            cp.start()

    @pl.when(step + 1 < n_steps)
    def _():
        nxt = step + 1
        for cp in copies(nxt // n_chunks, nxt % n_chunks, 1 - slot):
            cp.start()

    @pl.when(c == 0)
    def _():
        m_ref[...] = jnp.full(m_ref.shape, -jnp.inf, F32)
        l_ref[...] = jnp.zeros(l_ref.shape, F32)
        acc_ref[...] = jnp.zeros(acc_ref.shape, F32)

    def update(g, logits, mask, values):
        n = logits.shape[-1]
        x = jnp.where(jnp.broadcast_to(mask[None], (rep, t_dec, n)).reshape(rows, n) > 0.5, logits, -jnp.inf)
        rs = slice(g * rows, (g + 1) * rows)
        m_old = m_ref[rs, :]
        m_new = jnp.maximum(m_old, jnp.max(x, axis=-1, keepdims=True))
        m_safe = jnp.where(m_new == -jnp.inf, 0.0, m_new)
        alpha = jnp.exp(m_old - m_safe)
        p = jnp.exp(x - m_safe[:, 0:1])
        l_ref[rs, :] = alpha * l_ref[rs, :] + jnp.sum(p, axis=-1, keepdims=True)
        acc_ref[rs, :] = alpha * acc_ref[rs, :] + jnp.dot(p.astype(_MXU_DTYPE), values.astype(_MXU_DTYPE),
                                                          preferred_element_type=F32)
        m_ref[rs, :] = m_new

    def head_bias(g, kind):
        return ts_ref[g * rep:(g + 1) * rep, kind].reshape(rows, _LANES)

    def far_bias(g):
        r_idx = lax.broadcasted_iota(I32, (rows, 1), 0) // t_dec
        out = jnp.zeros((rows, 1), F32)
        for r in range(rep):
            out = jnp.where(r_idx == r, rb_ref[n_buckets - 1, g * rep + r], out)
        return out

    for cp in copies(b, c, slot):
        cp.wait()

    is_last = (c == n_chunks - 1).astype(F32)
    for g in range(n_kv):
        qg = q_ref[g * rep:(g + 1) * rep].reshape(rows, hd).astype(_MXU_DTYPE)
        kc = kv_head(kbuf_ref, g)
        far = far_bias(g)
        lg = _nt_dot(qg, kc.astype(_MXU_DTYPE)) + far
        near = lg[:, ck - page:] + (head_bias(g, 0) - far) * is_last
        lg = jnp.concatenate([lg[:, :ck - page], near], axis=1)
        update(g, lg, mask_ref[...], kv_head(vbuf_ref, g))

    @pl.when(c == n_chunks - 1)
    def _():
        zpad = jnp.zeros((page - t_dec, hd), F32)
        for g in range(n_kv):
            qg = q_ref[g * rep:(g + 1) * rep].reshape(rows, hd).astype(_MXU_DTYPE)
            kn = jnp.concatenate([kn_ref[pl.ds(g, t_dec, stride=n_kv), :], zpad], axis=0)
            vn = jnp.concatenate([vn_ref[pl.ds(g, t_dec, stride=n_kv), :], zpad], axis=0)
            lg = _nt_dot(qg, kn.astype(_MXU_DTYPE)) + head_bias(g, 1)
            update(g, lg, mtail_ref[...], vn)
            rs = slice(g * rows, (g + 1) * rows)
            o = acc_ref[rs, :] / l_ref[rs, :] * sg_ref[g * rep:(g + 1) * rep].reshape(rows, hd)
            o_ref[g * rep:(g + 1) * rep] = o.reshape(rep, t_dec, hd)


def _sample_attention(page_table, rel_bias, q_s, mask, kn, vn, sg_s, ts, cache_k, cache_v, *, t_dec, cpages, page):
    bd, n_pages = page_table.shape
    n_heads, _, hd = q_s.shape
    n_kv = cache_k.shape[1] // page
    rep = n_heads // n_kv
    n_chunks = n_pages // cpages
    ck = cpages * page
    past = n_pages * page
    hrow = lambda b, c, pt: (0, b, 0)
    grid_spec = pltpu.PrefetchScalarGridSpec(
        num_scalar_prefetch=1,
        grid=(bd, n_chunks),
        in_specs=[pl.BlockSpec(memory_space=pltpu.SMEM),
                  pl.BlockSpec((n_heads, t_dec, hd), hrow),
                  pl.BlockSpec((t_dec, ck), lambda b, c, pt: (b, c)),
                  pl.BlockSpec((t_dec, page), lambda b, c, pt: (b, past // page)),
                  pl.BlockSpec((t_dec * n_kv, hd), lambda b, c, pt: (b, 0)),
                  pl.BlockSpec((t_dec * n_kv, hd), lambda b, c, pt: (b, 0)),
                  pl.BlockSpec((n_heads, t_dec, hd), hrow),
                  pl.BlockSpec((n_heads, 2, t_dec, _LANES), lambda b, c, pt: (0, 0, 0, 0)),
                  pl.BlockSpec(memory_space=pl.ANY),
                  pl.BlockSpec(memory_space=pl.ANY)],
        out_specs=pl.BlockSpec((n_heads, t_dec, hd), hrow),
        scratch_shapes=[pltpu.VMEM((2, ck * n_kv, hd), F32),
                        pltpu.VMEM((2, ck * n_kv, hd), F32),
                        pltpu.VMEM((n_heads * t_dec, _LANES), F32),
                        pltpu.VMEM((n_heads * t_dec, _LANES), F32),
                        pltpu.VMEM((n_heads * t_dec, hd), F32),
                        pltpu.SemaphoreType.DMA((2,)),
                        pltpu.SemaphoreType.DMA((2,))],
    )
    return pl.pallas_call(
        functools.partial(_sample_attn_body, n_chunks=n_chunks, cpages=cpages, page=page, n_kv=n_kv,
                          rep=rep, t_dec=t_dec, n_buckets=rel_bias.shape[0]),
        grid_spec=grid_spec,
        out_shape=jax.ShapeDtypeStruct((n_heads, bd * t_dec, hd), F32),
        compiler_params=_cparams(2),
        name="sample_attention",
    )(page_table, rel_bias.astype(F32), q_s, mask, mask, kn, vn, sg_s, ts, cache_k, cache_v)


def _pool_body(u_ref, halo_ref, gb_ref, wp_ref, ps_ref, o_ref, ext_ref, *, tm, pos_base, own_halo, pg):
    i = pl.program_id(1)
    bb = u_ref.shape[0]
    halo = halo_ref[...]
    if own_halo:
        halo = jnp.where(i > 0, halo, 0.0)
    ext_ref[:, 0:_POOL_HALO, :] = halo
    ext_ref[:, _POOL_HALO:, :] = u_ref[...]
    pos = pos_base + i * tm + lax.broadcasted_iota(I32, (1, tm, 1), 1)
    for g, win in enumerate(_POOL_WINDOWS):
        cols = slice(g * pg, (g + 1) * pg)
        tot = ext_ref[:, _POOL_HALO:, cols]
        for k in range(1, win):
            tot = tot + ext_ref[:, _POOL_HALO - k:_POOL_HALO - k + tm, cols]
        cnt = jnp.minimum(pos + 1, win).astype(F32)
        d = tot / cnt - ext_ref[:, _POOL_HALO:, cols]
        y = jnp.dot(d.reshape(bb * tm, pg).astype(_MXU_DTYPE), wp_ref[g], preferred_element_type=F32)
        y = y.reshape(bb, tm, pg) * ps_ref[:, cols] * gb_ref[:, :, cols].astype(F32)
        o_ref[:, :, cols] = y.astype(o_ref.dtype)


def _pool(u, halo, gb, wp, ps, *, bb, tm, pos_base, own_halo, out_dtype):
    n_seq, rows, pw = u.shape
    pg = pw // len(_POOL_WINDOWS)
    if own_halo:
        hb = tm // _POOL_HALO
        halo_spec = pl.BlockSpec((bb, _POOL_HALO, pw), lambda b, i: (b, jnp.maximum(i * hb - 1, 0), 0))
    else:
        halo_spec = pl.BlockSpec((bb, _POOL_HALO, pw), lambda b, i: (b, 0, 0))
    return pl.pallas_call(
        functools.partial(_pool_body, tm=tm, pos_base=pos_base, own_halo=own_halo, pg=pg),
        grid=(n_seq // bb, rows // tm),
        in_specs=[pl.BlockSpec((bb, tm, pw), lambda b, i: (b, i, 0)),
                  halo_spec,
                  pl.BlockSpec((bb, tm, pw), lambda b, i: (b, i, 0)),
                  pl.BlockSpec(wp.shape, lambda b, i: (0, 0, 0)),
                  pl.BlockSpec((1, pw), lambda b, i: (0, 0))],
        out_specs=pl.BlockSpec((bb, tm, pw), lambda b, i: (b, i, 0)),
        out_shape=jax.ShapeDtypeStruct((n_seq, rows, pw), out_dtype),
        scratch_shapes=[pltpu.VMEM((bb, _POOL_HALO + tm, pw), F32)],
        compiler_params=_cparams(2),
        name="pool",
    )(u, halo, gb, wp, ps)


def _merge_body(ma_ref, mb_ref, w_ref, x_ref, g_ref, o_ref, mix_ref, *, n_j, tn):
    j = pl.program_id(1)
    n_heads, _, hd = ma_ref.shape

    @pl.when(j == 0)
    def _():
        for h in range(n_heads):
            mix_ref[:, h * hd:(h + 1) * hd] = ma_ref[h].astype(mix_ref.dtype)
        mix_ref[:, n_heads * hd:] = mb_ref[...].astype(mix_ref.dtype)

    hv = x_ref[...] + jnp.dot(mix_ref[...], w_ref[...], preferred_element_type=F32)
    for jj in range(n_j):
        @pl.when(j == jj)
        def _(jj=jj):
            o_ref[:, jj * tn:(jj + 1) * tn] = hv

    @pl.when(j == n_j - 1)
    def _():
        ssq = jnp.zeros((o_ref.shape[0], 1), F32)
        for jj in range(n_j):
            hj = o_ref[:, jj * tn:(jj + 1) * tn]
            ssq = ssq + jnp.sum(hj * hj, axis=-1, keepdims=True)
        inv = lax.rsqrt(ssq / (n_j * tn) + _RMS_EPS)
        for jj in range(n_j):
            cols = slice(jj * tn, (jj + 1) * tn)
            o_ref[:, cols] = o_ref[:, cols] * inv * g_ref[:, cols]


def _merge(mix_a, mix_b, w_out, x2d, final_g, *, tm, tn):
    n_heads, m, hd = mix_a.shape
    pw = mix_b.shape[1]
    mixw, d = w_out.shape
    n_j = d // tn
    return pl.pallas_call(
        functools.partial(_merge_body, n_j=n_j, tn=tn),
        grid=(m // tm, n_j),
        in_specs=[pl.BlockSpec((n_heads, tm, hd), lambda i, j: (0, i, 0)),
                  pl.BlockSpec((tm, pw), lambda i, j: (i, 0)),
                  pl.BlockSpec((mixw, tn), lambda i, j: (0, j)),
                  pl.BlockSpec((tm, tn), lambda i, j: (i, j)),
                  pl.BlockSpec((1, d), lambda i, j: (0, 0))],
        out_specs=pl.BlockSpec((tm, d), lambda i, j: (i, 0)),
        out_shape=jax.ShapeDtypeStruct((m, d), F32),
        scratch_shapes=[pltpu.VMEM((tm, mixw), _MXU_DTYPE)],
        compiler_params=_cparams(2),
        name="merge",
    )(mix_a, mix_b, w_out, x2d, final_g.reshape(1, d).astype(F32))


def kernel(x_prompt, x_sample, cache_k, cache_v, cache_idx_k, state_pool, page_table,
           rel_bias, ln_g, w_in, w_pool, pool_scale, w_out, final_g):
    depth = w_in.shape[0]
    assert depth == 1, "single-layer step only"
    batch, seq, d_model = x_prompt.shape
    bd, t_dec, _ = x_sample.shape
    _, n_phys, page, n_kv, hd = cache_k.shape
    idx_dim = cache_idx_k.shape[-1]
    n_buckets, n_heads = rel_bias.shape
    pw = pool_scale.shape[-1]
    n_pg = w_pool.shape[1]
    attn_w, kv_w = n_heads * hd, n_kv * hd
    in_w = w_in.shape[-1]
    idx_heads = (in_w - 2 * attn_w - 2 * kv_w - idx_dim - 2 * pw) // (idx_dim + 1)
    n_pages = page_table.shape[1]
    past = n_pages * page
    mp, ms = batch * seq, bd * t_dec
    tq = _PROMPT_Q
    assert hd == _LANES and idx_dim == _LANES and idx_heads <= _LANES and n_pg == len(_POOL_WINDOWS)
    assert page >= _MAX_DISTANCE and _LANES >= _MAX_DISTANCE and seq % (2 * tq) == 0 and t_dec <= _POOL_HALO - 1
    assert state_pool.shape[2] == _POOL_HALO - 1

    o_q, o_k, o_v = 0, attn_w, attn_w + kv_w
    o_ga = o_v + kv_w
    o_qi = o_ga + attn_w
    o_wi = o_qi + idx_heads * idx_dim
    o_ki = o_wi + idx_heads
    o_u = o_ki + idx_dim
    o_gb = o_u + pw
    wt = jnp.transpose(w_in[0])

    xn_p = _rmsnorm(x_prompt.reshape(mp, d_model), ln_g[0], _MXU_DTYPE)
    xn_s = _rmsnorm(x_sample.reshape(ms, d_model), ln_g[0], _MXU_DTYPE)

    tm = min(_PROJ_ROWS, mp)
    tn = lambda n: min(_PROJ_COLS, n)
    rm = lambda n, t, dt, s: _row_major(ms if s else mp, n, tm, t, dt, s)
    hm = lambda n, t, dt, s: _head_major(ms if s else mp, n, tm, t, dt, s)

    def proj(lo, hi, t, **kw):
        assert (hi - lo) % t == 0 and lo % 8 == 0
        return _proj_call(xn_p, xn_s, wt, [(lo, t)], (hi - lo) // t, tm, **kw)

    def scaled_heads(scale):
        def epi(acc, refs):
            _store_heads(refs[0], acc * scale)
        return epi

    def silu_heads(acc, refs):
        _store_heads(refs[0], _silu(acc))

    def silu_rows(acc, refs):
        refs[0][...] = _silu(acc).astype(refs[0].dtype)

    def plain_rows(acc, refs):
        refs[0][...] = acc.astype(refs[0].dtype)

    def kv_rows(acc, ref):
        for g in range(n_kv):
            ref[pl.ds(g, acc.shape[0], stride=n_kv), :] = acc[:, g * hd:(g + 1) * hd]

    def kv_p(acc, refs):
        kv_rows(acc, refs[0])
        _store_heads(refs[1], acc)

    def kv_s(acc, refs):
        kv_rows(acc, refs[0])

    def wk_p(acc, refs):
        refs[0][...] = jnp.concatenate([acc[:, :_LANES] * idx_heads ** -0.5, acc[:, _LANES:]], axis=1)
        refs[1][...] = acc[:, _LANES:].astype(refs[1].dtype)

    def wk_s(acc, refs):
        refs[0][...] = jnp.concatenate([acc[:, :_LANES] * idx_heads ** -0.5, acc[:, _LANES:]], axis=1)

    t = tn(attn_w)
    q_hm, q_s = proj(o_q, o_k, t, outs_p=[hm(attn_w, t, _MXU_DTYPE, False)], outs_s=[hm(attn_w, t, F32, True)],
                     epi_p=scaled_heads(hd ** -0.5 * _LOG2E), epi_s=scaled_heads(hd ** -0.5), name="proj_q")
    kv_outs = dict(outs_p=[((mp * n_kv, hd), F32, (tm * n_kv, hd), lambda j, i: (i, 0)),
                           ((n_kv, mp, hd), _MXU_DTYPE, (n_kv, tm, hd), lambda j, i: (0, i, 0))],
                   outs_s=[((ms * n_kv, hd), F32, (ms * n_kv, hd), lambda j, i: (0, 0))], epi_p=kv_p, epi_s=kv_s)
    k_p, k_hm, k_s = proj(o_k, o_v, kv_w, name="proj_k", **kv_outs)
    v_p, v_hm, v_s = proj(o_v, o_ga, kv_w, name="proj_v", **kv_outs)
    sga_hm, sga_s = proj(o_ga, o_qi, t, outs_p=[hm(attn_w, t, _MXU_DTYPE, False)],
                         outs_s=[hm(attn_w, t, F32, True)], epi_p=silu_heads, epi_s=silu_heads, name="proj_ga")
    t = tn(idx_heads * idx_dim)
    qi_hm, qi_s = proj(o_qi, o_wi, t, outs_p=[hm(idx_heads * idx_dim, t, _MXU_DTYPE, False)],
                       outs_s=[hm(idx_heads * idx_dim, t, F32, True)],
                       epi_p=scaled_heads(idx_dim ** -0.5), epi_s=scaled_heads(idx_dim ** -0.5), name="proj_qi")
    t = 2 * _LANES
    wk_pr, ki_bf, wk_sm = _proj_call(
        xn_p, xn_s, wt, [(o_wi, _LANES), (o_ki, idx_dim)], 1, tm,
        outs_p=[((mp, t), F32, (tm, t), lambda j, i: (i, 0)), ((mp, idx_dim), _MXU_DTYPE, (tm, idx_dim), lambda j, i: (i, 0))],
        outs_s=[((ms, t), F32, (ms, t), lambda j, i: (0, 0))],
        epi_p=wk_p, epi_s=wk_s, name="proj_wk")
    t = tn(pw)
    u_p, u_s = proj(o_u, o_gb, t, outs_p=[rm(pw, t, F32, False)], outs_s=[rm(pw, t, F32, True)],
                    epi_p=plain_rows, epi_s=plain_rows, name="proj_u")
    sgb_p, sgb_s = proj(o_gb, in_w, t, outs_p=[rm(pw, t, _MXU_DTYPE, False)], outs_s=[rm(pw, t, F32, True)],
                        epi_p=silu_rows, epi_s=silu_rows, name="proj_gb")

    tb_p, tb_s = _bias_tables(rel_bias, _LANES, t_dec, page)

    mixa_p = _prompt_attention(qi_hm, wk_pr, ki_bf, q_hm, k_hm, v_hm, sga_hm, tb_p, batch=batch, seq=seq, tq=tq)

    scores = _sample_scores(page_table, qi_s, wk_sm, cache_idx_k[0], t_dec=t_dec)
    topk_s = min(_TOPK_MAX, (past + t_dec) // 4)
    mask_s = _sample_select(scores, topk=topk_s, past=past).reshape(ms, past + page)
    cpages = max(1, min(_PAGES_PER_STEP, n_pages // 2))
    assert n_pages % cpages == 0
    mixa_s = _sample_attention(page_table, rel_bias, q_s, mask_s, k_s, v_s, sga_s, tb_s,
                               cache_k[0].reshape(n_phys, page * n_kv, hd), cache_v[0].reshape(n_phys, page * n_kv, hd),
                               t_dec=t_dec, cpages=cpages, page=page)

    wp = w_pool[0].astype(_MXU_DTYPE)
    ps = pool_scale[0].reshape(1, pw).astype(F32)
    u_p3 = u_p.reshape(batch, seq, pw)
    mixb_p = _pool(u_p3, u_p3, sgb_p.reshape(batch, seq, pw), wp, ps, bb=1, tm=min(_POOL_ROWS, seq), pos_base=0,
                   own_halo=True, out_dtype=_MXU_DTYPE).reshape(mp, pw)
    halo_s = jnp.concatenate([jnp.zeros((bd, 1, pw), F32), state_pool[0].astype(F32)], axis=1)
    mixb_s = _pool(u_s.reshape(bd, t_dec, pw), halo_s, sgb_s.reshape(bd, t_dec, pw), wp, ps, bb=bd, tm=t_dec,
                   pos_base=past, own_halo=False, out_dtype=F32).reshape(ms, pw)

    w_o = w_out[0].astype(_MXU_DTYPE)
    tn_o = min(_MERGE_COLS, d_model)
    y_p = _merge(mixa_p, mixb_p, w_o, x_prompt.reshape(mp, d_model), final_g, tm=min(_MERGE_ROWS, mp), tn=tn_o)
    y_s = _merge(mixa_s, mixb_s, w_o, x_sample.reshape(ms, d_model), final_g, tm=ms, tn=tn_o)

    n_state = state_pool.shape[2]
    u_full_s = jnp.concatenate([state_pool[0].astype(F32), u_s.reshape(bd, t_dec, pw)], axis=1)
    return (y_p.reshape(batch, seq, d_model), y_s.reshape(bd, t_dec, d_model),
            k_p.reshape(1, batch, seq, n_kv, hd), v_p.reshape(1, batch, seq, n_kv, hd),
            wk_pr[:, _LANES:].reshape(1, batch, seq, idx_dim),
            u_p.reshape(batch, seq, pw)[:, seq - n_state:][None],
            k_s.reshape(1, bd, t_dec, n_kv, hd), v_s.reshape(1, bd, t_dec, n_kv, hd),
            wk_sm[:, _LANES:].reshape(1, bd, t_dec, idx_dim),
            u_full_s[:, t_dec:][None])
```

```python
import functools
import math

import jax
import jax.numpy as jnp
from jax import lax
from jax.experimental import pallas as pl
from jax.experimental.pallas import tpu as pltpu

F32 = jnp.float32
BF16 = jnp.bfloat16
I32 = jnp.int32

_TOPK_MAX = 256
_POOL_WINDOWS = (2, 4, 8, 16)
_POOL_HALO = 16
_MAX_DISTANCE = 128
_RMS_EPS = 1e-6
_NEG = -1e30
_LOG2E = math.log2(math.e)
_INT_MIN = -(2 ** 31)
_LANES = 128
_SUBLANES = 8
_VMEM_LIMIT = 58 * 1024 * 1024
_MXU_DTYPE = BF16

_RMS_ROWS = 512
_PROJ_ROWS = 1024
_PROJ_ROWS_WIDE = 2048
_PROJ_COLS = 512
_PROMPT_Q = 2 * _LANES
_PAGES_PER_STEP = 32
_POOL_ROWS = 512
_MERGE_ROWS = 512
_MERGE_COLS = 1024


def _cparams(n_axes):
    return pltpu.CompilerParams(dimension_semantics=("arbitrary",) * n_axes,
                                vmem_limit_bytes=_VMEM_LIMIT)


def _nt_dot(a, b):
    return lax.dot_general(a, b, (((1,), (1,)), ((), ())), preferred_element_type=F32)


def _sort_key(x):
    bits = lax.bitcast_convert_type(x + 0.0, I32)
    return bits ^ ((bits >> 31) & 0x7FFFFFFF)


def _rmsnorm_body(x_ref, g_ref, o_ref):
    x = x_ref[...].astype(F32)
    var = jnp.mean(x * x, axis=-1, keepdims=True)
    o_ref[...] = (x * lax.rsqrt(var + _RMS_EPS) * g_ref[...]).astype(o_ref.dtype)


def _rmsnorm(x2d, g, out_dtype):
    m, d = x2d.shape
    tm = min(_RMS_ROWS, m)
    return pl.pallas_call(
        _rmsnorm_body,
        grid=(m // tm,),
        in_specs=[pl.BlockSpec((tm, d), lambda i: (i, 0)), pl.BlockSpec((1, d), lambda i: (0, 0))],
        out_specs=pl.BlockSpec((tm, d), lambda i: (i, 0)),
        out_shape=jax.ShapeDtypeStruct((m, d), out_dtype),
        compiler_params=_cparams(1),
        name="rmsnorm",
    )(x2d, g.reshape(1, d).astype(F32))


def _proj_body(xp_ref, xs_ref, wt_ref, *refs, w_rows, n_i, n_p, n_out, epi_p, epi_s):
    out_refs, wbuf_ref, wc_ref, sem_ref = refs[:n_out], refs[n_out], refs[n_out + 1], refs[n_out + 2]
    j, i = pl.program_id(0), pl.program_id(1)

    def w_copies(jj):
        out, r0 = [], 0
        for first, cnt in w_rows:
            src = wt_ref.at[pl.ds(pl.multiple_of(first + jj * cnt, _SUBLANES), cnt)]
            out.append(pltpu.make_async_copy(src, wbuf_ref.at[pl.ds(r0, cnt)], sem_ref.at[0]))
            r0 += cnt
        return out

    @pl.when((j == 0) & (i == 0))
    def _():
        for c in w_copies(0):
            c.start()

    @pl.when(i == 0)
    def _():
        for c in w_copies(j):
            c.wait()
        wc_ref[...] = wbuf_ref[...].astype(wc_ref.dtype)

    @pl.when((i == min(1, n_i - 1)) & (j + 1 < pl.num_programs(0)))
    def _():
        for c in w_copies(j + 1):
            c.start()

    epi_p(_nt_dot(xp_ref[...], wc_ref[...]), out_refs[:n_p])

    @pl.when(i == 0)
    def _():
        epi_s(_nt_dot(xs_ref[...], wc_ref[...]), out_refs[n_p:])


def _proj_call(xn_p, xn_s, wt, w_rows, n_tiles, tm, outs_p, outs_s, epi_p, epi_s, name):
    mp, d = xn_p.shape
    ms = xn_s.shape[0]
    assert mp % tm == 0 and all(first % _SUBLANES == 0 and cnt % _SUBLANES == 0 for first, cnt in w_rows)
    tn = sum(cnt for _, cnt in w_rows)
    outs = list(outs_p) + list(outs_s)
    return pl.pallas_call(
        functools.partial(_proj_body, w_rows=tuple(w_rows), n_i=mp // tm, n_p=len(outs_p), n_out=len(outs),
                          epi_p=epi_p, epi_s=epi_s),
        grid=(n_tiles, mp // tm),
        in_specs=[pl.BlockSpec((tm, d), lambda j, i: (i, 0)),
                  pl.BlockSpec((ms, d), lambda j, i: (0, 0), pipeline_mode=pl.Buffered(1)),
                  pl.BlockSpec(memory_space=pl.ANY)],
        out_specs=[pl.BlockSpec(o[2], o[3]) for o in outs],
        out_shape=[jax.ShapeDtypeStruct(o[0], o[1]) for o in outs],
        scratch_shapes=[pltpu.VMEM((tn, d), wt.dtype), pltpu.VMEM((tn, d), xn_p.dtype),
                        pltpu.SemaphoreType.DMA((1,))],
        compiler_params=_cparams(2),
        name=name,
    )(xn_p, xn_s, wt)


def _row_major(m, ncols, tm, tn, dtype, sample):
    if sample:
        return ((m, ncols), dtype, (m, tn), lambda j, i: (0, j))
    return ((m, ncols), dtype, (tm, tn), lambda j, i: (i, j))


def _head_major(m, ncols, tm, tn, dtype, sample):
    nh, hb = ncols // _LANES, tn // _LANES
    if sample:
        return ((nh, m, _LANES), dtype, (hb, m, _LANES), lambda j, i: (j, 0, 0))
    return ((nh, m, _LANES), dtype, (hb, tm, _LANES), lambda j, i: (j, i, 0))


def _store_heads(ref, val):
    for hh in range(ref.shape[0]):
        ref[hh] = val[:, hh * _LANES:(hh + 1) * _LANES].astype(ref.dtype)


def _silu(x):
    return x * jax.nn.sigmoid(x)


def _bias_tables_body(rb_ref, tp_ref, ts_ref, *, n_buckets, tq, t_dec, page):
    h = pl.program_id(0)
    max_exact = n_buckets // 2

    def table(dist):
        d = jnp.maximum(dist, 0)
        df = jnp.maximum(d.astype(F32), 1.0)
        large = max_exact + (jnp.log(df / max_exact) / math.log(_MAX_DISTANCE / max_exact)
                             * (n_buckets - max_exact)).astype(I32)
        bucket = jnp.where(d < max_exact, d, jnp.minimum(large, n_buckets - 1))
        out = jnp.zeros(dist.shape, F32)
        for b in range(n_buckets):
            out = jnp.where(bucket == b, rb_ref[b, h], out)
        return out

    rp = lax.broadcasted_iota(I32, (tq, _LANES), 0)
    cp = lax.broadcasted_iota(I32, (tq, _LANES), 1)
    far = rb_ref[n_buckets - 1, h]
    tp_ref[0, 0] = (table(rp - cp) - far) * _LOG2E
    tp_ref[0, 1] = (table(rp - cp + tq) - far) * _LOG2E
    rs = lax.broadcasted_iota(I32, (t_dec, _LANES), 0)
    cs = lax.broadcasted_iota(I32, (t_dec, _LANES), 1)
    ts_ref[0, 0] = table(rs - cs + page)
    ts_ref[0, 1] = table(rs - cs)


def _bias_tables(rel_bias, tq, t_dec, page):
    n_buckets, n_heads = rel_bias.shape
    return pl.pallas_call(
        functools.partial(_bias_tables_body, n_buckets=n_buckets, tq=tq, t_dec=t_dec, page=page),
        grid=(n_heads,),
        in_specs=[pl.BlockSpec(memory_space=pltpu.SMEM)],
        out_specs=[pl.BlockSpec((1, 2, tq, _LANES), lambda h: (h, 0, 0, 0)),
                   pl.BlockSpec((1, 2, t_dec, _LANES), lambda h: (h, 0, 0, 0))],
        out_shape=[jax.ShapeDtypeStruct((n_heads, 2, tq, _LANES), F32),
                   jax.ShapeDtypeStruct((n_heads, 2, t_dec, _LANES), F32)],
        compiler_params=_cparams(1),
        name="bias_tables",
    )(rel_bias.astype(F32))


def _prompt_attn_body(qi_ref, wk_ref, ki_ref, q_ref, k_ref, v_ref, sg_ref, tb_ref, o_ref,
                      wt_ref, sc_ref, keys_ref, madd_ref, m_ref, l_ref, acc_ref,
                      *, tq, topk, idx_heads, n_kv, rep):
    i = pl.program_id(1)
    n_ch = i + 1
    hd = q_ref.shape[-1]
    rows = rep * tq
    nb = tq // _LANES
    hgrp = 2
    kc2 = 2 * tq

    def chunk(c):
        return pl.ds(pl.multiple_of(c * tq, tq), tq)

    def causal(c):
        key_pos = lax.broadcasted_iota(I32, (tq, tq), 0) + c * tq
        return key_pos <= lax.broadcasted_iota(I32, (tq, tq), 1) + i * tq

    wt_ref[...] = wk_ref[...].T

    def score_body(c2, carry):
        ks = pl.ds(pl.multiple_of(c2 * kc2, kc2), kc2)
        keys = ki_ref[ks, :]
        a = jnp.zeros((kc2, tq), F32)
        for hg in range(idx_heads // hgrp):
            qi2 = qi_ref[hg * hgrp:(hg + 1) * hgrp].reshape(hgrp * tq, -1)
            s = _nt_dot(keys, qi2)
            for r in range(hgrp):
                h = hg * hgrp + r
                a = a + jnp.maximum(s[:, r * tq:(r + 1) * tq], 0.0) * wt_ref[h:h + 1, :]
        sc_ref[ks, :] = a
        return carry

    lax.fori_loop(0, (n_ch + 1) // 2, score_body, 0)

    def key_body(c, carry):
        keys_ref[chunk(c), :] = jnp.where(causal(c), _sort_key(sc_ref[chunk(c), :]), _INT_MIN)
        return carry

    lax.fori_loop(0, n_ch, key_body, 0)

    def count(pred):
        def hits(c):
            hit = pred(keys_ref[chunk(c), :]).astype(F32)
            return jnp.sum(hit.reshape(tq // 8, 8, tq), axis=0)

        def pair_body(c2, acc):
            return acc[0] + hits(2 * c2), acc[1] + hits(2 * c2 + 1)

        zero = jnp.zeros((8, tq), F32)
        even, odd = lax.fori_loop(0, n_ch // 2, pair_body, (zero, zero))
        last = lax.cond(n_ch % 2 == 1, lambda: hits(n_ch - 1), lambda: zero)
        return jnp.sum(even + odd + last, axis=0, keepdims=True)

    def radix_step(it, thr):
        cand = thr + jnp.left_shift(jnp.int32(1), 31 - it)
        return jnp.where(count(lambda k: k >= cand) >= topk, cand, thr)

    thr = lax.fori_loop(0, 32, radix_step, jnp.full((1, tq), _INT_MIN, I32))
    need = topk - count(lambda k: k > thr)
    n_eq = count(lambda k: k == thr)
    tied = jnp.max(jnp.where((n_eq > need) & (thr != _INT_MIN), 1.0, 0.0)) > 0.0

    def madd_body(c, carry):
        keep = (keys_ref[chunk(c), :] >= thr) & causal(c)
        madd_ref[c] = jnp.where(keep, 0.0, _NEG).T
        return carry

    lax.fori_loop(0, n_ch, madd_body, 0)

    @pl.when(tied)
    def _():
        tri = (lax.broadcasted_iota(I32, (tq, tq), 1) <= lax.broadcasted_iota(I32, (tq, tq), 0)).astype(BF16)

        def tie_body(c, run):
            kc = keys_ref[chunk(c), :]
            eq = kc == thr
            pre = jnp.dot(tri, eq.astype(F32).astype(BF16), preferred_element_type=F32) + run
            keep = ((kc > thr) | (eq & (pre <= need))) & causal(c)
            madd_ref[c] = jnp.where(keep, 0.0, _NEG).T
            return run + jnp.sum(eq.astype(F32), axis=0, keepdims=True)

        lax.fori_loop(0, n_ch, tie_body, jnp.zeros((1, tq), F32))

    m_ref[...] = jnp.full(m_ref.shape, _NEG, F32)
    l_ref[...] = jnp.zeros(l_ref.shape, F32)
    acc_ref[...] = jnp.zeros(acc_ref.shape, F32)

    def attend(c, near):
        madd = madd_ref[c]
        for g in range(n_kv):
            q4 = q_ref[g * rep:(g + 1) * rep].reshape(rows, hd)
            s = _nt_dot(q4, k_ref[g, chunk(c), :])
            slabs = []
            for r in range(rep):
                for a in range(nb):
                    rs = slice(r * tq + a * _LANES, r * tq + (a + 1) * _LANES)
                    x = s[rs, :] + madd[a * _LANES:(a + 1) * _LANES, :]
                    if near is not None:
                        cols = []
                        for b in range(nb):
                            xb = x[:, b * _LANES:(b + 1) * _LANES]
                            kind = near * nb + a - b
                            if kind in (0, 1):
                                xb = xb + tb_ref[g * rep + r, kind]
                            cols.append(xb)
                        x = jnp.concatenate(cols, axis=1)
                    slabs.append(x)
            x = jnp.concatenate(slabs, axis=0)
            xm = x[:, 0:_LANES]
            for b in range(1, nb):
                xm = jnp.maximum(xm, x[:, b * _LANES:(b + 1) * _LANES])
            m_old = m_ref[g]
            m_new = jnp.maximum(m_old, jnp.max(xm, axis=-1, keepdims=True))
            alpha = jnp.exp2(m_old - m_new)
            p = jnp.exp2(x - jnp.concatenate([m_new] * nb, axis=1))
            ps = p[:, 0:_LANES]
            for b in range(1, nb):
                ps = ps + p[:, b * _LANES:(b + 1) * _LANES]
            l_ref[g] = alpha * l_ref[g] + ps
            acc_ref[g] = alpha * acc_ref[g] + jnp.dot(p.astype(_MXU_DTYPE), v_ref[g, chunk(c), :],
                                                      preferred_element_type=F32)
            m_ref[g] = m_new

    attend(i, 0)

    @pl.when(i >= 1)
    def _():
        attend(i - 1, 1)

    def far_body(c, carry):
        attend(c, None)
        return carry

    lax.fori_loop(0, jnp.maximum(i - 1, 0), far_body, 0)

    for g in range(n_kv):
        o = acc_ref[g] / jnp.sum(l_ref[g], axis=-1, keepdims=True)
        o = o * sg_ref[g * rep:(g + 1) * rep].reshape(rows, hd).astype(F32)
        o_ref[g * rep:(g + 1) * rep] = o.reshape(rep, tq, hd).astype(o_ref.dtype)


def _prompt_attention(qi_hm, wk, ki_bf, q_hm, k_hm, v_hm, sg_hm, tb_p, *, batch, seq, tq):
    idx_heads, m, idx_dim = qi_hm.shape
    n_heads, _, hd = q_hm.shape
    n_kv = k_hm.shape[0]
    rep = n_heads // n_kv
    n_q = seq // tq
    topk = min(_TOPK_MAX, seq // 4)
    qrow = lambda b, i: (0, b * n_q + i, 0)
    body = functools.partial(_prompt_attn_body, tq=tq, topk=topk, idx_heads=idx_heads, n_kv=n_kv, rep=rep)
    return pl.pallas_call(
        body,
        grid=(batch, n_q),
        in_specs=[pl.BlockSpec((idx_heads, tq, idx_dim), qrow),
                  pl.BlockSpec((tq, _LANES), lambda b, i: (b * n_q + i, 0)),
                  pl.BlockSpec((seq, idx_dim), lambda b, i: (b, 0)),
                  pl.BlockSpec((n_heads, tq, hd), qrow),
                  pl.BlockSpec((n_kv, seq, hd), lambda b, i: (0, b, 0)),
                  pl.BlockSpec((n_kv, seq, hd), lambda b, i: (0, b, 0)),
                  pl.BlockSpec((n_heads, tq, hd), qrow),
                  pl.BlockSpec((n_heads, 2, _LANES, _LANES), lambda b, i: (0, 0, 0, 0))],
        out_specs=pl.BlockSpec((n_heads, tq, hd), qrow),
        out_shape=jax.ShapeDtypeStruct((n_heads, m, hd), _MXU_DTYPE),
        scratch_shapes=[pltpu.VMEM((_LANES, tq), F32),
                        pltpu.VMEM((seq, tq), F32),
                        pltpu.VMEM((seq, tq), I32),
                        pltpu.VMEM((n_q, tq, tq), F32),
                        pltpu.VMEM((n_kv, rep * tq, _LANES), F32),
                        pltpu.VMEM((n_kv, rep * tq, _LANES), F32),
                        pltpu.VMEM((n_kv, rep * tq, hd), F32)],
        compiler_params=_cparams(2),
        name="prompt_attention",
    )(qi_hm, wk, ki_bf, q_hm, k_hm, v_hm, sg_hm, tb_p)


def _page_copies(pt_ref, cache_ref, buf_ref, sem_ref, b, first_page, n_copy, slot, page):
    return [pltpu.make_async_copy(cache_ref.at[pt_ref[b, first_page + p]],
                                  buf_ref.at[slot, pl.ds(p * page, page)],
                                  sem_ref.at[slot]) for p in range(n_copy)]


def _sample_scores_body(pt_ref, qi_ref, wk_ref, cache_ref, o_ref, buf_ref, wb_ref, sem_ref,
                        *, n_pages, page, idx_heads, t_dec):
    b = pl.program_id(0)
    nb = pl.num_programs(0)
    slot = b % 2
    copies = functools.partial(_page_copies, pt_ref, cache_ref, buf_ref, sem_ref,
                               first_page=0, n_copy=n_pages, page=page)

    @pl.when(b == 0)
    def _():
        for c in copies(b=0, slot=0):
            c.start()

    @pl.when(b + 1 < nb)
    def _():
        for c in copies(b=b + 1, slot=1 - slot):
            c.start()

    qi = qi_ref[...].reshape(idx_heads * t_dec, -1).astype(_MXU_DTYPE)
    w_blk = wk_ref[:, 0:_LANES]
    for h in range(idx_heads):
        wb_ref[h * t_dec:(h + 1) * t_dec, :] = jnp.broadcast_to(w_blk[:, h:h + 1], (t_dec, _LANES))

    def page_scores(keys):
        s = _nt_dot(qi, keys.astype(_MXU_DTYPE))
        return jnp.sum((jnp.maximum(s, 0.0) * wb_ref[...]).reshape(idx_heads, t_dec, page), axis=0)

    past = n_pages * page
    ki_new = jnp.concatenate([wk_ref[:, _LANES:], jnp.zeros((page - t_dec, wk_ref.shape[1] - _LANES), F32)], axis=0)
    row = lax.broadcasted_iota(I32, (t_dec, page), 0)
    col = lax.broadcasted_iota(I32, (t_dec, page), 1)
    o_ref[0, :, past:past + page] = jnp.where(col <= row, page_scores(ki_new), -jnp.inf)

    for c in copies(b=b, slot=slot):
        c.wait()
    for p in range(n_pages):
        o_ref[0, :, p * page:(p + 1) * page] = page_scores(buf_ref[slot, p * page:(p + 1) * page, :])


def _sample_scores(page_table, qi_s, wk_s, cache_idx, *, t_dec):
    bd, n_pages = page_table.shape
    _, page, idx_dim = cache_idx.shape
    idx_heads = qi_s.shape[0]
    past = n_pages * page
    grid_spec = pltpu.PrefetchScalarGridSpec(
        num_scalar_prefetch=1,
        grid=(bd,),
        in_specs=[pl.BlockSpec((idx_heads, t_dec, idx_dim), lambda b, pt: (0, b, 0)),
                  pl.BlockSpec((t_dec, wk_s.shape[1]), lambda b, pt: (b, 0)),
                  pl.BlockSpec(memory_space=pl.ANY)],
        out_specs=pl.BlockSpec((1, t_dec, past + page), lambda b, pt: (b, 0, 0)),
        scratch_shapes=[pltpu.VMEM((2, past, idx_dim), F32),
                        pltpu.VMEM((idx_heads * t_dec, _LANES), F32),
                        pltpu.SemaphoreType.DMA((2,))],
    )
    return pl.pallas_call(
        functools.partial(_sample_scores_body, n_pages=n_pages, page=page, idx_heads=idx_heads, t_dec=t_dec),
        grid_spec=grid_spec,
        out_shape=jax.ShapeDtypeStruct((bd, t_dec, past + page), F32),
        compiler_params=_cparams(1),
        name="sample_scores",
    )(page_table, qi_s, wk_s, cache_idx)


def _sample_select_body(sc_ref, o_ref, keys_ref, *, topk, past):
    bd, t_dec, n_cols = sc_ref.shape

    def valid(c0, width):
        col = lax.broadcasted_iota(I32, (bd, t_dec, width), 2) + c0
        return col <= past + lax.broadcasted_iota(I32, (bd, t_dec, width), 1)

    keys_ref[...] = jnp.where(valid(0, n_cols), _sort_key(sc_ref[...]), _INT_MIN)

    def step(it, thr):
        cand = thr + jnp.left_shift(jnp.int32(1), 31 - it)
        cnt = jnp.sum((keys_ref[...] >= cand).astype(F32), axis=-1, keepdims=True)
        return jnp.where(cnt >= topk, cand, thr)

    thr = lax.fori_loop(0, 32, step, jnp.full((bd, t_dec, 1), _INT_MIN, I32))
    keys = keys_ref[...]
    cnt_gt = jnp.sum((keys > thr).astype(F32), axis=-1, keepdims=True)
    cnt_eq = jnp.sum((keys == thr).astype(F32), axis=-1, keepdims=True)
    need = topk - cnt_gt
    o_ref[...] = ((keys >= thr) & valid(0, n_cols)).astype(F32)
    tied = jnp.max(jnp.where((cnt_eq > need) & (thr != _INT_MIN), 1.0, 0.0)) > 0.0

    @pl.when(tied)
    def _():
        tri = (lax.broadcasted_iota(I32, (_LANES, _LANES), 0)
               <= lax.broadcasted_iota(I32, (_LANES, _LANES), 1)).astype(BF16)
        run = jnp.zeros((bd, t_dec, 1), F32)
        for c0 in range(0, n_cols, _LANES):
            kc = keys_ref[:, :, c0:c0 + _LANES]
            eq = kc == thr
            eq2 = eq.astype(F32).reshape(bd * t_dec, _LANES).astype(BF16)
            pre = jnp.dot(eq2, tri, preferred_element_type=F32).reshape(bd, t_dec, _LANES) + run
            run = run + jnp.sum(eq.astype(F32), axis=-1, keepdims=True)
            keep = (kc > thr) | (eq & (pre <= need))
            o_ref[:, :, c0:c0 + _LANES] = (keep & valid(c0, _LANES)).astype(F32)


def _sample_select(scores, *, topk, past):
    return pl.pallas_call(
        functools.partial(_sample_select_body, topk=topk, past=past),
        out_shape=jax.ShapeDtypeStruct(scores.shape, F32),
        scratch_shapes=[pltpu.VMEM(scores.shape, I32)],
        compiler_params=pltpu.CompilerParams(vmem_limit_bytes=_VMEM_LIMIT),
        name="sample_select",
    )(scores)


def _sample_attn_body(pt_ref, rb_ref, q_ref, mask_ref, mtail_ref, kn_ref, vn_ref, sg_ref, ts_ref,
                      ck_ref, cv_ref, o_ref, kbuf_ref, vbuf_ref, m_ref, l_ref, acc_ref, ksem_ref, vsem_ref,
                      *, n_chunks, cpages, page, n_kv, rep, t_dec, n_buckets):
    b = pl.program_id(0)
    c = pl.program_id(1)
    step = b * n_chunks + c
    n_steps = pl.num_programs(0) * n_chunks
    slot = step % 2
    hd = q_ref.shape[-1]
    rows = rep * t_dec
    ck = cpages * page

    prow = page * n_kv

    def copies(bb, cc, sl):
        return (_page_copies(pt_ref, ck_ref, kbuf_ref, ksem_ref, bb, cc * cpages, cpages, sl, prow)
                + _page_copies(pt_ref, cv_ref, vbuf_ref, vsem_ref, bb, cc * cpages, cpages, sl, prow))

    def kv_head(buf_ref, g):
        return buf_ref[slot, pl.ds(g, ck, stride=n_kv), :]

    @pl.when(step == 0)
    def _():
        for cp in copies(0, 0, 0):
            cp.start()

    @pl.when(step + 1 < n_steps)
    def _():
        nxt = step + 1
        for cp in copies(nxt // n_chunks, nxt % n_chunks, 1 - slot):
            cp.start()

    @pl.when(c == 0)
    def _():
        m_ref[...] = jnp.full(m_ref.shape, -jnp.inf, F32)
        l_ref[...] = jnp.zeros(l_ref.shape, F32)
        acc_ref[...] = jnp.zeros(acc_ref.shape, F32)

    def update(g, logits, mask, values):
        n = logits.shape[-1]
        x = jnp.where(jnp.broadcast_to(mask[None], (rep, t_dec, n)).reshape(rows, n) > 0.5, logits, -jnp.inf)
        rs = slice(g * rows, (g + 1) * rows)
        m_old = m_ref[rs, :]
        m_new = jnp.maximum(m_old, jnp.max(x, axis=-1, keepdims=True))
        m_safe = jnp.where(m_new == -jnp.inf, 0.0, m_new)
        alpha = jnp.exp(m_old - m_safe)
        p = jnp.exp(x - m_safe[:, 0:1])
        l_ref[rs, :] = alpha * l_ref[rs, :] + jnp.sum(p, axis=-1, keepdims=True)
        acc_ref[rs, :] = alpha * acc_ref[rs, :] + jnp.dot(p.astype(_MXU_DTYPE), values.astype(_MXU_DTYPE),
                                                          preferred_element_type=F32)
        m_ref[rs, :] = m_new

    def head_bias(g, kind):
        return ts_ref[g * rep:(g + 1) * rep, kind].reshape(rows, _LANES)

    def far_bias(g):
        r_idx = lax.broadcasted_iota(I32, (rows, 1), 0) // t_dec
        out = jnp.zeros((rows, 1), F32)
        for r in range(rep):
            out = jnp.where(r_idx == r, rb_ref[n_buckets - 1, g * rep + r], out)
        return out

    for cp in copies(b, c, slot):
        cp.wait()

    is_last = (c == n_chunks - 1).astype(F32)
    for g in range(n_kv):
        qg = q_ref[g * rep:(g + 1) * rep].reshape(rows, hd).astype(_MXU_DTYPE)
        kc = kv_head(kbuf_ref, g)
        far = far_bias(g)
        lg = _nt_dot(qg, kc.astype(_MXU_DTYPE)) + far
        near = lg[:, ck - page:] + (head_bias(g, 0) - far) * is_last
        lg = jnp.concatenate([lg[:, :ck - page], near], axis=1)
        update(g, lg, mask_ref[...], kv_head(vbuf_ref, g))

    @pl.when(c == n_chunks - 1)
    def _():
        zpad = jnp.zeros((page - t_dec, hd), F32)
        for g in range(n_kv):
            qg = q_ref[g * rep:(g + 1) * rep].reshape(rows, hd).astype(_MXU_DTYPE)
            kn = jnp.concatenate([kn_ref[pl.ds(g, t_dec, stride=n_kv), :], zpad], axis=0)
            vn = jnp.concatenate([vn_ref[pl.ds(g, t_dec, stride=n_kv), :], zpad], axis=0)
            lg = _nt_dot(qg, kn.astype(_MXU_DTYPE)) + head_bias(g, 1)
            update(g, lg, mtail_ref[...], vn)
            rs = slice(g * rows, (g + 1) * rows)
            o = acc_ref[rs, :] / l_ref[rs, :] * sg_ref[g * rep:(g + 1) * rep].reshape(rows, hd)
            o_ref[g * rep:(g + 1) * rep] = o.reshape(rep, t_dec, hd)


def _sample_attention(page_table, rel_bias, q_s, mask, kn, vn, sg_s, ts, cache_k, cache_v, *, t_dec, cpages, page):
    bd, n_pages = page_table.shape
    n_heads, _, hd = q_s.shape
    n_kv = cache_k.shape[1] // page
    rep = n_heads // n_kv
    n_chunks = n_pages // cpages
    ck = cpages * page
    past = n_pages * page
    hrow = lambda b, c, pt: (0, b, 0)
    grid_spec = pltpu.PrefetchScalarGridSpec(
        num_scalar_prefetch=1,
        grid=(bd, n_chunks),
        in_specs=[pl.BlockSpec(memory_space=pltpu.SMEM),
                  pl.BlockSpec((n_heads, t_dec, hd), hrow),
                  pl.BlockSpec((t_dec, ck), lambda b, c, pt: (b, c)),
                  pl.BlockSpec((t_dec, page), lambda b, c, pt: (b, past // page)),
                  pl.BlockSpec((t_dec * n_kv, hd), lambda b, c, pt: (b, 0)),
                  pl.BlockSpec((t_dec * n_kv, hd), lambda b, c, pt: (b, 0)),
                  pl.BlockSpec((n_heads, t_dec, hd), hrow),
                  pl.BlockSpec((n_heads, 2, t_dec, _LANES), lambda b, c, pt: (0, 0, 0, 0)),
                  pl.BlockSpec(memory_space=pl.ANY),
                  pl.BlockSpec(memory_space=pl.ANY)],
        out_specs=pl.BlockSpec((n_heads, t_dec, hd), hrow),
        scratch_shapes=[pltpu.VMEM((2, ck * n_kv, hd), F32),
                        pltpu.VMEM((2, ck * n_kv, hd), F32),
                        pltpu.VMEM((n_heads * t_dec, _LANES), F32),
                        pltpu.VMEM((n_heads * t_dec, _LANES), F32),
                        pltpu.VMEM((n_heads * t_dec, hd), F32),
                        pltpu.SemaphoreType.DMA((2,)),
                        pltpu.SemaphoreType.DMA((2,))],
    )
    return pl.pallas_call(
        functools.partial(_sample_attn_body, n_chunks=n_chunks, cpages=cpages, page=page, n_kv=n_kv,
                          rep=rep, t_dec=t_dec, n_buckets=rel_bias.shape[0]),
        grid_spec=grid_spec,
        out_shape=jax.ShapeDtypeStruct((n_heads, bd * t_dec, hd), F32),
        compiler_params=_cparams(2),
        name="sample_attention",
    )(page_table, rel_bias.astype(F32), q_s, mask, mask, kn, vn, sg_s, ts, cache_k, cache_v)


def _pool_body(u_ref, halo_ref, gb_ref, wp_ref, ps_ref, o_ref, ext_ref, *, tm, pos_base, own_halo, pg):
    i = pl.program_id(1)
    bb = u_ref.shape[0]
    halo = halo_ref[...]
    if own_halo:
        halo = jnp.where(i > 0, halo, 0.0)
    ext_ref[:, 0:_POOL_HALO, :] = halo
    ext_ref[:, _POOL_HALO:, :] = u_ref[...]
    pos = pos_base + i * tm + lax.broadcasted_iota(I32, (1, tm, 1), 1)
    for g, win in enumerate(_POOL_WINDOWS):
        cols = slice(g * pg, (g + 1) * pg)
        tot = ext_ref[:, _POOL_HALO:, cols]
        for k in range(1, win):
            tot = tot + ext_ref[:, _POOL_HALO - k:_POOL_HALO - k + tm, cols]
        cnt = jnp.minimum(pos + 1, win).astype(F32)
        d = tot / cnt - ext_ref[:, _POOL_HALO:, cols]
        y = jnp.dot(d.reshape(bb * tm, pg).astype(_MXU_DTYPE), wp_ref[g], preferred_element_type=F32)
        y = y.reshape(bb, tm, pg) * ps_ref[:, cols] * gb_ref[:, :, cols].astype(F32)
        o_ref[:, :, cols] = y.astype(o_ref.dtype)


def _pool(u, halo, gb, wp, ps, *, bb, tm, pos_base, own_halo, out_dtype):
    n_seq, rows, pw = u.shape
    pg = pw // len(_POOL_WINDOWS)
    if own_halo:
        hb = tm // _POOL_HALO
        halo_spec = pl.BlockSpec((bb, _POOL_HALO, pw), lambda b, i: (b, jnp.maximum(i * hb - 1, 0), 0))
    else:
        halo_spec = pl.BlockSpec((bb, _POOL_HALO, pw), lambda b, i: (b, 0, 0))
    return pl.pallas_call(
        functools.partial(_pool_body, tm=tm, pos_base=pos_base, own_halo=own_halo, pg=pg),
        grid=(n_seq // bb, rows // tm),
        in_specs=[pl.BlockSpec((bb, tm, pw), lambda b, i: (b, i, 0)),
                  halo_spec,
                  pl.BlockSpec((bb, tm, pw), lambda b, i: (b, i, 0)),
                  pl.BlockSpec(wp.shape, lambda b, i: (0, 0, 0)),
                  pl.BlockSpec((1, pw), lambda b, i: (0, 0))],
        out_specs=pl.BlockSpec((bb, tm, pw), lambda b, i: (b, i, 0)),
        out_shape=jax.ShapeDtypeStruct((n_seq, rows, pw), out_dtype),
        scratch_shapes=[pltpu.VMEM((bb, _POOL_HALO + tm, pw), F32)],
        compiler_params=_cparams(2),
        name="pool",
    )(u, halo, gb, wp, ps)


def _merge_body(ma_ref, mb_ref, w_ref, x_ref, g_ref, o_ref, mix_ref, *, n_j, tn):
    j = pl.program_id(1)
    n_heads, _, hd = ma_ref.shape

    @pl.when(j == 0)
    def _():
        for h in range(n_heads):
            mix_ref[:, h * hd:(h + 1) * hd] = ma_ref[h].astype(mix_ref.dtype)
        mix_ref[:, n_heads * hd:] = mb_ref[...].astype(mix_ref.dtype)

    hv = x_ref[...] + jnp.dot(mix_ref[...], w_ref[...], preferred_element_type=F32)
    for jj in range(n_j):
        @pl.when(j == jj)
        def _(jj=jj):
            o_ref[:, jj * tn:(jj + 1) * tn] = hv

    @pl.when(j == n_j - 1)
    def _():
        ssq = jnp.zeros((o_ref.shape[0], 1), F32)
        for jj in range(n_j):
            hj = o_ref[:, jj * tn:(jj + 1) * tn]
            ssq = ssq + jnp.sum(hj * hj, axis=-1, keepdims=True)
        inv = lax.rsqrt(ssq / (n_j * tn) + _RMS_EPS)
        for jj in range(n_j):
            cols = slice(jj * tn, (jj + 1) * tn)
            o_ref[:, cols] = o_ref[:, cols] * inv * g_ref[:, cols]


def _merge(mix_a, mix_b, w_out, x2d, final_g, *, tm, tn):
    n_heads, m, hd = mix_a.shape
    pw = mix_b.shape[1]
    mixw, d = w_out.shape
    n_j = d // tn
    return pl.pallas_call(
        functools.partial(_merge_body, n_j=n_j, tn=tn),
        grid=(m // tm, n_j),
        in_specs=[pl.BlockSpec((n_heads, tm, hd), lambda i, j: (0, i, 0)),
                  pl.BlockSpec((tm, pw), lambda i, j: (i, 0)),
                  pl.BlockSpec((mixw, tn), lambda i, j: (0, j)),
                  pl.BlockSpec((tm, tn), lambda i, j: (i, j)),
                  pl.BlockSpec((1, d), lambda i, j: (0, 0))],
        out_specs=pl.BlockSpec((tm, d), lambda i, j: (i, 0)),
        out_shape=jax.ShapeDtypeStruct((m, d), F32),
        scratch_shapes=[pltpu.VMEM((tm, mixw), _MXU_DTYPE)],
        compiler_params=_cparams(2),
        name="merge",
    )(mix_a, mix_b, w_out, x2d, final_g.reshape(1, d).astype(F32))


def kernel(x_prompt, x_sample, cache_k, cache_v, cache_idx_k, state_pool, page_table,
           rel_bias, ln_g, w_in, w_pool, pool_scale, w_out, final_g):
    depth = w_in.shape[0]
    assert depth == 1, "single-layer step only"
    batch, seq, d_model = x_prompt.shape
    bd, t_dec, _ = x_sample.shape
    _, n_phys, page, n_kv, hd = cache_k.shape
    idx_dim = cache_idx_k.shape[-1]
    n_buckets, n_heads = rel_bias.shape
    pw = pool_scale.shape[-1]
    n_pg = w_pool.shape[1]
    attn_w, kv_w = n_heads * hd, n_kv * hd
    in_w = w_in.shape[-1]
    idx_heads = (in_w - 2 * attn_w - 2 * kv_w - idx_dim - 2 * pw) // (idx_dim + 1)
    n_pages = page_table.shape[1]
    past = n_pages * page
    mp, ms = batch * seq, bd * t_dec
    tq = _PROMPT_Q
    assert hd == _LANES and idx_dim == _LANES and idx_heads <= _LANES and n_pg == len(_POOL_WINDOWS)
    assert page >= _MAX_DISTANCE and _LANES >= _MAX_DISTANCE and seq % (2 * tq) == 0 and t_dec <= _POOL_HALO - 1
    assert state_pool.shape[2] == _POOL_HALO - 1

    o_q, o_k, o_v = 0, attn_w, attn_w + kv_w
    o_ga = o_v + kv_w
    o_qi = o_ga + attn_w
    o_wi = o_qi + idx_heads * idx_dim
    o_ki = o_wi + idx_heads
    o_u = o_ki + idx_dim
    o_gb = o_u + pw
    wt = jnp.transpose(w_in[0])

    xn_p = _rmsnorm(x_prompt.reshape(mp, d_model), ln_g[0], _MXU_DTYPE)
    xn_s = _rmsnorm(x_sample.reshape(ms, d_model), ln_g[0], _MXU_DTYPE)

    tm = min(_PROJ_ROWS, mp)
    tw = min(_PROJ_ROWS_WIDE, mp)
    tn = lambda n: min(_PROJ_COLS, n)
    rm = lambda n, t, dt, s, rows=tm: _row_major(ms if s else mp, n, rows, t, dt, s)
    hm = lambda n, t, dt, s, rows=tm: _head_major(ms if s else mp, n, rows, t, dt, s)

    def proj(lo, hi, t, rows=tm, **kw):
        assert (hi - lo) % t == 0
        return _proj_call(xn_p, xn_s, wt, [(lo, t)], (hi - lo) // t, rows, **kw)

    def scaled_heads(scale):
        def epi(acc, refs):
            _store_heads(refs[0], acc * scale)
        return epi

    def silu_heads(acc, refs):
        _store_heads(refs[0], _silu(acc))

    def silu_rows(acc, refs):
        refs[0][...] = _silu(acc).astype(refs[0].dtype)

    def plain_rows(acc, refs):
        refs[0][...] = acc.astype(refs[0].dtype)

    def kv_rows(acc, ref):
        for g in range(n_kv):
            ref[pl.ds(g, acc.shape[0], stride=n_kv), :] = acc[:, g * hd:(g + 1) * hd]

    def kv_p(acc, refs):
        kv_rows(acc, refs[0])
        _store_heads(refs[1], acc)

    def kv_s(acc, refs):
        kv_rows(acc, refs[0])

    def wk_p(acc, refs):
        refs[0][...] = jnp.concatenate([acc[:, :_LANES] * idx_heads ** -0.5, acc[:, _LANES:]], axis=1)
        refs[1][...] = acc[:, _LANES:].astype(refs[1].dtype)

    def wk_s(acc, refs):
        refs[0][...] = jnp.concatenate([acc[:, :_LANES] * idx_heads ** -0.5, acc[:, _LANES:]], axis=1)

    t = tn(attn_w)
    q_hm, q_s = proj(o_q, o_k, t, tw, outs_p=[hm(attn_w, t, _MXU_DTYPE, False, tw)],
                     outs_s=[hm(attn_w, t, F32, True)],
                     epi_p=scaled_heads(hd ** -0.5 * _LOG2E), epi_s=scaled_heads(hd ** -0.5), name="proj_q")
    kv_outs = dict(outs_p=[((mp * n_kv, hd), F32, (tm * n_kv, hd), lambda j, i: (i, 0)),
                           ((n_kv, mp, hd), _MXU_DTYPE, (n_kv, tm, hd), lambda j, i: (0, i, 0))],
                   outs_s=[((ms * n_kv, hd), F32, (ms * n_kv, hd), lambda j, i: (0, 0))], epi_p=kv_p, epi_s=kv_s)
    k_p, k_hm, k_s = proj(o_k, o_v, kv_w, name="proj_k", **kv_outs)
    v_p, v_hm, v_s = proj(o_v, o_ga, kv_w, name="proj_v", **kv_outs)
    sga_hm, sga_s = proj(o_ga, o_qi, t, tw, outs_p=[hm(attn_w, t, _MXU_DTYPE, False, tw)],
                         outs_s=[hm(attn_w, t, F32, True)], epi_p=silu_heads, epi_s=silu_heads, name="proj_ga")
    t = tn(idx_heads * idx_dim)
    qi_hm, qi_s = proj(o_qi, o_wi, t, tw, outs_p=[hm(idx_heads * idx_dim, t, _MXU_DTYPE, False, tw)],
                       outs_s=[hm(idx_heads * idx_dim, t, F32, True)],
                       epi_p=scaled_heads(idx_dim ** -0.5), epi_s=scaled_heads(idx_dim ** -0.5), name="proj_qi")
    t = 2 * _LANES
    wk_pr, ki_bf, wk_sm = _proj_call(
        xn_p, xn_s, wt, [(o_wi, _LANES), (o_ki, idx_dim)], 1, tm,
        outs_p=[((mp, t), F32, (tm, t), lambda j, i: (i, 0)), ((mp, idx_dim), _MXU_DTYPE, (tm, idx_dim), lambda j, i: (i, 0))],
        outs_s=[((ms, t), F32, (ms, t), lambda j, i: (0, 0))],
        epi_p=wk_p, epi_s=wk_s, name="proj_wk")
    t = tn(pw)
    u_p, u_s = proj(o_u, o_gb, t, outs_p=[rm(pw, t, F32, False)], outs_s=[rm(pw, t, F32, True)],
                    epi_p=plain_rows, epi_s=plain_rows, name="proj_u")
    sgb_p, sgb_s = proj(o_gb, in_w, t, tw, outs_p=[rm(pw, t, _MXU_DTYPE, False, tw)], outs_s=[rm(pw, t, F32, True)],
                        epi_p=silu_rows, epi_s=silu_rows, name="proj_gb")

    tb_p, tb_s = _bias_tables(rel_bias, _LANES, t_dec, page)

    mixa_p = _prompt_attention(qi_hm, wk_pr, ki_bf, q_hm, k_hm, v_hm, sga_hm, tb_p, batch=batch, seq=seq, tq=tq)

    scores = _sample_scores(page_table, qi_s, wk_sm, cache_idx_k[0], t_dec=t_dec)
    topk_s = min(_TOPK_MAX, (past + t_dec) // 4)
    mask_s = _sample_select(scores, topk=topk_s, past=past).reshape(ms, past + page)
    cpages = max(1, min(_PAGES_PER_STEP, n_pages // 2))
    assert n_pages % cpages == 0
    mixa_s = _sample_attention(page_table, rel_bias, q_s, mask_s, k_s, v_s, sga_s, tb_s,
                               cache_k[0].reshape(n_phys, page * n_kv, hd), cache_v[0].reshape(n_phys, page * n_kv, hd),
                               t_dec=t_dec, cpages=cpages, page=page)

    wp = w_pool[0].astype(_MXU_DTYPE)
    ps = pool_scale[0].reshape(1, pw).astype(F32)
    u_p3 = u_p.reshape(batch, seq, pw)
    mixb_p = _pool(u_p3, u_p3, sgb_p.reshape(batch, seq, pw), wp, ps, bb=1, tm=min(_POOL_ROWS, seq), pos_base=0,
                   own_halo=True, out_dtype=_MXU_DTYPE).reshape(mp, pw)
    halo_s = jnp.concatenate([jnp.zeros((bd, 1, pw), F32), state_pool[0].astype(F32)], axis=1)
    mixb_s = _pool(u_s.reshape(bd, t_dec, pw), halo_s, sgb_s.reshape(bd, t_dec, pw), wp, ps, bb=bd, tm=t_dec,
                   pos_base=past, own_halo=False, out_dtype=F32).reshape(ms, pw)

    w_o = w_out[0].astype(_MXU_DTYPE)
    tn_o = min(_MERGE_COLS, d_model)
    y_p = _merge(mixa_p, mixb_p, w_o, x_prompt.reshape(mp, d_model), final_g, tm=min(_MERGE_ROWS, mp), tn=tn_o)
    y_s = _merge(mixa_s, mixb_s, w_o, x_sample.reshape(ms, d_model), final_g, tm=ms, tn=tn_o)

    n_state = state_pool.shape[2]
    u_full_s = jnp.concatenate([state_pool[0].astype(F32), u_s.reshape(bd, t_dec, pw)], axis=1)
    return (y_p.reshape(batch, seq, d_model), y_s.reshape(bd, t_dec, d_model),
            k_p.reshape(1, batch, seq, n_kv, hd), v_p.reshape(1, batch, seq, n_kv, hd),
            wk_pr[:, _LANES:].reshape(1, batch, seq, idx_dim),
            u_p.reshape(batch, seq, pw)[:, seq - n_state:][None],
            k_s.reshape(1, bd, t_dec, n_kv, hd), v_s.reshape(1, bd, t_dec, n_kv, hd),
            wk_sm[:, _LANES:].reshape(1, bd, t_dec, idx_dim),
            u_full_s[:, t_dec:][None])
```

```python
import functools
import math

import jax
import jax.numpy as jnp
import numpy as np
from jax import lax
from jax.experimental import pallas as pl
from jax.experimental.pallas import tpu as pltpu

F32 = jnp.float32
BF16 = jnp.bfloat16
I32 = jnp.int32

_TOPK_MAX = 256
_POOL_WINDOWS = (2, 4, 8, 16)
_POOL_HALO = 16
_MAX_DISTANCE = 128
_RMS_EPS = 1e-6
_NEG = -1e30
_LOG2E = math.log2(math.e)
_INT_MIN = -(2 ** 31)
_LANES = 128
_SUBLANES = 8
_VMEM_LIMIT = 58 * 1024 * 1024
_MXU_DTYPE = BF16

_RMS_ROWS = 512
_PROJ_ROWS = 1024
_PROJ_ROWS_WIDE = 2048
_PROJ_COLS = 512
_PROMPT_Q = 2 * _LANES
_PAGES_PER_CHUNK = 16
_PAGE_SLOTS = 3
_POOL_ROWS = 512
_MERGE_ROWS = 512
_MERGE_COLS = 1024


def _cparams(n_axes):
    return pltpu.CompilerParams(dimension_semantics=("arbitrary",) * n_axes,
                                vmem_limit_bytes=_VMEM_LIMIT)


def _nt_dot(a, b):
    return lax.dot_general(a, b, (((1,), (1,)), ((), ())), preferred_element_type=F32)


def _key_value(key):
    return lax.bitcast_convert_type(key ^ ((key >> 31) & 0x7FFFFFFF), F32)


def _rmsnorm_body(x_ref, g_ref, o_ref):
    x = x_ref[...].astype(F32)
    var = jnp.mean(x * x, axis=-1, keepdims=True)
    o_ref[...] = (x * lax.rsqrt(var + _RMS_EPS) * g_ref[...]).astype(o_ref.dtype)


def _rmsnorm(x2d, g, out_dtype):
    m, d = x2d.shape
    tm = min(_RMS_ROWS, m)
    return pl.pallas_call(
        _rmsnorm_body,
        grid=(m // tm,),
        in_specs=[pl.BlockSpec((tm, d), lambda i: (i, 0)), pl.BlockSpec((1, d), lambda i: (0, 0))],
        out_specs=pl.BlockSpec((tm, d), lambda i: (i, 0)),
        out_shape=jax.ShapeDtypeStruct((m, d), out_dtype),
        compiler_params=_cparams(1),
        name="rmsnorm",
    )(x2d, g.reshape(1, d).astype(F32))


def _proj_body(xp_ref, xs_ref, wt_ref, *refs, w_rows, n_i, n_p, n_out, epi_p, epi_s):
    out_refs, wbuf_ref, wc_ref, sem_ref = refs[:n_out], refs[n_out], refs[n_out + 1], refs[n_out + 2]
    j, i = pl.program_id(0), pl.program_id(1)

    def w_copies(jj):
        out, r0 = [], 0
        for first, cnt in w_rows:
            src = wt_ref.at[pl.ds(pl.multiple_of(first + jj * cnt, _SUBLANES), cnt)]
            out.append(pltpu.make_async_copy(src, wbuf_ref.at[pl.ds(r0, cnt)], sem_ref.at[0]))
            r0 += cnt
        return out

    @pl.when((j == 0) & (i == 0))
    def _():
        for c in w_copies(0):
            c.start()

    @pl.when(i == 0)
    def _():
        for c in w_copies(j):
            c.wait()
        wc_ref[...] = wbuf_ref[...].astype(wc_ref.dtype)

    @pl.when((i == min(1, n_i - 1)) & (j + 1 < pl.num_programs(0)))
    def _():
        for c in w_copies(j + 1):
            c.start()

    epi_p(_nt_dot(xp_ref[...], wc_ref[...]), out_refs[:n_p])

    @pl.when(i == 0)
    def _():
        epi_s(_nt_dot(xs_ref[...], wc_ref[...]), out_refs[n_p:])


def _proj_call(xn_p, xn_s, wt, w_rows, n_tiles, tm, outs_p, outs_s, epi_p, epi_s, name):
    mp, d = xn_p.shape
    ms = xn_s.shape[0]
    assert mp % tm == 0 and all(first % _SUBLANES == 0 and cnt % _SUBLANES == 0 for first, cnt in w_rows)
    tn = sum(cnt for _, cnt in w_rows)
    outs = list(outs_p) + list(outs_s)
    return pl.pallas_call(
        functools.partial(_proj_body, w_rows=tuple(w_rows), n_i=mp // tm, n_p=len(outs_p), n_out=len(outs),
                          epi_p=epi_p, epi_s=epi_s),
        grid=(n_tiles, mp // tm),
        in_specs=[pl.BlockSpec((tm, d), lambda j, i: (i, 0)),
                  pl.BlockSpec((ms, d), lambda j, i: (0, 0), pipeline_mode=pl.Buffered(1)),
                  pl.BlockSpec(memory_space=pl.ANY)],
        out_specs=[pl.BlockSpec(o[2], o[3]) for o in outs],
        out_shape=[jax.ShapeDtypeStruct(o[0], o[1]) for o in outs],
        scratch_shapes=[pltpu.VMEM((tn, d), wt.dtype), pltpu.VMEM((tn, d), xn_p.dtype),
                        pltpu.SemaphoreType.DMA((1,))],
        compiler_params=_cparams(2),
        name=name,
    )(xn_p, xn_s, wt)


def _row_major(m, ncols, tm, tn, dtype, sample):
    if sample:
        return ((m, ncols), dtype, (m, tn), lambda j, i: (0, j))
    return ((m, ncols), dtype, (tm, tn), lambda j, i: (i, j))


def _head_major(m, ncols, tm, tn, dtype, sample):
    nh, hb = ncols // _LANES, tn // _LANES
    if sample:
        return ((nh, m, _LANES), dtype, (hb, m, _LANES), lambda j, i: (j, 0, 0))
    return ((nh, m, _LANES), dtype, (hb, tm, _LANES), lambda j, i: (j, i, 0))


def _store_heads(ref, val):
    for hh in range(ref.shape[0]):
        ref[hh] = val[:, hh * _LANES:(hh + 1) * _LANES].astype(ref.dtype)


def _silu(x):
    return x * jax.nn.sigmoid(x)


def _bucket_steps(n_buckets):
    max_exact = n_buckets // 2
    d = np.arange(max_exact, 2 * _MAX_DISTANCE + 1)
    tabs = []
    for dt in (np.float32, np.float64):
        v = np.log(d.astype(dt) / dt(max_exact)) / dt(math.log(_MAX_DISTANCE / max_exact)) * dt(n_buckets - max_exact)
        tabs.append(np.minimum(v.astype(np.int64), n_buckets - 1 - max_exact))
    assert (tabs[0] == tabs[1]).all()
    return [int(d[np.argmax(tabs[1] >= j)]) for j in range(1, n_buckets - max_exact)]


def _bias_tables_body(rb_ref, tp_ref, ts_ref, *, n_buckets, tq, t_dec, page):
    h = pl.program_id(0)
    max_exact = n_buckets // 2
    steps = _bucket_steps(n_buckets)

    def table(dist):
        d = jnp.maximum(dist, 0)
        bucket = jnp.minimum(d, max_exact)
        for first in steps:
            bucket = bucket + (d >= first).astype(I32)
        out = jnp.zeros(dist.shape, F32)
        for b in range(n_buckets):
            out = jnp.where(bucket == b, rb_ref[b, h], out)
        return out

    rp = lax.broadcasted_iota(I32, (tq, _LANES), 0)
    cp = lax.broadcasted_iota(I32, (tq, _LANES), 1)
    far = rb_ref[n_buckets - 1, h]
    tp_ref[0, 0] = (table(rp - cp) - far) * _LOG2E
    tp_ref[0, 1] = (table(rp - cp + tq) - far) * _LOG2E
    rs = lax.broadcasted_iota(I32, (t_dec, _LANES), 0)
    cs = lax.broadcasted_iota(I32, (t_dec, _LANES), 1)
    ts_ref[0, 0] = table(rs - cs + page)
    ts_ref[0, 1] = table(rs - cs)


def _bias_tables(rel_bias, tq, t_dec, page):
    n_buckets, n_heads = rel_bias.shape
    return pl.pallas_call(
        functools.partial(_bias_tables_body, n_buckets=n_buckets, tq=tq, t_dec=t_dec, page=page),
        grid=(n_heads,),
        in_specs=[pl.BlockSpec(memory_space=pltpu.SMEM)],
        out_specs=[pl.BlockSpec((1, 2, tq, _LANES), lambda h: (h, 0, 0, 0)),
                   pl.BlockSpec((1, 2, t_dec, _LANES), lambda h: (h, 0, 0, 0))],
        out_shape=[jax.ShapeDtypeStruct((n_heads, 2, tq, _LANES), F32),
                   jax.ShapeDtypeStruct((n_heads, 2, t_dec, _LANES), F32)],
        compiler_params=_cparams(1),
        name="bias_tables",
    )(rel_bias.astype(F32))


def _prompt_attn_body(qi_ref, wk_ref, ki_ref, q_ref, k_ref, v_ref, sg_ref, tb_ref, o_ref,
                      wt_ref, sc_ref, keys_ref, madd_ref, m_ref, l_ref, acc_ref,
                      *, tq, topk, idx_heads, n_kv, rep):
    i = pl.program_id(1)
    n_ch = i + 1
    hd = q_ref.shape[-1]
    rows = rep * tq
    nb = tq // _LANES
    hgrp = 2
    kc2 = 2 * tq

    def chunk(c):
        return pl.ds(pl.multiple_of(c * tq, tq), tq)

    def causal(c):
        key_pos = lax.broadcasted_iota(I32, (tq, tq), 0) + c * tq
        return key_pos <= lax.broadcasted_iota(I32, (tq, tq), 1) + i * tq

    wt_ref[...] = wk_ref[...].T

    def score_body(c2, carry):
        ks = pl.ds(pl.multiple_of(c2 * kc2, kc2), kc2)
        keys = ki_ref[ks, :]
        a = jnp.zeros((kc2, tq), F32)
        for hg in range(idx_heads // hgrp):
            qi2 = qi_ref[hg * hgrp:(hg + 1) * hgrp].reshape(hgrp * tq, -1)
            s = _nt_dot(keys, qi2)
            for r in range(hgrp):
                h = hg * hgrp + r
                a = a + jnp.maximum(s[:, r * tq:(r + 1) * tq], 0.0) * wt_ref[h:h + 1, :]
        sc_ref[ks, :] = a
        return carry

    lax.fori_loop(0, (n_ch + 1) // 2, score_body, 0)

    def key_body(c, carry):
        keys_ref[chunk(c), :] = jnp.where(causal(c), sc_ref[chunk(c), :], -jnp.inf)
        return carry

    lax.fori_loop(0, n_ch, key_body, 0)

    def count(pred):
        def hits(c):
            hit = pred(keys_ref[chunk(c), :]).astype(F32)
            return jnp.sum(hit.reshape(tq // 8, 8, tq), axis=0)

        def pair_body(c2, acc):
            return acc[0] + hits(2 * c2), acc[1] + hits(2 * c2 + 1)

        zero = jnp.zeros((8, tq), F32)
        even, odd = lax.fori_loop(0, n_ch // 2, pair_body, (zero, zero))
        last = lax.cond(n_ch % 2 == 1, lambda: hits(n_ch - 1), lambda: zero)
        return jnp.sum(even + odd + last, axis=0, keepdims=True)

    def radix_step(it, key):
        cand = key + jnp.left_shift(jnp.int32(1), 31 - it)
        cand_value = _key_value(cand)
        return jnp.where(count(lambda s: s >= cand_value) >= topk, cand, key)

    thr_key = lax.fori_loop(0, 32, radix_step, jnp.full((1, tq), _INT_MIN, I32))
    thr = jnp.where(thr_key == _INT_MIN, -jnp.inf, _key_value(thr_key))
    need = topk - count(lambda s: s > thr)
    n_eq = count(lambda s: s == thr)
    tied = jnp.max(jnp.where((n_eq > need) & (thr_key != _INT_MIN), 1.0, 0.0)) > 0.0

    def madd_body(c, carry):
        keep = (keys_ref[chunk(c), :] >= thr) & causal(c)
        madd_ref[c] = jnp.where(keep, 0.0, _NEG).T
        return carry

    lax.fori_loop(0, n_ch, madd_body, 0)

    @pl.when(tied)
    def _():
        tri = (lax.broadcasted_iota(I32, (tq, tq), 1) <= lax.broadcasted_iota(I32, (tq, tq), 0)).astype(BF16)

        def tie_body(c, run):
            kc = keys_ref[chunk(c), :]
            eq = kc == thr
            pre = jnp.dot(tri, eq.astype(F32).astype(BF16), preferred_element_type=F32) + run
            keep = ((kc > thr) | (eq & (pre <= need))) & causal(c)
            madd_ref[c] = jnp.where(keep, 0.0, _NEG).T
            return run + jnp.sum(eq.astype(F32), axis=0, keepdims=True)

        lax.fori_loop(0, n_ch, tie_body, jnp.zeros((1, tq), F32))

    m_ref[...] = jnp.full(m_ref.shape, _NEG, F32)
    l_ref[...] = jnp.zeros(l_ref.shape, F32)
    acc_ref[...] = jnp.zeros(acc_ref.shape, F32)

    def attend(c, near):
        madd = madd_ref[c]
        for g in range(n_kv):
            q4 = q_ref[g * rep:(g + 1) * rep].reshape(rows, hd)
            s = _nt_dot(q4, k_ref[g, chunk(c), :])
            slabs = []
            for r in range(rep):
                for a in range(nb):
                    rs = slice(r * tq + a * _LANES, r * tq + (a + 1) * _LANES)
                    x = s[rs, :] + madd[a * _LANES:(a + 1) * _LANES, :]
                    if near is not None:
                        cols = []
                        for b in range(nb):
                            xb = x[:, b * _LANES:(b + 1) * _LANES]
                            kind = near * nb + a - b
                            if kind in (0, 1):
                                xb = xb + tb_ref[g * rep + r, kind]
                            cols.append(xb)
                        x = jnp.concatenate(cols, axis=1)
                    slabs.append(x)
            x = jnp.concatenate(slabs, axis=0)
            xm = x[:, 0:_LANES]
            for b in range(1, nb):
                xm = jnp.maximum(xm, x[:, b * _LANES:(b + 1) * _LANES])
            m_old = m_ref[g]
            m_new = jnp.maximum(m_old, jnp.max(xm, axis=-1, keepdims=True))
            alpha = jnp.exp2(m_old - m_new)
            p = jnp.exp2(x - jnp.concatenate([m_new] * nb, axis=1))
            ps = p[:, 0:_LANES]
            for b in range(1, nb):
                ps = ps + p[:, b * _LANES:(b + 1) * _LANES]
            l_ref[g] = alpha * l_ref[g] + ps
            acc_ref[g] = alpha * acc_ref[g] + jnp.dot(p.astype(_MXU_DTYPE), v_ref[g, chunk(c), :],
                                                      preferred_element_type=F32)
            m_ref[g] = m_new

    attend(i, 0)

    @pl.when(i >= 1)
    def _():
        attend(i - 1, 1)

    def far_body(c, carry):
        attend(c, None)
        return carry

    lax.fori_loop(0, jnp.maximum(i - 1, 0), far_body, 0)

    for g in range(n_kv):
        o = acc_ref[g] / jnp.sum(l_ref[g], axis=-1, keepdims=True)
        o = o * sg_ref[g * rep:(g + 1) * rep].reshape(rows, hd).astype(F32)
        o_ref[g * rep:(g + 1) * rep] = o.reshape(rep, tq, hd).astype(o_ref.dtype)


def _page_copies(pt_ref, cache_ref, buf_ref, sem_ref, b, first_page, n_copy, slot, page):
    return [pltpu.make_async_copy(cache_ref.at[pt_ref[b, first_page + p]],
                                  buf_ref.at[slot, pl.ds(p * page, page)],
                                  sem_ref.at[slot]) for p in range(n_copy)]


def _sample_scores_body(pt_ref, qi_ref, wk_ref, cache_ref, o_ref, buf_ref, wb_ref, sem_ref,
                        *, n_pages, page, idx_heads, t_dec):
    b = pl.program_id(0)
    nb = pl.num_programs(0)
    slot = b % 2
    copies = functools.partial(_page_copies, pt_ref, cache_ref, buf_ref, sem_ref,
                               first_page=0, n_copy=n_pages, page=page)

    @pl.when(b == 0)
    def _():
        for c in copies(b=0, slot=0):
            c.start()

    @pl.when(b + 1 < nb)
    def _():
        for c in copies(b=b + 1, slot=1 - slot):
            c.start()

    qi = qi_ref[...].reshape(idx_heads * t_dec, -1).astype(_MXU_DTYPE)
    w_blk = wk_ref[:, 0:_LANES]
    for h in range(idx_heads):
        wb_ref[h * t_dec:(h + 1) * t_dec, :] = jnp.broadcast_to(w_blk[:, h:h + 1], (t_dec, _LANES))

    def page_scores(keys):
        s = _nt_dot(qi, keys.astype(_MXU_DTYPE))
        return jnp.sum((jnp.maximum(s, 0.0) * wb_ref[...]).reshape(idx_heads, t_dec, page), axis=0)

    past = n_pages * page
    ki_new = jnp.concatenate([wk_ref[:, _LANES:], jnp.zeros((page - t_dec, wk_ref.shape[1] - _LANES), F32)], axis=0)
    row = lax.broadcasted_iota(I32, (t_dec, page), 0)
    col = lax.broadcasted_iota(I32, (t_dec, page), 1)
    o_ref[0, :, past:past + page] = jnp.where(col <= row, page_scores(ki_new), -jnp.inf)

    for c in copies(b=b, slot=slot):
        c.wait()
    for p in range(n_pages):
        o_ref[0, :, p * page:(p + 1) * page] = page_scores(buf_ref[slot, p * page:(p + 1) * page, :])


def _sample_scores(page_table, qi_s, wk_s, cache_idx, *, t_dec):
    bd, n_pages = page_table.shape
    _, page, idx_dim = cache_idx.shape
    idx_heads = qi_s.shape[0]
    past = n_pages * page
    grid_spec = pltpu.PrefetchScalarGridSpec(
        num_scalar_prefetch=1,
        grid=(bd,),
        in_specs=[pl.BlockSpec((idx_heads, t_dec, idx_dim), lambda b, pt: (0, b, 0)),
                  pl.BlockSpec((t_dec, wk_s.shape[1]), lambda b, pt: (b, 0)),
                  pl.BlockSpec(memory_space=pl.ANY)],
        out_specs=pl.BlockSpec((1, t_dec, past + page), lambda b, pt: (b, 0, 0)),
        scratch_shapes=[pltpu.VMEM((2, past, idx_dim), F32),
                        pltpu.VMEM((idx_heads * t_dec, _LANES), F32),
                        pltpu.SemaphoreType.DMA((2,))],
    )
    return pl.pallas_call(
        functools.partial(_sample_scores_body, n_pages=n_pages, page=page, idx_heads=idx_heads, t_dec=t_dec),
        grid_spec=grid_spec,
        out_shape=jax.ShapeDtypeStruct((bd, t_dec, past + page), F32),
        compiler_params=_cparams(1),
        name="sample_scores",
    )(page_table, qi_s, wk_s, cache_idx)


def _sample_select_body(sc_ref, o_ref, keys_ref, *, topk, past):
    bd, t_dec, n_cols = sc_ref.shape

    def valid(c0, width):
        col = lax.broadcasted_iota(I32, (bd, t_dec, width), 2) + c0
        return col <= past + lax.broadcasted_iota(I32, (bd, t_dec, width), 1)

    keys_ref[...] = jnp.where(valid(0, n_cols), sc_ref[...], -jnp.inf)

    def step(it, key):
        cand = key + jnp.left_shift(jnp.int32(1), 31 - it)
        cnt = jnp.sum((keys_ref[...] >= _key_value(cand)).astype(F32), axis=-1, keepdims=True)
        return jnp.where(cnt >= topk, cand, key)

    thr_key = lax.fori_loop(0, 32, step, jnp.full((bd, t_dec, 1), _INT_MIN, I32))
    thr = jnp.where(thr_key == _INT_MIN, -jnp.inf, _key_value(thr_key))
    keys = keys_ref[...]
    cnt_gt = jnp.sum((keys > thr).astype(F32), axis=-1, keepdims=True)
    cnt_eq = jnp.sum((keys == thr).astype(F32), axis=-1, keepdims=True)
    need = topk - cnt_gt
    o_ref[...] = ((keys >= thr) & valid(0, n_cols)).astype(F32)
    tied = jnp.max(jnp.where((cnt_eq > need) & (thr_key != _INT_MIN), 1.0, 0.0)) > 0.0

    @pl.when(tied)
    def _():
        tri = (lax.broadcasted_iota(I32, (_LANES, _LANES), 0)
               <= lax.broadcasted_iota(I32, (_LANES, _LANES), 1)).astype(BF16)
        run = jnp.zeros((bd, t_dec, 1), F32)
        for c0 in range(0, n_cols, _LANES):
            kc = keys_ref[:, :, c0:c0 + _LANES]
            eq = kc == thr
            eq2 = eq.astype(F32).reshape(bd * t_dec, _LANES).astype(BF16)
            pre = jnp.dot(eq2, tri, preferred_element_type=F32).reshape(bd, t_dec, _LANES) + run
            run = run + jnp.sum(eq.astype(F32), axis=-1, keepdims=True)
            keep = (kc > thr) | (eq & (pre <= need))
            o_ref[:, :, c0:c0 + _LANES] = (keep & valid(c0, _LANES)).astype(F32)


def _sample_select(scores, *, topk, past):
    return pl.pallas_call(
        functools.partial(_sample_select_body, topk=topk, past=past),
        out_shape=jax.ShapeDtypeStruct(scores.shape, F32),
        scratch_shapes=[pltpu.VMEM(scores.shape, F32)],
        compiler_params=pltpu.CompilerParams(vmem_limit_bytes=_VMEM_LIMIT),
        name="sample_select",
    )(scores)


def _sample_seq_attend(rb_ref, seq_id, q_ref, mask_ref, kn_ref, vn_ref, sg_ref, ts_ref, o_ref,
                       kbuf_ref, vbuf_ref, m_ref, l_ref, acc_ref, copies,
                       *, n_chunks, cpages, page, n_slots, n_kv, rep, t_dec, n_buckets):
    hd = q_ref.shape[-1]
    rows = rep * t_dec
    ck = cpages * page
    past = n_chunks * ck

    m_ref[...] = jnp.full(m_ref.shape, -jnp.inf, F32)
    l_ref[...] = jnp.zeros(l_ref.shape, F32)
    acc_ref[...] = jnp.zeros(acc_ref.shape, F32)

    def update(g, logits, mask, values):
        n = logits.shape[-1]
        x = jnp.where(jnp.broadcast_to(mask[None], (rep, t_dec, n)).reshape(rows, n) > 0.5, logits, -jnp.inf)
        rs = slice(g * rows, (g + 1) * rows)
        m_old = m_ref[rs, :]
        m_new = jnp.maximum(m_old, jnp.max(x, axis=-1, keepdims=True))
        m_safe = jnp.where(m_new == -jnp.inf, 0.0, m_new)
        alpha = jnp.exp(m_old - m_safe)
        p = jnp.exp(x - m_safe[:, 0:1])
        l_ref[rs, :] = alpha * l_ref[rs, :] + jnp.sum(p, axis=-1, keepdims=True)
        acc_ref[rs, :] = alpha * acc_ref[rs, :] + jnp.dot(p.astype(_MXU_DTYPE), values.astype(_MXU_DTYPE),
                                                          preferred_element_type=F32)
        m_ref[rs, :] = m_new

    def head_bias(g, kind):
        return ts_ref[g * rep:(g + 1) * rep, kind].reshape(rows, _LANES)

    def far_bias(g):
        r_idx = lax.broadcasted_iota(I32, (rows, 1), 0) // t_dec
        out = jnp.zeros((rows, 1), F32)
        for r in range(rep):
            out = jnp.where(r_idx == r, rb_ref[n_buckets - 1, g * rep + r], out)
        return out

    def queries(g):
        return q_ref[g * rep:(g + 1) * rep].reshape(rows, hd).astype(_MXU_DTYPE)

    for c in range(n_chunks):
        slot = c % n_slots
        for cp in copies(seq_id, c):
            cp.wait()
        for g in range(n_kv):
            far = far_bias(g)
            lg = _nt_dot(queries(g), kbuf_ref[slot, pl.ds(g, ck, stride=n_kv), :].astype(_MXU_DTYPE)) + far
            if c == n_chunks - 1:
                lg = jnp.concatenate([lg[:, :ck - page], lg[:, ck - page:] + (head_bias(g, 0) - far)], axis=1)
            update(g, lg, mask_ref[:, c * ck:(c + 1) * ck], vbuf_ref[slot, pl.ds(g, ck, stride=n_kv), :])
        if c + n_slots < n_chunks:
            for cp in copies(seq_id, c + n_slots):
                cp.start()

    zpad = jnp.zeros((page - t_dec, hd), F32)
    for g in range(n_kv):
        kn = jnp.concatenate([kn_ref[pl.ds(g, t_dec, stride=n_kv), :], zpad], axis=0)
        vn = jnp.concatenate([vn_ref[pl.ds(g, t_dec, stride=n_kv), :], zpad], axis=0)
        lg = _nt_dot(queries(g), kn.astype(_MXU_DTYPE)) + head_bias(g, 1)
        update(g, lg, mask_ref[:, past:past + page], vn)
        rs = slice(g * rows, (g + 1) * rows)
        o = acc_ref[rs, :] / l_ref[rs, :] * sg_ref[g * rep:(g + 1) * rep].reshape(rows, hd)
        o_ref[g * rep:(g + 1) * rep] = o.reshape(rep, t_dec, hd)


def _attn_body(pt_ref, rb_ref, qi_ref, wk_ref, ki_ref, q_ref, k_ref, v_ref, sg_ref, tb_ref,
               qs_ref, ms_ref, kn_ref, vn_ref, sgs_ref, ts_ref, ck_ref, cv_ref, o_ref, os_ref,
               wt_ref, sc_ref, keys_ref, madd_ref, m_ref, l_ref, acc_ref,
               kbuf_ref, vbuf_ref, sm_ref, sl_ref, sacc_ref, ksem_ref, vsem_ref, *, prompt, sample, n_seq):
    seq_id = pl.program_id(0) * pl.num_programs(1) + pl.program_id(1)
    has_seq = seq_id < n_seq
    prow = sample["page"] * sample["n_kv"]
    cpages, n_slots = sample["cpages"], sample["n_slots"]

    def copies(sid, c):
        return (_page_copies(pt_ref, ck_ref, kbuf_ref, ksem_ref, sid, c * cpages, cpages, c % n_slots, prow)
                + _page_copies(pt_ref, cv_ref, vbuf_ref, vsem_ref, sid, c * cpages, cpages, c % n_slots, prow))

    @pl.when(has_seq)
    def _():
        for c in range(min(n_slots, sample["n_chunks"])):
            for cp in copies(seq_id, c):
                cp.start()

    _prompt_attn_body(qi_ref, wk_ref, ki_ref, q_ref, k_ref, v_ref, sg_ref, tb_ref, o_ref,
                      wt_ref, sc_ref, keys_ref, madd_ref, m_ref, l_ref, acc_ref, **prompt)

    @pl.when(has_seq)
    def _():
        _sample_seq_attend(rb_ref, seq_id, qs_ref, ms_ref, kn_ref, vn_ref, sgs_ref, ts_ref, os_ref,
                           kbuf_ref, vbuf_ref, sm_ref, sl_ref, sacc_ref, copies, **sample)


def _attention(page_table, rel_bias, qi_hm, wk, ki_bf, q_hm, k_hm, v_hm, sg_hm, tb_p,
               q_s, mask_s, kn, vn, sg_s, tb_s, cache_k, cache_v, *, batch, seq, tq, t_dec, page, cpages, n_slots):
    idx_heads, m, idx_dim = qi_hm.shape
    n_heads, _, hd = q_hm.shape
    n_kv = k_hm.shape[0]
    rep = n_heads // n_kv
    n_q = seq // tq
    n_seq, n_pages = page_table.shape
    assert n_seq <= batch * n_q and n_pages % cpages == 0
    n_chunks = n_pages // cpages
    ck = cpages * page
    once = dict(pipeline_mode=pl.Buffered(1))
    qrow = lambda b, i, pt: (0, b * n_q + i, 0)
    srow = lambda b, i, pt: (0, jnp.minimum(b * n_q + i, n_seq - 1), 0)
    srow2 = lambda b, i, pt: (jnp.minimum(b * n_q + i, n_seq - 1), 0)
    prompt = dict(tq=tq, topk=min(_TOPK_MAX, seq // 4), idx_heads=idx_heads, n_kv=n_kv, rep=rep)
    sample = dict(n_chunks=n_chunks, cpages=cpages, page=page, n_slots=n_slots, n_kv=n_kv, rep=rep,
                  t_dec=t_dec, n_buckets=rel_bias.shape[0])
    grid_spec = pltpu.PrefetchScalarGridSpec(
        num_scalar_prefetch=1,
        grid=(batch, n_q),
        in_specs=[pl.BlockSpec(memory_space=pltpu.SMEM),
                  pl.BlockSpec((idx_heads, tq, idx_dim), qrow),
                  pl.BlockSpec((tq, _LANES), lambda b, i, pt: (b * n_q + i, 0)),
                  pl.BlockSpec((seq, idx_dim), lambda b, i, pt: (b, 0), **once),
                  pl.BlockSpec((n_heads, tq, hd), qrow),
                  pl.BlockSpec((n_kv, seq, hd), lambda b, i, pt: (0, b, 0), **once),
                  pl.BlockSpec((n_kv, seq, hd), lambda b, i, pt: (0, b, 0), **once),
                  pl.BlockSpec((n_heads, tq, hd), qrow),
                  pl.BlockSpec((n_heads, 2, _LANES, _LANES), lambda b, i, pt: (0, 0, 0, 0), **once),
                  pl.BlockSpec((n_heads, t_dec, hd), srow),
                  pl.BlockSpec((t_dec, mask_s.shape[1]), srow2),
                  pl.BlockSpec((t_dec * n_kv, hd), srow2),
                  pl.BlockSpec((t_dec * n_kv, hd), srow2),
                  pl.BlockSpec((n_heads, t_dec, hd), srow),
                  pl.BlockSpec((n_heads, 2, t_dec, _LANES), lambda b, i, pt: (0, 0, 0, 0), **once),
                  pl.BlockSpec(memory_space=pl.ANY),
                  pl.BlockSpec(memory_space=pl.ANY)],
        out_specs=[pl.BlockSpec((n_heads, tq, hd), qrow),
                   pl.BlockSpec((n_heads, t_dec, hd), srow)],
        scratch_shapes=[pltpu.VMEM((_LANES, tq), F32),
                        pltpu.VMEM((seq, tq), F32),
                        pltpu.VMEM((seq, tq), F32),
                        pltpu.VMEM((n_q, tq, tq), F32),
                        pltpu.VMEM((n_kv, rep * tq, _LANES), F32),
                        pltpu.VMEM((n_kv, rep * tq, _LANES), F32),
                        pltpu.VMEM((n_kv, rep * tq, hd), F32),
                        pltpu.VMEM((n_slots, ck * n_kv, hd), F32),
                        pltpu.VMEM((n_slots, ck * n_kv, hd), F32),
                        pltpu.VMEM((n_heads * t_dec, _LANES), F32),
                        pltpu.VMEM((n_heads * t_dec, _LANES), F32),
                        pltpu.VMEM((n_heads * t_dec, hd), F32),
                        pltpu.SemaphoreType.DMA((n_slots,)),
                        pltpu.SemaphoreType.DMA((n_slots,))],
    )
    return pl.pallas_call(
        functools.partial(_attn_body, prompt=prompt, sample=sample, n_seq=n_seq),
        grid_spec=grid_spec,
        out_shape=[jax.ShapeDtypeStruct((n_heads, m, hd), _MXU_DTYPE),
                   jax.ShapeDtypeStruct((n_heads, n_seq * t_dec, hd), F32)],
        compiler_params=_cparams(2),
        name="attention",
    )(page_table, rel_bias.astype(F32), qi_hm, wk, ki_bf, q_hm, k_hm, v_hm, sg_hm, tb_p,
      q_s, mask_s, kn, vn, sg_s, tb_s, cache_k, cache_v)


def _pool_body(u_ref, halo_ref, gb_ref, wp_ref, ps_ref, o_ref, ext_ref, *, tm, pos_base, own_halo, pg):
    i = pl.program_id(1)
    bb = u_ref.shape[0]
    halo = halo_ref[...]
    if own_halo:
        halo = jnp.where(i > 0, halo, 0.0)
    ext_ref[:, 0:_POOL_HALO, :] = halo
    ext_ref[:, _POOL_HALO:, :] = u_ref[...]
    pos = pos_base + i * tm + lax.broadcasted_iota(I32, (1, tm, 1), 1)
    for g, win in enumerate(_POOL_WINDOWS):
        cols = slice(g * pg, (g + 1) * pg)
        tot = ext_ref[:, _POOL_HALO:, cols]
        for k in range(1, win):
            tot = tot + ext_ref[:, _POOL_HALO - k:_POOL_HALO - k + tm, cols]
        cnt = jnp.minimum(pos + 1, win).astype(F32)
        d = tot / cnt - ext_ref[:, _POOL_HALO:, cols]
        y = jnp.dot(d.reshape(bb * tm, pg).astype(_MXU_DTYPE), wp_ref[g], preferred_element_type=F32)
        y = y.reshape(bb, tm, pg) * ps_ref[:, cols] * gb_ref[:, :, cols].astype(F32)
        o_ref[:, :, cols] = y.astype(o_ref.dtype)


def _pool(u, halo, gb, wp, ps, *, bb, tm, pos_base, own_halo, out_dtype):
    n_seq, rows, pw = u.shape
    pg = pw // len(_POOL_WINDOWS)
    if own_halo:
        hb = tm // _POOL_HALO
        halo_spec = pl.BlockSpec((bb, _POOL_HALO, pw), lambda b, i: (b, jnp.maximum(i * hb - 1, 0), 0))
    else:
        halo_spec = pl.BlockSpec((bb, _POOL_HALO, pw), lambda b, i: (b, 0, 0))
    return pl.pallas_call(
        functools.partial(_pool_body, tm=tm, pos_base=pos_base, own_halo=own_halo, pg=pg),
        grid=(n_seq // bb, rows // tm),
        in_specs=[pl.BlockSpec((bb, tm, pw), lambda b, i: (b, i, 0)),
                  halo_spec,
                  pl.BlockSpec((bb, tm, pw), lambda b, i: (b, i, 0)),
                  pl.BlockSpec(wp.shape, lambda b, i: (0, 0, 0)),
                  pl.BlockSpec((1, pw), lambda b, i: (0, 0))],
        out_specs=pl.BlockSpec((bb, tm, pw), lambda b, i: (b, i, 0)),
        out_shape=jax.ShapeDtypeStruct((n_seq, rows, pw), out_dtype),
        scratch_shapes=[pltpu.VMEM((bb, _POOL_HALO + tm, pw), F32)],
        compiler_params=_cparams(2),
        name="pool",
    )(u, halo, gb, wp, ps)


def _merge_body(ma_ref, mb_ref, w_ref, x_ref, g_ref, o_ref, mix_ref, *, n_j, tn):
    j = pl.program_id(1)
    n_heads, _, hd = ma_ref.shape

    @pl.when(j == 0)
    def _():
        for h in range(n_heads):
            mix_ref[:, h * hd:(h + 1) * hd] = ma_ref[h].astype(mix_ref.dtype)
        mix_ref[:, n_heads * hd:] = mb_ref[...].astype(mix_ref.dtype)

    hv = x_ref[...] + jnp.dot(mix_ref[...], w_ref[...], preferred_element_type=F32)
    for jj in range(n_j):
        @pl.when(j == jj)
        def _(jj=jj):
            o_ref[:, jj * tn:(jj + 1) * tn] = hv

    @pl.when(j == n_j - 1)
    def _():
        ssq = jnp.zeros((o_ref.shape[0], 1), F32)
        for jj in range(n_j):
            hj = o_ref[:, jj * tn:(jj + 1) * tn]
            ssq = ssq + jnp.sum(hj * hj, axis=-1, keepdims=True)
        inv = lax.rsqrt(ssq / (n_j * tn) + _RMS_EPS)
        for jj in range(n_j):
            cols = slice(jj * tn, (jj + 1) * tn)
            o_ref[:, cols] = o_ref[:, cols] * inv * g_ref[:, cols]


def _merge(mix_a, mix_b, w_out, x2d, final_g, *, tm, tn):
    n_heads, m, hd = mix_a.shape
    pw = mix_b.shape[1]
    mixw, d = w_out.shape
    n_j = d // tn
    return pl.pallas_call(
        functools.partial(_merge_body, n_j=n_j, tn=tn),
        grid=(m // tm, n_j),
        in_specs=[pl.BlockSpec((n_heads, tm, hd), lambda i, j: (0, i, 0)),
                  pl.BlockSpec((tm, pw), lambda i, j: (i, 0)),
                  pl.BlockSpec((mixw, tn), lambda i, j: (0, j)),
                  pl.BlockSpec((tm, tn), lambda i, j: (i, j)),
                  pl.BlockSpec((1, d), lambda i, j: (0, 0))],
        out_specs=pl.BlockSpec((tm, d), lambda i, j: (i, 0)),
        out_shape=jax.ShapeDtypeStruct((m, d), F32),
        scratch_shapes=[pltpu.VMEM((tm, mixw), _MXU_DTYPE)],
        compiler_params=_cparams(2),
        name="merge",
    )(mix_a, mix_b, w_out, x2d, final_g.reshape(1, d).astype(F32))


def kernel(x_prompt, x_sample, cache_k, cache_v, cache_idx_k, state_pool, page_table,
           rel_bias, ln_g, w_in, w_pool, pool_scale, w_out, final_g):
    depth = w_in.shape[0]
    assert depth == 1, "single-layer step only"
    batch, seq, d_model = x_prompt.shape
    bd, t_dec, _ = x_sample.shape
    _, n_phys, page, n_kv, hd = cache_k.shape
    idx_dim = cache_idx_k.shape[-1]
    n_buckets, n_heads = rel_bias.shape
    pw = pool_scale.shape[-1]
    n_pg = w_pool.shape[1]
    attn_w, kv_w = n_heads * hd, n_kv * hd
    in_w = w_in.shape[-1]
    idx_heads = (in_w - 2 * attn_w - 2 * kv_w - idx_dim - 2 * pw) // (idx_dim + 1)
    n_pages = page_table.shape[1]
    past = n_pages * page
    mp, ms = batch * seq, bd * t_dec
    tq = _PROMPT_Q
    assert hd == _LANES and idx_dim == _LANES and idx_heads <= _LANES and n_pg == len(_POOL_WINDOWS)
    assert page >= _MAX_DISTANCE and _LANES >= _MAX_DISTANCE and seq % (2 * tq) == 0 and t_dec <= _POOL_HALO - 1
    assert state_pool.shape[2] == _POOL_HALO - 1

    o_q, o_k, o_v = 0, attn_w, attn_w + kv_w
    o_ga = o_v + kv_w
    o_qi = o_ga + attn_w
    o_wi = o_qi + idx_heads * idx_dim
    o_ki = o_wi + idx_heads
    o_u = o_ki + idx_dim
    o_gb = o_u + pw
    wt = jnp.transpose(w_in[0])

    xn_p = _rmsnorm(x_prompt.reshape(mp, d_model), ln_g[0], _MXU_DTYPE)
    xn_s = _rmsnorm(x_sample.reshape(ms, d_model), ln_g[0], _MXU_DTYPE)

    tm = min(_PROJ_ROWS, mp)
    tw = min(_PROJ_ROWS_WIDE, mp)
    tn = lambda n: min(_PROJ_COLS, n)
    rm = lambda n, t, dt, s, rows=tm: _row_major(ms if s else mp, n, rows, t, dt, s)
    hm = lambda n, t, dt, s, rows=tm: _head_major(ms if s else mp, n, rows, t, dt, s)

    def proj(lo, hi, t, rows=tm, **kw):
        assert (hi - lo) % t == 0
        return _proj_call(xn_p, xn_s, wt, [(lo, t)], (hi - lo) // t, rows, **kw)

    def scaled_heads(scale):
        def epi(acc, refs):
            _store_heads(refs[0], acc * scale)
        return epi

    def silu_heads(acc, refs):
        _store_heads(refs[0], _silu(acc))

    def silu_rows(acc, refs):
        refs[0][...] = _silu(acc).astype(refs[0].dtype)

    def plain_rows(acc, refs):
        refs[0][...] = acc.astype(refs[0].dtype)

    def kv_rows(acc, ref):
        for g in range(n_kv):
            ref[pl.ds(g, acc.shape[0], stride=n_kv), :] = acc[:, g * hd:(g + 1) * hd]

    def kv_p(acc, refs):
        kv_rows(acc, refs[0])
        _store_heads(refs[1], acc)

    def kv_s(acc, refs):
        kv_rows(acc, refs[0])

    def wk_p(acc, refs):
        refs[0][...] = jnp.concatenate([acc[:, :_LANES] * idx_heads ** -0.5, acc[:, _LANES:]], axis=1)
        refs[1][...] = acc[:, _LANES:].astype(refs[1].dtype)

    def wk_s(acc, refs):
        refs[0][...] = jnp.concatenate([acc[:, :_LANES] * idx_heads ** -0.5, acc[:, _LANES:]], axis=1)

    t = tn(attn_w)
    q_hm, q_s = proj(o_q, o_k, t, tw, outs_p=[hm(attn_w, t, _MXU_DTYPE, False, tw)],
                     outs_s=[hm(attn_w, t, F32, True)],
                     epi_p=scaled_heads(hd ** -0.5 * _LOG2E), epi_s=scaled_heads(hd ** -0.5), name="proj_q")
    kv_outs = dict(outs_p=[((mp * n_kv, hd), F32, (tm * n_kv, hd), lambda j, i: (i, 0)),
                           ((n_kv, mp, hd), _MXU_DTYPE, (n_kv, tm, hd), lambda j, i: (0, i, 0))],
                   outs_s=[((ms * n_kv, hd), F32, (ms * n_kv, hd), lambda j, i: (0, 0))], epi_p=kv_p, epi_s=kv_s)
    k_p, k_hm, k_s = proj(o_k, o_v, kv_w, name="proj_k", **kv_outs)
    v_p, v_hm, v_s = proj(o_v, o_ga, kv_w, name="proj_v", **kv_outs)
    sga_hm, sga_s = proj(o_ga, o_qi, t, tw, outs_p=[hm(attn_w, t, _MXU_DTYPE, False, tw)],
                         outs_s=[hm(attn_w, t, F32, True)], epi_p=silu_heads, epi_s=silu_heads, name="proj_ga")
    t = tn(idx_heads * idx_dim)
    qi_hm, qi_s = proj(o_qi, o_wi, t, tw, outs_p=[hm(idx_heads * idx_dim, t, _MXU_DTYPE, False, tw)],
                       outs_s=[hm(idx_heads * idx_dim, t, F32, True)],
                       epi_p=scaled_heads(idx_dim ** -0.5), epi_s=scaled_heads(idx_dim ** -0.5), name="proj_qi")
    t = 2 * _LANES
    wk_pr, ki_bf, wk_sm = _proj_call(
        xn_p, xn_s, wt, [(o_wi, _LANES), (o_ki, idx_dim)], 1, tm,
        outs_p=[((mp, t), F32, (tm, t), lambda j, i: (i, 0)), ((mp, idx_dim), _MXU_DTYPE, (tm, idx_dim), lambda j, i: (i, 0))],
        outs_s=[((ms, t), F32, (ms, t), lambda j, i: (0, 0))],
        epi_p=wk_p, epi_s=wk_s, name="proj_wk")
    t = tn(pw)
    u_p, u_s = proj(o_u, o_gb, t, outs_p=[rm(pw, t, F32, False)], outs_s=[rm(pw, t, F32, True)],
                    epi_p=plain_rows, epi_s=plain_rows, name="proj_u")
    sgb_p, sgb_s = proj(o_gb, in_w, t, tw, outs_p=[rm(pw, t, _MXU_DTYPE, False, tw)], outs_s=[rm(pw, t, F32, True)],
                        epi_p=silu_rows, epi_s=silu_rows, name="proj_gb")

    tb_p, tb_s = _bias_tables(rel_bias, _LANES, t_dec, page)

    scores = _sample_scores(page_table, qi_s, wk_sm, cache_idx_k[0], t_dec=t_dec)
    topk_s = min(_TOPK_MAX, (past + t_dec) // 4)
    mask_s = _sample_select(scores, topk=topk_s, past=past).reshape(ms, past + page)
    cpages = max(1, min(_PAGES_PER_CHUNK, n_pages // 4))
    mixa_p, mixa_s = _attention(
        page_table, rel_bias, qi_hm, wk_pr, ki_bf, q_hm, k_hm, v_hm, sga_hm, tb_p,
        q_s, mask_s, k_s, v_s, sga_s, tb_s,
        cache_k[0].reshape(n_phys, page * n_kv, hd), cache_v[0].reshape(n_phys, page * n_kv, hd),
        batch=batch, seq=seq, tq=tq, t_dec=t_dec, page=page, cpages=cpages, n_slots=_PAGE_SLOTS)

    wp = w_pool[0].astype(_MXU_DTYPE)
    ps = pool_scale[0].reshape(1, pw).astype(F32)
    u_p3 = u_p.reshape(batch, seq, pw)
    mixb_p = _pool(u_p3, u_p3, sgb_p.reshape(batch, seq, pw), wp, ps, bb=1, tm=min(_POOL_ROWS, seq), pos_base=0,
                   own_halo=True, out_dtype=_MXU_DTYPE).reshape(mp, pw)
    halo_s = jnp.concatenate([jnp.zeros((bd, 1, pw), F32), state_pool[0].astype(F32)], axis=1)
    mixb_s = _pool(u_s.reshape(bd, t_dec, pw), halo_s, sgb_s.reshape(bd, t_dec, pw), wp, ps, bb=bd, tm=t_dec,
                   pos_base=past, own_halo=False, out_dtype=F32).reshape(ms, pw)

    w_o = w_out[0].astype(_MXU_DTYPE)
    tn_o = min(_MERGE_COLS, d_model)
    y_p = _merge(mixa_p, mixb_p, w_o, x_prompt.reshape(mp, d_model), final_g, tm=min(_MERGE_ROWS, mp), tn=tn_o)
    y_s = _merge(mixa_s, mixb_s, w_o, x_sample.reshape(ms, d_model), final_g, tm=ms, tn=tn_o)

    n_state = state_pool.shape[2]
    u_full_s = jnp.concatenate([state_pool[0].astype(F32), u_s.reshape(bd, t_dec, pw)], axis=1)
    return (y_p.reshape(batch, seq, d_model), y_s.reshape(bd, t_dec, d_model),
            k_p.reshape(1, batch, seq, n_kv, hd), v_p.reshape(1, batch, seq, n_kv, hd),
            wk_pr[:, _LANES:].reshape(1, batch, seq, idx_dim),
            u_p.reshape(batch, seq, pw)[:, seq - n_state:][None],
            k_s.reshape(1, bd, t_dec, n_kv, hd), v_s.reshape(1, bd, t_dec, n_kv, hd),
            wk_sm[:, _LANES:].reshape(1, bd, t_dec, idx_dim),
            u_full_s[:, t_dec:][None])
```

```python
import functools
import math

import jax
import jax.numpy as jnp
import numpy as np
from jax import lax
from jax.experimental import pallas as pl
from jax.experimental.pallas import tpu as pltpu

F32 = jnp.float32
BF16 = jnp.bfloat16
I32 = jnp.int32

_TOPK_MAX = 256
_POOL_WINDOWS = (2, 4, 8, 16)
_POOL_HALO = 16
_MAX_DISTANCE = 128
_RMS_EPS = 1e-6
_NEG = -1e30
_LOG2E = math.log2(math.e)
_INT_MIN = -(2 ** 31)
_LANES = 128
_SUBLANES = 8
_VMEM_LIMIT = 58 * 1024 * 1024
_MXU_DTYPE = BF16

_RMS_ROWS = 512
_PROJ_ROWS = 1024
_PROJ_ROWS_WIDE = 2048
_PROJ_COLS = 512
_PROMPT_Q = 2 * _LANES
_PAGES_PER_CHUNK = 16
_PAGE_SLOTS = 3
_POOL_ROWS = 512
_MERGE_ROWS = 512
_MERGE_COLS = 1024


def _cparams(n_axes):
    return pltpu.CompilerParams(dimension_semantics=("arbitrary",) * n_axes,
                                vmem_limit_bytes=_VMEM_LIMIT)


def _nt_dot(a, b):
    return lax.dot_general(a, b, (((1,), (1,)), ((), ())), preferred_element_type=F32)


def _key_value(key):
    return lax.bitcast_convert_type(key ^ ((key >> 31) & 0x7FFFFFFF), F32)


def _rmsnorm_body(x_ref, g_ref, o_ref):
    x = x_ref[...].astype(F32)
    var = jnp.mean(x * x, axis=-1, keepdims=True)
    o_ref[...] = (x * lax.rsqrt(var + _RMS_EPS) * g_ref[...]).astype(o_ref.dtype)


def _rmsnorm(x2d, g, out_dtype):
    m, d = x2d.shape
    tm = min(_RMS_ROWS, m)
    return pl.pallas_call(
        _rmsnorm_body,
        grid=(m // tm,),
        in_specs=[pl.BlockSpec((tm, d), lambda i: (i, 0)), pl.BlockSpec((1, d), lambda i: (0, 0))],
        out_specs=pl.BlockSpec((tm, d), lambda i: (i, 0)),
        out_shape=jax.ShapeDtypeStruct((m, d), out_dtype),
        compiler_params=_cparams(1),
        name="rmsnorm",
    )(x2d, g.reshape(1, d).astype(F32))


def _proj_body(xp_ref, xs_ref, wt_ref, *refs, w_rows, n_i, n_p, n_out, epi_p, epi_s):
    out_refs, wbuf_ref, wc_ref, sem_ref = refs[:n_out], refs[n_out], refs[n_out + 1], refs[n_out + 2]
    j, i = pl.program_id(0), pl.program_id(1)

    def w_copies(jj):
        out, r0 = [], 0
        for first, cnt in w_rows:
            src = wt_ref.at[pl.ds(pl.multiple_of(first + jj * cnt, _SUBLANES), cnt)]
            out.append(pltpu.make_async_copy(src, wbuf_ref.at[pl.ds(r0, cnt)], sem_ref.at[0]))
            r0 += cnt
        return out

    @pl.when((j == 0) & (i == 0))
    def _():
        for c in w_copies(0):
            c.start()

    @pl.when(i == 0)
    def _():
        for c in w_copies(j):
            c.wait()
        wc_ref[...] = wbuf_ref[...].astype(wc_ref.dtype)

    @pl.when((i == min(1, n_i - 1)) & (j + 1 < pl.num_programs(0)))
    def _():
        for c in w_copies(j + 1):
            c.start()

    epi_p(_nt_dot(xp_ref[...], wc_ref[...]), out_refs[:n_p])

    @pl.when(i == 0)
    def _():
        epi_s(_nt_dot(xs_ref[...], wc_ref[...]), out_refs[n_p:])


def _proj_call(xn_p, xn_s, wt, w_rows, n_tiles, tm, outs_p, outs_s, epi_p, epi_s, name):
    mp, d = xn_p.shape
    ms = xn_s.shape[0]
    assert mp % tm == 0 and all(first % _SUBLANES == 0 and cnt % _SUBLANES == 0 for first, cnt in w_rows)
    tn = sum(cnt for _, cnt in w_rows)
    outs = list(outs_p) + list(outs_s)
    return pl.pallas_call(
        functools.partial(_proj_body, w_rows=tuple(w_rows), n_i=mp // tm, n_p=len(outs_p), n_out=len(outs),
                          epi_p=epi_p, epi_s=epi_s),
        grid=(n_tiles, mp // tm),
        in_specs=[pl.BlockSpec((tm, d), lambda j, i: (i, 0)),
                  pl.BlockSpec((ms, d), lambda j, i: (0, 0), pipeline_mode=pl.Buffered(1)),
                  pl.BlockSpec(memory_space=pl.ANY)],
        out_specs=[pl.BlockSpec(o[2], o[3]) for o in outs],
        out_shape=[jax.ShapeDtypeStruct(o[0], o[1]) for o in outs],
        scratch_shapes=[pltpu.VMEM((tn, d), wt.dtype), pltpu.VMEM((tn, d), xn_p.dtype),
                        pltpu.SemaphoreType.DMA((1,))],
        compiler_params=_cparams(2),
        name=name,
    )(xn_p, xn_s, wt)


def _row_major(m, ncols, tm, tn, dtype, sample):
    if sample:
        return ((m, ncols), dtype, (m, tn), lambda j, i: (0, j))
    return ((m, ncols), dtype, (tm, tn), lambda j, i: (i, j))


def _head_major(m, ncols, tm, tn, dtype, sample):
    nh, hb = ncols // _LANES, tn // _LANES
    if sample:
        return ((nh, m, _LANES), dtype, (hb, m, _LANES), lambda j, i: (j, 0, 0))
    return ((nh, m, _LANES), dtype, (hb, tm, _LANES), lambda j, i: (j, i, 0))


def _store_heads(ref, val):
    for hh in range(ref.shape[0]):
        ref[hh] = val[:, hh * _LANES:(hh + 1) * _LANES].astype(ref.dtype)


def _silu(x):
    return x * jax.nn.sigmoid(x)


def _bucket_steps(n_buckets):
    max_exact = n_buckets // 2
    d = np.arange(max_exact, 2 * _MAX_DISTANCE + 1)
    tabs = []
    for dt in (np.float32, np.float64):
        v = np.log(d.astype(dt) / dt(max_exact)) / dt(math.log(_MAX_DISTANCE / max_exact)) * dt(n_buckets - max_exact)
        tabs.append(np.minimum(v.astype(np.int64), n_buckets - 1 - max_exact))
    assert (tabs[0] == tabs[1]).all()
    return [int(d[np.argmax(tabs[1] >= j)]) for j in range(1, n_buckets - max_exact)]


def _bias_tables_body(rb_ref, tp_ref, ts_ref, *, n_buckets, tq, t_dec, page):
    h = pl.program_id(0)
    max_exact = n_buckets // 2
    steps = _bucket_steps(n_buckets)

    def table(dist):
        d = jnp.maximum(dist, 0)
        bucket = jnp.minimum(d, max_exact)
        for first in steps:
            bucket = bucket + (d >= first).astype(I32)
        out = jnp.zeros(dist.shape, F32)
        for b in range(n_buckets):
            out = jnp.where(bucket == b, rb_ref[b, h], out)
        return out

    rp = lax.broadcasted_iota(I32, (tq, _LANES), 0)
    cp = lax.broadcasted_iota(I32, (tq, _LANES), 1)
    far = rb_ref[n_buckets - 1, h]
    tp_ref[0, 0] = (table(rp - cp) - far) * _LOG2E
    tp_ref[0, 1] = (table(rp - cp + tq) - far) * _LOG2E
    rs = lax.broadcasted_iota(I32, (t_dec, _LANES), 0)
    cs = lax.broadcasted_iota(I32, (t_dec, _LANES), 1)
    ts_ref[0, 0] = table(rs - cs + page)
    ts_ref[0, 1] = table(rs - cs)


def _bias_tables(rel_bias, tq, t_dec, page):
    n_buckets, n_heads = rel_bias.shape
    return pl.pallas_call(
        functools.partial(_bias_tables_body, n_buckets=n_buckets, tq=tq, t_dec=t_dec, page=page),
        grid=(n_heads,),
        in_specs=[pl.BlockSpec(memory_space=pltpu.SMEM)],
        out_specs=[pl.BlockSpec((1, 2, tq, _LANES), lambda h: (h, 0, 0, 0)),
                   pl.BlockSpec((1, 2, t_dec, _LANES), lambda h: (h, 0, 0, 0))],
        out_shape=[jax.ShapeDtypeStruct((n_heads, 2, tq, _LANES), F32),
                   jax.ShapeDtypeStruct((n_heads, 2, t_dec, _LANES), F32)],
        compiler_params=_cparams(1),
        name="bias_tables",
    )(rel_bias.astype(F32))


def _prompt_attn_body(qi_ref, wk_ref, ki_ref, q_ref, k_ref, v_ref, sg_ref, tb_ref, o_ref,
                      wt_ref, sc_ref, keys_ref, madd_ref, m_ref, l_ref, acc_ref,
                      *, tq, topk, idx_heads, n_kv, rep):
    i = pl.program_id(1)
    n_ch = i + 1
    hd = q_ref.shape[-1]
    rows = rep * tq
    nb = tq // _LANES
    hgrp = 2
    kc2 = 2 * tq

    def chunk(c):
        return pl.ds(pl.multiple_of(c * tq, tq), tq)

    def causal(c):
        key_pos = lax.broadcasted_iota(I32, (tq, tq), 0) + c * tq
        return key_pos <= lax.broadcasted_iota(I32, (tq, tq), 1) + i * tq

    wt_ref[...] = wk_ref[...].T

    def score_body(c2, carry):
        ks = pl.ds(pl.multiple_of(c2 * kc2, kc2), kc2)
        keys = ki_ref[ks, :]
        a = jnp.zeros((kc2, tq), F32)
        for hg in range(idx_heads // hgrp):
            qi2 = qi_ref[hg * hgrp:(hg + 1) * hgrp].reshape(hgrp * tq, -1)
            s = _nt_dot(keys, qi2)
            for r in range(hgrp):
                h = hg * hgrp + r
                a = a + jnp.maximum(s[:, r * tq:(r + 1) * tq], 0.0) * wt_ref[h:h + 1, :]
        sc_ref[ks, :] = a
        return carry

    lax.fori_loop(0, (n_ch + 1) // 2, score_body, 0)

    def key_body(c, carry):
        keys_ref[chunk(c), :] = jnp.where(causal(c), sc_ref[chunk(c), :], -jnp.inf)
        return carry

    lax.fori_loop(0, n_ch, key_body, 0)

    def count(pred):
        def hits(c):
            hit = pred(keys_ref[chunk(c), :]).astype(F32)
            return jnp.sum(hit.reshape(tq // 8, 8, tq), axis=0)

        def pair_body(c2, acc):
            return acc[0] + hits(2 * c2), acc[1] + hits(2 * c2 + 1)

        zero = jnp.zeros((8, tq), F32)
        even, odd = lax.fori_loop(0, n_ch // 2, pair_body, (zero, zero))
        last = lax.cond(n_ch % 2 == 1, lambda: hits(n_ch - 1), lambda: zero)
        return jnp.sum(even + odd + last, axis=0, keepdims=True)

    def radix_step(it, key):
        cand = key + jnp.left_shift(jnp.int32(1), 31 - it)
        cand_value = _key_value(cand)
        return jnp.where(count(lambda s: s >= cand_value) >= topk, cand, key)

    thr_key = lax.fori_loop(0, 32, radix_step, jnp.full((1, tq), _INT_MIN, I32))
    thr = jnp.where(thr_key == _INT_MIN, -jnp.inf, _key_value(thr_key))
    need = topk - count(lambda s: s > thr)
    n_eq = count(lambda s: s == thr)
    tied = jnp.max(jnp.where((n_eq > need) & (thr_key != _INT_MIN), 1.0, 0.0)) > 0.0

    def madd_body(c, carry):
        keep = (keys_ref[chunk(c), :] >= thr) & causal(c)
        madd_ref[c] = jnp.where(keep, 0.0, _NEG).T
        return carry

    lax.fori_loop(0, n_ch, madd_body, 0)

    @pl.when(tied)
    def _():
        tri = (lax.broadcasted_iota(I32, (tq, tq), 1) <= lax.broadcasted_iota(I32, (tq, tq), 0)).astype(BF16)

        def tie_body(c, run):
            kc = keys_ref[chunk(c), :]
            eq = kc == thr
            pre = jnp.dot(tri, eq.astype(F32).astype(BF16), preferred_element_type=F32) + run
            keep = ((kc > thr) | (eq & (pre <= need))) & causal(c)
            madd_ref[c] = jnp.where(keep, 0.0, _NEG).T
            return run + jnp.sum(eq.astype(F32), axis=0, keepdims=True)

        lax.fori_loop(0, n_ch, tie_body, jnp.zeros((1, tq), F32))

    m_ref[...] = jnp.full(m_ref.shape, _NEG, F32)
    l_ref[...] = jnp.zeros(l_ref.shape, F32)
    acc_ref[...] = jnp.zeros(acc_ref.shape, F32)

    def attend(c, near):
        madd = madd_ref[c]
        for g in range(n_kv):
            q4 = q_ref[g * rep:(g + 1) * rep].reshape(rows, hd)
            s = _nt_dot(q4, k_ref[g, chunk(c), :])
            slabs = []
            for r in range(rep):
                for a in range(nb):
                    rs = slice(r * tq + a * _LANES, r * tq + (a + 1) * _LANES)
                    x = s[rs, :] + madd[a * _LANES:(a + 1) * _LANES, :]
                    if near is not None:
                        cols = []
                        for b in range(nb):
                            xb = x[:, b * _LANES:(b + 1) * _LANES]
                            kind = near * nb + a - b
                            if kind in (0, 1):
                                xb = xb + tb_ref[g * rep + r, kind]
                            cols.append(xb)
                        x = jnp.concatenate(cols, axis=1)
                    slabs.append(x)
            x = jnp.concatenate(slabs, axis=0)
            xm = x[:, 0:_LANES]
            for b in range(1, nb):
                xm = jnp.maximum(xm, x[:, b * _LANES:(b + 1) * _LANES])
            m_old = m_ref[g]
            m_new = jnp.maximum(m_old, jnp.max(xm, axis=-1, keepdims=True))
            alpha = jnp.exp2(m_old - m_new)
            p = jnp.exp2(x - jnp.concatenate([m_new] * nb, axis=1))
            ps = p[:, 0:_LANES]
            for b in range(1, nb):
                ps = ps + p[:, b * _LANES:(b + 1) * _LANES]
            l_ref[g] = alpha * l_ref[g] + ps
            acc_ref[g] = alpha * acc_ref[g] + jnp.dot(p.astype(_MXU_DTYPE), v_ref[g, chunk(c), :],
                                                      preferred_element_type=F32)
            m_ref[g] = m_new

    attend(i, 0)

    @pl.when(i >= 1)
    def _():
        attend(i - 1, 1)

    def far_body(c, carry):
        attend(c, None)
        return carry

    lax.fori_loop(0, jnp.maximum(i - 1, 0), far_body, 0)

    for g in range(n_kv):
        o = acc_ref[g] / jnp.sum(l_ref[g], axis=-1, keepdims=True)
        o = o * sg_ref[g * rep:(g + 1) * rep].reshape(rows, hd).astype(F32)
        o_ref[g * rep:(g + 1) * rep] = o.reshape(rep, tq, hd).astype(o_ref.dtype)


def _page_copies(pt_ref, cache_ref, buf_ref, sem_ref, b, first_page, n_copy, slot, page):
    return [pltpu.make_async_copy(cache_ref.at[pt_ref[b, first_page + p]],
                                  buf_ref.at[slot, pl.ds(p * page, page)],
                                  sem_ref.at[slot]) for p in range(n_copy)]


def _sample_scores_body(pt_ref, qi_ref, wk_ref, cache_ref, o_ref, buf_ref, wb_ref, sem_ref,
                        *, n_pages, page, idx_heads, t_dec):
    b = pl.program_id(0)
    nb = pl.num_programs(0)
    slot = b % 2
    copies = functools.partial(_page_copies, pt_ref, cache_ref, buf_ref, sem_ref,
                               first_page=0, n_copy=n_pages, page=page)

    @pl.when(b == 0)
    def _():
        for c in copies(b=0, slot=0):
            c.start()

    @pl.when(b + 1 < nb)
    def _():
        for c in copies(b=b + 1, slot=1 - slot):
            c.start()

    qi = qi_ref[...].reshape(idx_heads * t_dec, -1).astype(_MXU_DTYPE)
    w_blk = wk_ref[:, 0:_LANES]
    for h in range(idx_heads):
        wb_ref[h * t_dec:(h + 1) * t_dec, :] = jnp.broadcast_to(w_blk[:, h:h + 1], (t_dec, _LANES))

    def page_scores(keys):
        s = _nt_dot(qi, keys.astype(_MXU_DTYPE))
        return jnp.sum((jnp.maximum(s, 0.0) * wb_ref[...]).reshape(idx_heads, t_dec, page), axis=0)

    past = n_pages * page
    ki_new = jnp.concatenate([wk_ref[:, _LANES:], jnp.zeros((page - t_dec, wk_ref.shape[1] - _LANES), F32)], axis=0)
    row = lax.broadcasted_iota(I32, (t_dec, page), 0)
    col = lax.broadcasted_iota(I32, (t_dec, page), 1)
    o_ref[0, :, past:past + page] = jnp.where(col <= row, page_scores(ki_new), -jnp.inf)

    for c in copies(b=b, slot=slot):
        c.wait()
    for p in range(n_pages):
        o_ref[0, :, p * page:(p + 1) * page] = page_scores(buf_ref[slot, p * page:(p + 1) * page, :])


def _sample_scores(page_table, qi_s, wk_s, cache_idx, *, t_dec):
    bd, n_pages = page_table.shape
    _, page, idx_dim = cache_idx.shape
    idx_heads = qi_s.shape[0]
    past = n_pages * page
    grid_spec = pltpu.PrefetchScalarGridSpec(
        num_scalar_prefetch=1,
        grid=(bd,),
        in_specs=[pl.BlockSpec((idx_heads, t_dec, idx_dim), lambda b, pt: (0, b, 0)),
                  pl.BlockSpec((t_dec, wk_s.shape[1]), lambda b, pt: (b, 0)),
                  pl.BlockSpec(memory_space=pl.ANY)],
        out_specs=pl.BlockSpec((1, t_dec, past + page), lambda b, pt: (b, 0, 0)),
        scratch_shapes=[pltpu.VMEM((2, past, idx_dim), F32),
                        pltpu.VMEM((idx_heads * t_dec, _LANES), F32),
                        pltpu.SemaphoreType.DMA((2,))],
    )
    return pl.pallas_call(
        functools.partial(_sample_scores_body, n_pages=n_pages, page=page, idx_heads=idx_heads, t_dec=t_dec),
        grid_spec=grid_spec,
        out_shape=jax.ShapeDtypeStruct((bd, t_dec, past + page), F32),
        compiler_params=_cparams(1),
        name="sample_scores",
    )(page_table, qi_s, wk_s, cache_idx)


def _sample_select_body(sc_ref, o_ref, keys_ref, *, topk, past):
    bd, t_dec, n_cols = sc_ref.shape

    def valid(c0, width):
        col = lax.broadcasted_iota(I32, (bd, t_dec, width), 2) + c0
        return col <= past + lax.broadcasted_iota(I32, (bd, t_dec, width), 1)

    keys_ref[...] = jnp.where(valid(0, n_cols), sc_ref[...], -jnp.inf)

    def step(it, key):
        cand = key + jnp.left_shift(jnp.int32(1), 31 - it)
        cnt = jnp.sum((keys_ref[...] >= _key_value(cand)).astype(F32), axis=-1, keepdims=True)
        return jnp.where(cnt >= topk, cand, key)

    thr_key = lax.fori_loop(0, 32, step, jnp.full((bd, t_dec, 1), _INT_MIN, I32))
    thr = jnp.where(thr_key == _INT_MIN, -jnp.inf, _key_value(thr_key))
    keys = keys_ref[...]
    cnt_gt = jnp.sum((keys > thr).astype(F32), axis=-1, keepdims=True)
    cnt_eq = jnp.sum((keys == thr).astype(F32), axis=-1, keepdims=True)
    need = topk - cnt_gt
    o_ref[...] = ((keys >= thr) & valid(0, n_cols)).astype(F32)
    tied = jnp.max(jnp.where((cnt_eq > need) & (thr_key != _INT_MIN), 1.0, 0.0)) > 0.0

    @pl.when(tied)
    def _():
        tri = (lax.broadcasted_iota(I32, (_LANES, _LANES), 0)
               <= lax.broadcasted_iota(I32, (_LANES, _LANES), 1)).astype(BF16)
        run = jnp.zeros((bd, t_dec, 1), F32)
        for c0 in range(0, n_cols, _LANES):
            kc = keys_ref[:, :, c0:c0 + _LANES]
            eq = kc == thr
            eq2 = eq.astype(F32).reshape(bd * t_dec, _LANES).astype(BF16)
            pre = jnp.dot(eq2, tri, preferred_element_type=F32).reshape(bd, t_dec, _LANES) + run
            run = run + jnp.sum(eq.astype(F32), axis=-1, keepdims=True)
            keep = (kc > thr) | (eq & (pre <= need))
            o_ref[:, :, c0:c0 + _LANES] = (keep & valid(c0, _LANES)).astype(F32)


def _sample_select(scores, *, topk, past):
    return pl.pallas_call(
        functools.partial(_sample_select_body, topk=topk, past=past),
        out_shape=jax.ShapeDtypeStruct(scores.shape, F32),
        scratch_shapes=[pltpu.VMEM(scores.shape, F32)],
        compiler_params=pltpu.CompilerParams(vmem_limit_bytes=_VMEM_LIMIT),
        name="sample_select",
    )(scores)


def _sample_seq_attend(rb_ref, seq_id, q_ref, mask_ref, kn_ref, vn_ref, sg_ref, ts_ref, o_ref,
                       kbuf_ref, vbuf_ref, m_ref, l_ref, acc_ref, copies,
                       *, n_chunks, cpages, page, n_slots, n_kv, rep, t_dec, n_buckets):
    hd = q_ref.shape[-1]
    rows = rep * t_dec
    ck = cpages * page
    past = n_chunks * ck

    m_ref[...] = jnp.full(m_ref.shape, -jnp.inf, F32)
    l_ref[...] = jnp.zeros(l_ref.shape, F32)
    acc_ref[...] = jnp.zeros(acc_ref.shape, F32)

    def update(g, logits, mask, values):
        n = logits.shape[-1]
        x = jnp.where(jnp.broadcast_to(mask[None], (rep, t_dec, n)).reshape(rows, n) > 0.5, logits, -jnp.inf)
        rs = slice(g * rows, (g + 1) * rows)
        m_old = m_ref[rs, :]
        m_new = jnp.maximum(m_old, jnp.max(x, axis=-1, keepdims=True))
        m_safe = jnp.where(m_new == -jnp.inf, 0.0, m_new)
        alpha = jnp.exp(m_old - m_safe)
        p = jnp.exp(x - m_safe[:, 0:1])
        l_ref[rs, :] = alpha * l_ref[rs, :] + jnp.sum(p, axis=-1, keepdims=True)
        acc_ref[rs, :] = alpha * acc_ref[rs, :] + jnp.dot(p.astype(_MXU_DTYPE), values.astype(_MXU_DTYPE),
                                                          preferred_element_type=F32)
        m_ref[rs, :] = m_new

    def head_bias(g, kind):
        return ts_ref[g * rep:(g + 1) * rep, kind].reshape(rows, _LANES)

    def far_bias(g):
        r_idx = lax.broadcasted_iota(I32, (rows, 1), 0) // t_dec
        out = jnp.zeros((rows, 1), F32)
        for r in range(rep):
            out = jnp.where(r_idx == r, rb_ref[n_buckets - 1, g * rep + r], out)
        return out

    def queries(g):
        return q_ref[g * rep:(g + 1) * rep].reshape(rows, hd).astype(_MXU_DTYPE)

    zq = jnp.zeros((rows, hd), _MXU_DTYPE)
    q_bd = jnp.concatenate([jnp.concatenate([queries(g) if gg == g else zq for gg in range(n_kv)], axis=1)
                            for g in range(n_kv)], axis=0)

    for c in range(n_chunks):
        slot = c % n_slots
        for cp in copies(seq_id, c):
            cp.wait()
        keys = jnp.concatenate([kbuf_ref[slot, pl.ds(g, ck, stride=n_kv), :].astype(_MXU_DTYPE)
                                for g in range(n_kv)], axis=1)
        lg_all = _nt_dot(keys, q_bd).T
        for g in range(n_kv):
            far = far_bias(g)
            lg = lg_all[g * rows:(g + 1) * rows, :] + far
            if c == n_chunks - 1:
                lg = jnp.concatenate([lg[:, :ck - page], lg[:, ck - page:] + (head_bias(g, 0) - far)], axis=1)
            update(g, lg, mask_ref[:, c * ck:(c + 1) * ck], vbuf_ref[slot, pl.ds(g, ck, stride=n_kv), :])
        if c + n_slots < n_chunks:
            for cp in copies(seq_id, c + n_slots):
                cp.start()

    zpad = jnp.zeros((page - t_dec, hd), F32)
    for g in range(n_kv):
        kn = jnp.concatenate([kn_ref[pl.ds(g, t_dec, stride=n_kv), :], zpad], axis=0)
        vn = jnp.concatenate([vn_ref[pl.ds(g, t_dec, stride=n_kv), :], zpad], axis=0)
        lg = _nt_dot(queries(g), kn.astype(_MXU_DTYPE)) + head_bias(g, 1)
        update(g, lg, mask_ref[:, past:past + page], vn)
        rs = slice(g * rows, (g + 1) * rows)
        o = acc_ref[rs, :] / l_ref[rs, :] * sg_ref[g * rep:(g + 1) * rep].reshape(rows, hd)
        o_ref[g * rep:(g + 1) * rep] = o.reshape(rep, t_dec, hd)


def _attn_body(pt_ref, rb_ref, qi_ref, wk_ref, ki_ref, q_ref, k_ref, v_ref, sg_ref, tb_ref,
               qs_ref, ms_ref, kn_ref, vn_ref, sgs_ref, ts_ref, ck_ref, cv_ref, o_ref, os_ref,
               wt_ref, sc_ref, keys_ref, madd_ref, m_ref, l_ref, acc_ref,
               kbuf_ref, vbuf_ref, sm_ref, sl_ref, sacc_ref, ksem_ref, vsem_ref, *, prompt, sample, n_seq):
    seq_id = pl.program_id(0) * pl.num_programs(1) + pl.program_id(1)
    has_seq = seq_id < n_seq
    prow = sample["page"] * sample["n_kv"]
    cpages, n_slots = sample["cpages"], sample["n_slots"]

    def copies(sid, c):
        return (_page_copies(pt_ref, ck_ref, kbuf_ref, ksem_ref, sid, c * cpages, cpages, c % n_slots, prow)
                + _page_copies(pt_ref, cv_ref, vbuf_ref, vsem_ref, sid, c * cpages, cpages, c % n_slots, prow))

    @pl.when(has_seq)
    def _():
        for c in range(min(n_slots, sample["n_chunks"])):
            for cp in copies(seq_id, c):
                cp.start()

    _prompt_attn_body(qi_ref, wk_ref, ki_ref, q_ref, k_ref, v_ref, sg_ref, tb_ref, o_ref,
                      wt_ref, sc_ref, keys_ref, madd_ref, m_ref, l_ref, acc_ref, **prompt)

    @pl.when(has_seq)
    def _():
        _sample_seq_attend(rb_ref, seq_id, qs_ref, ms_ref, kn_ref, vn_ref, sgs_ref, ts_ref, os_ref,
                           kbuf_ref, vbuf_ref, sm_ref, sl_ref, sacc_ref, copies, **sample)


def _attention(page_table, rel_bias, qi_hm, wk, ki_bf, q_hm, k_hm, v_hm, sg_hm, tb_p,
               q_s, mask_s, kn, vn, sg_s, tb_s, cache_k, cache_v, *, batch, seq, tq, t_dec, page, cpages, n_slots):
    idx_heads, m, idx_dim = qi_hm.shape
    n_heads, _, hd = q_hm.shape
    n_kv = k_hm.shape[0]
    rep = n_heads // n_kv
    n_q = seq // tq
    n_seq, n_pages = page_table.shape
    assert n_seq <= batch * n_q and n_pages % cpages == 0
    n_chunks = n_pages // cpages
    ck = cpages * page
    once = dict(pipeline_mode=pl.Buffered(1))
    qrow = lambda b, i, pt: (0, b * n_q + i, 0)
    srow = lambda b, i, pt: (0, jnp.minimum(b * n_q + i, n_seq - 1), 0)
    srow2 = lambda b, i, pt: (jnp.minimum(b * n_q + i, n_seq - 1), 0)
    prompt = dict(tq=tq, topk=min(_TOPK_MAX, seq // 4), idx_heads=idx_heads, n_kv=n_kv, rep=rep)
    sample = dict(n_chunks=n_chunks, cpages=cpages, page=page, n_slots=n_slots, n_kv=n_kv, rep=rep,
                  t_dec=t_dec, n_buckets=rel_bias.shape[0])
    grid_spec = pltpu.PrefetchScalarGridSpec(
        num_scalar_prefetch=1,
        grid=(batch, n_q),
        in_specs=[pl.BlockSpec(memory_space=pltpu.SMEM),
                  pl.BlockSpec((idx_heads, tq, idx_dim), qrow),
                  pl.BlockSpec((tq, _LANES), lambda b, i, pt: (b * n_q + i, 0)),
                  pl.BlockSpec((seq, idx_dim), lambda b, i, pt: (b, 0), **once),
                  pl.BlockSpec((n_heads, tq, hd), qrow),
                  pl.BlockSpec((n_kv, seq, hd), lambda b, i, pt: (0, b, 0), **once),
                  pl.BlockSpec((n_kv, seq, hd), lambda b, i, pt: (0, b, 0), **once),
                  pl.BlockSpec((n_heads, tq, hd), qrow),
                  pl.BlockSpec((n_heads, 2, _LANES, _LANES), lambda b, i, pt: (0, 0, 0, 0), **once),
                  pl.BlockSpec((n_heads, t_dec, hd), srow),
                  pl.BlockSpec((t_dec, mask_s.shape[1]), srow2),
                  pl.BlockSpec((t_dec * n_kv, hd), srow2),
                  pl.BlockSpec((t_dec * n_kv, hd), srow2),
                  pl.BlockSpec((n_heads, t_dec, hd), srow),
                  pl.BlockSpec((n_heads, 2, t_dec, _LANES), lambda b, i, pt: (0, 0, 0, 0), **once),
                  pl.BlockSpec(memory_space=pl.ANY),
                  pl.BlockSpec(memory_space=pl.ANY)],
        out_specs=[pl.BlockSpec((n_heads, tq, hd), qrow),
                   pl.BlockSpec((n_heads, t_dec, hd), srow)],
        scratch_shapes=[pltpu.VMEM((_LANES, tq), F32),
                        pltpu.VMEM((seq, tq), F32),
                        pltpu.VMEM((seq, tq), F32),
                        pltpu.VMEM((n_q, tq, tq), F32),
                        pltpu.VMEM((n_kv, rep * tq, _LANES), F32),
                        pltpu.VMEM((n_kv, rep * tq, _LANES), F32),
                        pltpu.VMEM((n_kv, rep * tq, hd), F32),
                        pltpu.VMEM((n_slots, ck * n_kv, hd), F32),
                        pltpu.VMEM((n_slots, ck * n_kv, hd), F32),
                        pltpu.VMEM((n_heads * t_dec, _LANES), F32),
                        pltpu.VMEM((n_heads * t_dec, _LANES), F32),
                        pltpu.VMEM((n_heads * t_dec, hd), F32),
                        pltpu.SemaphoreType.DMA((n_slots,)),
                        pltpu.SemaphoreType.DMA((n_slots,))],
    )
    return pl.pallas_call(
        functools.partial(_attn_body, prompt=prompt, sample=sample, n_seq=n_seq),
        grid_spec=grid_spec,
        out_shape=[jax.ShapeDtypeStruct((n_heads, m, hd), _MXU_DTYPE),
                   jax.ShapeDtypeStruct((n_heads, n_seq * t_dec, hd), F32)],
        compiler_params=_cparams(2),
        name="attention",
    )(page_table, rel_bias.astype(F32), qi_hm, wk, ki_bf, q_hm, k_hm, v_hm, sg_hm, tb_p,
      q_s, mask_s, kn, vn, sg_s, tb_s, cache_k, cache_v)


def _pool_body(u_ref, halo_ref, gb_ref, wp_ref, ps_ref, o_ref, ext_ref, *, tm, pos_base, own_halo, pg):
    i = pl.program_id(1)
    bb = u_ref.shape[0]
    halo = halo_ref[...]
    if own_halo:
        halo = jnp.where(i > 0, halo, 0.0)
    ext_ref[:, 0:_POOL_HALO, :] = halo
    ext_ref[:, _POOL_HALO:, :] = u_ref[...]
    pos = pos_base + i * tm + lax.broadcasted_iota(I32, (1, tm, 1), 1)
    for g, win in enumerate(_POOL_WINDOWS):
        cols = slice(g * pg, (g + 1) * pg)
        tot = ext_ref[:, _POOL_HALO:, cols]
        for k in range(1, win):
            tot = tot + ext_ref[:, _POOL_HALO - k:_POOL_HALO - k + tm, cols]
        cnt = jnp.minimum(pos + 1, win).astype(F32)
        d = tot / cnt - ext_ref[:, _POOL_HALO:, cols]
        y = jnp.dot(d.reshape(bb * tm, pg).astype(_MXU_DTYPE), wp_ref[g], preferred_element_type=F32)
        y = y.reshape(bb, tm, pg) * ps_ref[:, cols] * gb_ref[:, :, cols].astype(F32)
        o_ref[:, :, cols] = y.astype(o_ref.dtype)


def _pool(u, halo, gb, wp, ps, *, bb, tm, pos_base, own_halo, out_dtype):
    n_seq, rows, pw = u.shape
    pg = pw // len(_POOL_WINDOWS)
    if own_halo:
        hb = tm // _POOL_HALO
        halo_spec = pl.BlockSpec((bb, _POOL_HALO, pw), lambda b, i: (b, jnp.maximum(i * hb - 1, 0), 0))
    else:
        halo_spec = pl.BlockSpec((bb, _POOL_HALO, pw), lambda b, i: (b, 0, 0))
    return pl.pallas_call(
        functools.partial(_pool_body, tm=tm, pos_base=pos_base, own_halo=own_halo, pg=pg),
        grid=(n_seq // bb, rows // tm),
        in_specs=[pl.BlockSpec((bb, tm, pw), lambda b, i: (b, i, 0)),
                  halo_spec,
                  pl.BlockSpec((bb, tm, pw), lambda b, i: (b, i, 0)),
                  pl.BlockSpec(wp.shape, lambda b, i: (0, 0, 0)),
                  pl.BlockSpec((1, pw), lambda b, i: (0, 0))],
        out_specs=pl.BlockSpec((bb, tm, pw), lambda b, i: (b, i, 0)),
        out_shape=jax.ShapeDtypeStruct((n_seq, rows, pw), out_dtype),
        scratch_shapes=[pltpu.VMEM((bb, _POOL_HALO + tm, pw), F32)],
        compiler_params=_cparams(2),
        name="pool",
    )(u, halo, gb, wp, ps)


def _merge_body(ma_ref, mb_ref, w_ref, x_ref, g_ref, o_ref, mix_ref, *, n_j, tn):
    j = pl.program_id(1)
    n_heads, _, hd = ma_ref.shape

    @pl.when(j == 0)
    def _():
        for h in range(n_heads):
            mix_ref[:, h * hd:(h + 1) * hd] = ma_ref[h].astype(mix_ref.dtype)
        mix_ref[:, n_heads * hd:] = mb_ref[...].astype(mix_ref.dtype)

    hv = x_ref[...] + jnp.dot(mix_ref[...], w_ref[...], preferred_element_type=F32)
    for jj in range(n_j):
        @pl.when(j == jj)
        def _(jj=jj):
            o_ref[:, jj * tn:(jj + 1) * tn] = hv

    @pl.when(j == n_j - 1)
    def _():
        ssq = jnp.zeros((o_ref.shape[0], 1), F32)
        for jj in range(n_j):
            hj = o_ref[:, jj * tn:(jj + 1) * tn]
            ssq = ssq + jnp.sum(hj * hj, axis=-1, keepdims=True)
        inv = lax.rsqrt(ssq / (n_j * tn) + _RMS_EPS)
        for jj in range(n_j):
            cols = slice(jj * tn, (jj + 1) * tn)
            o_ref[:, cols] = o_ref[:, cols] * inv * g_ref[:, cols]


def _merge(mix_a, mix_b, w_out, x2d, final_g, *, tm, tn):
    n_heads, m, hd = mix_a.shape
    pw = mix_b.shape[1]
    mixw, d = w_out.shape
    n_j = d // tn
    return pl.pallas_call(
        functools.partial(_merge_body, n_j=n_j, tn=tn),
        grid=(m // tm, n_j),
        in_specs=[pl.BlockSpec((n_heads, tm, hd), lambda i, j: (0, i, 0)),
                  pl.BlockSpec((tm, pw), lambda i, j: (i, 0)),
                  pl.BlockSpec((mixw, tn), lambda i, j: (0, j)),
                  pl.BlockSpec((tm, tn), lambda i, j: (i, j)),
                  pl.BlockSpec((1, d), lambda i, j: (0, 0))],
        out_specs=pl.BlockSpec((tm, d), lambda i, j: (i, 0)),
        out_shape=jax.ShapeDtypeStruct((m, d), F32),
        scratch_shapes=[pltpu.VMEM((tm, mixw), _MXU_DTYPE)],
        compiler_params=_cparams(2),
        name="merge",
    )(mix_a, mix_b, w_out, x2d, final_g.reshape(1, d).astype(F32))


def kernel(x_prompt, x_sample, cache_k, cache_v, cache_idx_k, state_pool, page_table,
           rel_bias, ln_g, w_in, w_pool, pool_scale, w_out, final_g):
    depth = w_in.shape[0]
    assert depth == 1, "single-layer step only"
    batch, seq, d_model = x_prompt.shape
    bd, t_dec, _ = x_sample.shape
    _, n_phys, page, n_kv, hd = cache_k.shape
    idx_dim = cache_idx_k.shape[-1]
    n_buckets, n_heads = rel_bias.shape
    pw = pool_scale.shape[-1]
    n_pg = w_pool.shape[1]
    attn_w, kv_w = n_heads * hd, n_kv * hd
    in_w = w_in.shape[-1]
    idx_heads = (in_w - 2 * attn_w - 2 * kv_w - idx_dim - 2 * pw) // (idx_dim + 1)
    n_pages = page_table.shape[1]
    past = n_pages * page
    mp, ms = batch * seq, bd * t_dec
    tq = _PROMPT_Q
    assert hd == _LANES and idx_dim == _LANES and idx_heads <= _LANES and n_pg == len(_POOL_WINDOWS)
    assert page >= _MAX_DISTANCE and _LANES >= _MAX_DISTANCE and seq % (2 * tq) == 0 and t_dec <= _POOL_HALO - 1
    assert state_pool.shape[2] == _POOL_HALO - 1

    o_q, o_k, o_v = 0, attn_w, attn_w + kv_w
    o_ga = o_v + kv_w
    o_qi = o_ga + attn_w
    o_wi = o_qi + idx_heads * idx_dim
    o_ki = o_wi + idx_heads
    o_u = o_ki + idx_dim
    o_gb = o_u + pw
    wt = jnp.transpose(w_in[0])

    xn_p = _rmsnorm(x_prompt.reshape(mp, d_model), ln_g[0], _MXU_DTYPE)
    xn_s = _rmsnorm(x_sample.reshape(ms, d_model), ln_g[0], _MXU_DTYPE)

    tm = min(_PROJ_ROWS, mp)
    tw = min(_PROJ_ROWS_WIDE, mp)
    tn = lambda n: min(_PROJ_COLS, n)
    rm = lambda n, t, dt, s, rows=tm: _row_major(ms if s else mp, n, rows, t, dt, s)
    hm = lambda n, t, dt, s, rows=tm: _head_major(ms if s else mp, n, rows, t, dt, s)

    def proj(lo, hi, t, rows=tm, **kw):
        assert (hi - lo) % t == 0
        return _proj_call(xn_p, xn_s, wt, [(lo, t)], (hi - lo) // t, rows, **kw)

    def scaled_heads(scale):
        def epi(acc, refs):
            _store_heads(refs[0], acc * scale)
        return epi

    def silu_heads(acc, refs):
        _store_heads(refs[0], _silu(acc))

    def silu_rows(acc, refs):
        refs[0][...] = _silu(acc).astype(refs[0].dtype)

    def plain_rows(acc, refs):
        refs[0][...] = acc.astype(refs[0].dtype)

    def kv_rows(acc, ref):
        for g in range(n_kv):
            ref[pl.ds(g, acc.shape[0], stride=n_kv), :] = acc[:, g * hd:(g + 1) * hd]

    def kv_p(acc, refs):
        kv_rows(acc, refs[0])
        _store_heads(refs[1], acc)

    def kv_s(acc, refs):
        kv_rows(acc, refs[0])

    def wk_p(acc, refs):
        refs[0][...] = jnp.concatenate([acc[:, :_LANES] * idx_heads ** -0.5, acc[:, _LANES:]], axis=1)
        refs[1][...] = acc[:, _LANES:].astype(refs[1].dtype)

    def wk_s(acc, refs):
        refs[0][...] = jnp.concatenate([acc[:, :_LANES] * idx_heads ** -0.5, acc[:, _LANES:]], axis=1)

    t = tn(attn_w)
    q_hm, q_s = proj(o_q, o_k, t, tw, outs_p=[hm(attn_w, t, _MXU_DTYPE, False, tw)],
                     outs_s=[hm(attn_w, t, F32, True)],
                     epi_p=scaled_heads(hd ** -0.5 * _LOG2E), epi_s=scaled_heads(hd ** -0.5), name="proj_q")
    kv_outs = dict(outs_p=[((mp * n_kv, hd), F32, (tm * n_kv, hd), lambda j, i: (i, 0)),
                           ((n_kv, mp, hd), _MXU_DTYPE, (n_kv, tm, hd), lambda j, i: (0, i, 0))],
                   outs_s=[((ms * n_kv, hd), F32, (ms * n_kv, hd), lambda j, i: (0, 0))], epi_p=kv_p, epi_s=kv_s)
    k_p, k_hm, k_s = proj(o_k, o_v, kv_w, name="proj_k", **kv_outs)
    v_p, v_hm, v_s = proj(o_v, o_ga, kv_w, name="proj_v", **kv_outs)
    sga_hm, sga_s = proj(o_ga, o_qi, t, tw, outs_p=[hm(attn_w, t, _MXU_DTYPE, False, tw)],
                         outs_s=[hm(attn_w, t, F32, True)], epi_p=silu_heads, epi_s=silu_heads, name="proj_ga")
    t = tn(idx_heads * idx_dim)
    qi_hm, qi_s = proj(o_qi, o_wi, t, tw, outs_p=[hm(idx_heads * idx_dim, t, _MXU_DTYPE, False, tw)],
                       outs_s=[hm(idx_heads * idx_dim, t, F32, True)],
                       epi_p=scaled_heads(idx_dim ** -0.5), epi_s=scaled_heads(idx_dim ** -0.5), name="proj_qi")
    t = 2 * _LANES
    wk_pr, ki_bf, wk_sm = _proj_call(
        xn_p, xn_s, wt, [(o_wi, _LANES), (o_ki, idx_dim)], 1, tm,
        outs_p=[((mp, t), F32, (tm, t), lambda j, i: (i, 0)), ((mp, idx_dim), _MXU_DTYPE, (tm, idx_dim), lambda j, i: (i, 0))],
        outs_s=[((ms, t), F32, (ms, t), lambda j, i: (0, 0))],
        epi_p=wk_p, epi_s=wk_s, name="proj_wk")
    t = tn(pw)
    u_p, u_s = proj(o_u, o_gb, t, outs_p=[rm(pw, t, F32, False)], outs_s=[rm(pw, t, F32, True)],
                    epi_p=plain_rows, epi_s=plain_rows, name="proj_u")
    sgb_p, sgb_s = proj(o_gb, in_w, t, tw, outs_p=[rm(pw, t, _MXU_DTYPE, False, tw)], outs_s=[rm(pw, t, F32, True)],
                        epi_p=silu_rows, epi_s=silu_rows, name="proj_gb")

    tb_p, tb_s = _bias_tables(rel_bias, _LANES, t_dec, page)

    scores = _sample_scores(page_table, qi_s, wk_sm, cache_idx_k[0], t_dec=t_dec)
    topk_s = min(_TOPK_MAX, (past + t_dec) // 4)
    mask_s = _sample_select(scores, topk=topk_s, past=past).reshape(ms, past + page)
    cpages = max(1, min(_PAGES_PER_CHUNK, n_pages // 4))
    mixa_p, mixa_s = _attention(
        page_table, rel_bias, qi_hm, wk_pr, ki_bf, q_hm, k_hm, v_hm, sga_hm, tb_p,
        q_s, mask_s, k_s, v_s, sga_s, tb_s,
        cache_k[0].reshape(n_phys, page * n_kv, hd), cache_v[0].reshape(n_phys, page * n_kv, hd),
        batch=batch, seq=seq, tq=tq, t_dec=t_dec, page=page, cpages=cpages, n_slots=_PAGE_SLOTS)

    wp = w_pool[0].astype(_MXU_DTYPE)
    ps = pool_scale[0].reshape(1, pw).astype(F32)
    u_p3 = u_p.reshape(batch, seq, pw)
    mixb_p = _pool(u_p3, u_p3, sgb_p.reshape(batch, seq, pw), wp, ps, bb=1, tm=min(_POOL_ROWS, seq), pos_base=0,
                   own_halo=True, out_dtype=_MXU_DTYPE).reshape(mp, pw)
    halo_s = jnp.concatenate([jnp.zeros((bd, 1, pw), F32), state_pool[0].astype(F32)], axis=1)
    mixb_s = _pool(u_s.reshape(bd, t_dec, pw), halo_s, sgb_s.reshape(bd, t_dec, pw), wp, ps, bb=bd, tm=t_dec,
                   pos_base=past, own_halo=False, out_dtype=F32).reshape(ms, pw)

    w_o = w_out[0].astype(_MXU_DTYPE)
    tn_o = min(_MERGE_COLS, d_model)
    y_p = _merge(mixa_p, mixb_p, w_o, x_prompt.reshape(mp, d_model), final_g, tm=min(_MERGE_ROWS, mp), tn=tn_o)
    y_s = _merge(mixa_s, mixb_s, w_o, x_sample.reshape(ms, d_model), final_g, tm=ms, tn=tn_o)

    n_state = state_pool.shape[2]
    u_full_s = jnp.concatenate([state_pool[0].astype(F32), u_s.reshape(bd, t_dec, pw)], axis=1)
    return (y_p.reshape(batch, seq, d_model), y_s.reshape(bd, t_dec, d_model),
            k_p.reshape(1, batch, seq, n_kv, hd), v_p.reshape(1, batch, seq, n_kv, hd),
            wk_pr[:, _LANES:].reshape(1, batch, seq, idx_dim),
            u_p.reshape(batch, seq, pw)[:, seq - n_state:][None],
            k_s.reshape(1, bd, t_dec, n_kv, hd), v_s.reshape(1, bd, t_dec, n_kv, hd),
            wk_sm[:, _LANES:].reshape(1, bd, t_dec, idx_dim),
            u_full_s[:, t_dec:][None])
```
